```python
import jax, jax.numpy as jnp
from jax import lax
import numpy as np

D_MODEL = 1024
BATCH = 8
SEQ = 4096
DEPTH = 2

D_MIX = D_MODEL
N_MIXERS = 4
G_WIDTH = D_MIX // N_MIXERS
HEAD_DIM = 64
N_GROUP_HEADS = G_WIDTH // HEAD_DIM
D_FF = 2816
RMS_EPS = 1e-6
POOL_WINDOWS = (2, 4, 8, 16)
POOL_CH = G_WIDTH // len(POOL_WINDOWS)
RWKV_HEADS = N_GROUP_HEADS
RWKV_N = HEAD_DIM
RWKV_W_RANK = 64
RWKV_A_RANK = 32
RWKV_G_RANK = 64
RWKV_SIZES = (G_WIDTH, G_WIDTH, G_WIDTH, RWKV_W_RANK, RWKV_A_RANK, RWKV_G_RANK)
RWKV_COLS = sum(RWKV_SIZES)
RWKV_LN_EPS = 64e-5
NSA_HEADS = N_GROUP_HEADS
NSA_CMP_LEN = 32
NSA_CMP_STRIDE = 16
NSA_CMP_HIDDEN = 128
NSA_SEL_BLOCK = 64
NSA_TOP_N = 16
NSA_WINDOW = 512
NSA_QBLK = 64
NSA_FORCE_BONUS = 1e4
NSA_NEG = -1e9
NSA_SIZES = (G_WIDTH,) + (HEAD_DIM,) * 6 + (3 * NSA_HEADS,)
NSA_COLS = sum(NSA_SIZES)
CONV_WIDTH = 3
CONV_COLS = 3 * G_WIDTH
P_SIZES = (G_WIDTH, RWKV_COLS, NSA_COLS, CONV_COLS)
P_TOTAL = sum(P_SIZES)

kernel_name = "hybrid_parallel_heads_block"


def _splits(sizes):
    return [int(s) for s in np.cumsum(sizes)[:-1]]


def rms_norm(x, g, eps=RMS_EPS):
    xf = x.astype(jnp.float32)
    y = xf * lax.rsqrt(jnp.mean(xf * xf, axis=-1, keepdims=True) + eps)
    return (y * g.astype(jnp.float32)).astype(x.dtype)


def swiglu_ffn(x, g, w_gate, w_up, w_down):
    h = rms_norm(x, g)
    return (jax.nn.silu(h @ w_gate) * (h @ w_up)) @ w_down


def token_shift(u):
    return jnp.pad(u, ((0, 0), (1, 0), (0, 0)))[:, :-1]


def alibi_slopes(h):
    return 2.0 ** (-8.0 * jnp.arange(1, h + 1, dtype=jnp.float32) / h)


def pool_mixer(u, pool_w, pool_scale):
    B, S, _ = u.shape
    c = jnp.pad(jnp.cumsum(u.astype(jnp.float32), axis=1), ((0, 0), (1, 0), (0, 0)))
    t = jnp.arange(S)
    outs = []
    for gi, w in enumerate(POOL_WINDOWS):
        sl = slice(gi * POOL_CH, (gi + 1) * POOL_CH)
        lo = jnp.maximum(t + 1 - w, 0)
        total = c[:, 1:, sl] - c[:, lo, sl]
        cnt = (t + 1 - lo).astype(jnp.float32)[None, :, None]
        outs.append(total / cnt)
    pooled = jnp.stack(outs, axis=2).astype(u.dtype)
    ug = u.reshape(B, S, len(POOL_WINDOWS), POOL_CH)
    y = jnp.einsum('bsgc,gcd->bsgd', pooled - ug, pool_w).reshape(B, S, G_WIDTH)
    return y * pool_scale


def _rwkv_step(state, inp):
    r_t, k_t, v_t, w_t, kk_t, akk_t = inp
    sa = jnp.einsum('bhvk,bhk->bhv', state, kk_t)
    state = (state * w_t[:, :, None, :]
             - sa[..., None] * akk_t[:, :, None, :]
             + v_t[..., None] * k_t[:, :, None, :])
    y = jnp.einsum('bhvk,bhk->bhv', state, r_t)
    return state, y


def rwkv7_mixer(p, mu, w0, w_up, a0, a_up, g_up, k_k, k_a, r_k, ln_w, ln_b):
    B, S, _ = p.shape
    H, N = RWKV_HEADS, RWKV_N
    p = p + mu * (token_shift(p) - p)
    r, k, v, wd, ad, gd = jnp.split(p, _splits(RWKV_SIZES), axis=-1)
    w = (w0 + jnp.tanh(wd) @ w_up).astype(jnp.float32)
    decay = jnp.exp(-jnp.exp(-jax.nn.softplus(-w) - 0.5))
    a = jax.nn.sigmoid(a0 + ad @ a_up)
    g = jax.nn.sigmoid(gd) @ g_up
    kk = (k * k_k).reshape(B, S, H, N).astype(jnp.float32)
    kk = kk * lax.rsqrt(jnp.maximum(jnp.sum(kk * kk, axis=-1, keepdims=True), 1e-24))
    k = k * (1 + (a - 1) * k_a)

    def heads(z):
        return z.reshape(B, S, H, N).astype(jnp.float32)

    r_h, k_h, v_h, a_h = heads(r), heads(k), heads(v), heads(a)
    xs = tuple(jnp.swapaxes(z, 0, 1) for z in (r_h, k_h, v_h, heads(decay), kk, kk * a_h))
    state0 = jnp.zeros((B, H, N, N), jnp.float32)
    _, ys = lax.scan(_rwkv_step, state0, xs)
    y = jnp.swapaxes(ys, 0, 1)
    mean = jnp.mean(y, axis=-1, keepdims=True)
    var = jnp.mean(jnp.square(y - mean), axis=-1, keepdims=True)
    y = ((y - mean) * lax.rsqrt(var + RWKV_LN_EPS)).reshape(B, S, G_WIDTH) * ln_w + ln_b
    bonus = jnp.sum(r_h * k_h * r_k, axis=-1, keepdims=True) * v_h
    out = (y + bonus.reshape(B, S, G_WIDTH)) * g
    return out.astype(p.dtype)


def nsa_mixer(p, q_norm_w, k_norm_w, cmp_pos, cmp_k_w1, cmp_k_w2, cmp_v_w1, cmp_v_w2):
    B, S, _ = p.shape
    H, Dh = NSA_HEADS, HEAD_DIM
    q, kc, vc, ksl, vsl, kwn, vwn, gates = jnp.split(p, _splits(NSA_SIZES), axis=-1)
    q = rms_norm(q.reshape(B, S, H, Dh), q_norm_w)
    scale = Dh ** -0.5
    slopes = alibi_slopes(H)
    t = jnp.arange(S)

    n_cmp = (S - NSA_CMP_LEN) // NSA_CMP_STRIDE + 1
    cmp_start = jnp.arange(n_cmp) * NSA_CMP_STRIDE
    idx = cmp_start[:, None] + jnp.arange(NSA_CMP_LEN)[None]

    def compress(u, w1, w2):
        blocks = u[:, idx] + cmp_pos
        return jax.nn.gelu(blocks.reshape(B, n_cmp, NSA_CMP_LEN * Dh) @ w1) @ w2

    k_cmp = rms_norm(compress(kc, cmp_k_w1, cmp_k_w2), k_norm_w[0])
    v_cmp = compress(vc, cmp_v_w1, cmp_v_w2)
    blk_end = cmp_start + NSA_CMP_LEN - 1
    dist = (t[:, None] - blk_end[None]).astype(jnp.float32)
    valid = dist >= 0
    s = jnp.einsum('bshd,bnd->bhsn', q, k_cmp).astype(jnp.float32) * scale - slopes[:, None, None] * dist
    s = jnp.where(valid, s, NSA_NEG)
    p_cmp = jax.nn.softmax(s, axis=-1) * valid
    o_cmp = jnp.einsum('bhsn,bnd->bshd', p_cmp.astype(v_cmp.dtype), v_cmp)

    n_sel = S // NSA_SEL_BLOCK
    n_top = min(NSA_TOP_N, n_sel)
    sel_start = jnp.arange(n_sel) * NSA_SEL_BLOCK
    overlap = ((cmp_start[:, None] <= sel_start[None] + NSA_SEL_BLOCK - 1)
               & (blk_end[:, None] >= sel_start[None])).astype(jnp.float32)
    imp = jnp.einsum('bhsn,nj->bsj', p_cmp, overlap)
    cur = t // NSA_SEL_BLOCK
    j = jnp.arange(n_sel)
    sel_ok = j[None] <= cur[:, None]
    forced = (j[None] == 0) | (j[None] == cur[:, None]) | (j[None] == cur[:, None] - 1)
    imp = jnp.where(sel_ok, imp + jnp.where(forced, NSA_FORCE_BONUS, 0.0), -1.0)
    _, sel_idx = lax.top_k(imp, n_top)

    k_sel_blocks = rms_norm(ksl, k_norm_w[1]).reshape(B, n_sel, NSA_SEL_BLOCK, Dh)
    v_sel_blocks = vsl.reshape(B, n_sel, NSA_SEL_BLOCK, Dh)
    pad = ((0, 0), (NSA_WINDOW, 0), (0, 0))
    k_win_p = jnp.pad(rms_norm(kwn, k_norm_w[2]), pad)
    v_win_p = jnp.pad(vwn, pad)

    n_qb = S // NSA_QBLK
    q_b = q.reshape(B, n_qb, NSA_QBLK, H, Dh).transpose(1, 0, 2, 3, 4)
    idx_b = sel_idx.reshape(B, n_qb, NSA_QBLK, n_top).transpose(1, 0, 2, 3)
    win_len = NSA_WINDOW + NSA_QBLK

    def block_fn(args):
        i, q_i, idx_i = args
        t_i = i * NSA_QBLK + jnp.arange(NSA_QBLK)
        k_g = jax.vmap(lambda kb, ib: kb[ib])(k_sel_blocks, idx_i)
        v_g = jax.vmap(lambda vb, ib: vb[ib])(v_sel_blocks, idx_i)
        pos = idx_i[..., None] * NSA_SEL_BLOCK + jnp.arange(NSA_SEL_BLOCK)
        d = (t_i[None, :, None, None] - pos).astype(jnp.float32)
        ss = (jnp.einsum('bqhd,bqnkd->bhqnk', q_i, k_g).astype(jnp.float32) * scale
              - slopes[None, :, None, None, None] * d[:, None])
        ss = jnp.where((d >= 0)[:, None], ss, NSA_NEG).reshape(B, H, NSA_QBLK, n_top * NSA_SEL_BLOCK)
        ps = jax.nn.softmax(ss, axis=-1).reshape(B, H, NSA_QBLK, n_top, NSA_SEL_BLOCK)
        o_sel = jnp.einsum('bhqnk,bqnkd->bqhd', ps.astype(v_g.dtype), v_g)
        kw_i = lax.dynamic_slice_in_dim(k_win_p, i * NSA_QBLK, win_len, axis=1)
        vw_i = lax.dynamic_slice_in_dim(v_win_p, i * NSA_QBLK, win_len, axis=1)
        s_pos = i * NSA_QBLK - NSA_WINDOW + jnp.arange(win_len)
        dw = (t_i[:, None] - s_pos[None]).astype(jnp.float32)
        okw = (dw >= 0) & (dw < NSA_WINDOW) & (s_pos[None] >= 0)
        sw = (jnp.einsum('bqhd,bkd->bhqk', q_i, kw_i).astype(jnp.float32) * scale
              - slopes[None, :, None, None] * dw[None, None])
        sw = jnp.where(okw, sw, NSA_NEG)
        pw = jax.nn.softmax(sw, axis=-1)
        o_win = jnp.einsum('bhqk,bkd->bqhd', pw.astype(vw_i.dtype), vw_i)
        return o_sel, o_win

    o_sel, o_win = lax.map(block_fn, (jnp.arange(n_qb), q_b, idx_b))
    o_sel = o_sel.transpose(1, 0, 2, 3, 4).reshape(B, S, H, Dh)
    o_win = o_win.transpose(1, 0, 2, 3, 4).reshape(B, S, H, Dh)
    gt = jax.nn.sigmoid(gates).reshape(B, S, H, 3)
    out = gt[..., 0:1] * o_cmp + gt[..., 1:2] * o_sel + gt[..., 2:3] * o_win
    return out.reshape(B, S, G_WIDTH)


def short_conv_mixer(p, conv_w):
    u, b, c = jnp.split(p, 3, axis=-1)
    z = c * u
    S = z.shape[1]
    zp = jnp.pad(z, ((0, 0), (CONV_WIDTH - 1, 0), (0, 0)))
    y = conv_w[0] * zp[:, 0:S]
    for jw in range(1, CONV_WIDTH):
        y = y + conv_w[jw] * zp[:, jw:jw + S]
    return b * y


def setup_inputs(seed: int = 0) -> dict:
    key = jax.random.key(seed)
    keys = iter(jax.random.split(key, 48))
    f32 = jnp.float32

    def nrm(shape, scale):
        return scale * jax.random.normal(next(keys), shape, f32)

    def gain(shape):
        return 1.0 + nrm(shape, 0.02)

    def unif(shape, lo, hi):
        return jax.random.uniform(next(keys), shape, f32, lo, hi)

    L, D, F, G, Dh = DEPTH, D_MODEL, D_FF, G_WIDTH, HEAD_DIM
    cmp_in = NSA_CMP_LEN * Dh
    return {
        "x": nrm((BATCH, SEQ, D), 1.0),
        "ffn1_norm": gain((L, D)),
        "ffn1_w_gate": nrm((L, D, F), D ** -0.5),
        "ffn1_w_up": nrm((L, D, F), D ** -0.5),
        "ffn1_w_down": nrm((L, F, D), F ** -0.5),
        "mix_norm": gain((L, D)),
        "w_in": nrm((L, D, P_TOTAL), D ** -0.5),
        "pool_w": nrm((L, len(POOL_WINDOWS), POOL_CH, POOL_CH), POOL_CH ** -0.5),
        "pool_scale": 1.0 + nrm((L, G), 0.1),
        "rwkv_mu": unif((L, RWKV_COLS), 0.0, 1.0),
        "rwkv_w0": unif((L, G), -6.0, -1.0),
        "rwkv_w_up": nrm((L, RWKV_W_RANK, G), 0.1 * RWKV_W_RANK ** -0.5),
        "rwkv_a0": nrm((L, G), 0.3),
        "rwkv_a_up": nrm((L, RWKV_A_RANK, G), 0.1 * RWKV_A_RANK ** -0.5),
        "rwkv_g_up": nrm((L, RWKV_G_RANK, G), RWKV_G_RANK ** -0.5),
        "rwkv_k_k": 0.85 + nrm((L, G), 0.1),
        "rwkv_k_a": 1.0 + nrm((L, G), 0.1),
        "rwkv_r_k": nrm((L, RWKV_HEADS, RWKV_N), 0.1),
        "rwkv_ln_w": gain((L, G)),
        "rwkv_ln_b": nrm((L, G), 0.02),
        "nsa_q_norm": gain((L, Dh)),
        "nsa_k_norm": gain((L, 3, Dh)),
        "nsa_cmp_pos": nrm((L, NSA_CMP_LEN, Dh), 0.1),
        "nsa_cmp_k_w1": nrm((L, cmp_in, NSA_CMP_HIDDEN), cmp_in ** -0.5),
        "nsa_cmp_k_w2": nrm((L, NSA_CMP_HIDDEN, Dh), NSA_CMP_HIDDEN ** -0.5),
        "nsa_cmp_v_w1": nrm((L, cmp_in, NSA_CMP_HIDDEN), cmp_in ** -0.5),
        "nsa_cmp_v_w2": nrm((L, NSA_CMP_HIDDEN, Dh), NSA_CMP_HIDDEN ** -0.5),
        "conv_w": nrm((L, CONV_WIDTH, G), CONV_WIDTH ** -0.5),
        "w_out": nrm((L, D_MIX, D), D_MIX ** -0.5),
        "ffn2_norm": gain((L, D)),
        "ffn2_w_gate": nrm((L, D, F), D ** -0.5),
        "ffn2_w_up": nrm((L, D, F), D ** -0.5),
        "ffn2_w_down": nrm((L, F, D), F ** -0.5),
    }


def reference(x, ffn1_norm, ffn1_w_gate, ffn1_w_up, ffn1_w_down, mix_norm, w_in,
              pool_w, pool_scale,
              rwkv_mu, rwkv_w0, rwkv_w_up, rwkv_a0, rwkv_a_up, rwkv_g_up, rwkv_k_k, rwkv_k_a,
              rwkv_r_k, rwkv_ln_w, rwkv_ln_b,
              nsa_q_norm, nsa_k_norm, nsa_cmp_pos, nsa_cmp_k_w1, nsa_cmp_k_w2, nsa_cmp_v_w1,
              nsa_cmp_v_w2,
              conv_w, w_out, ffn2_norm, ffn2_w_gate, ffn2_w_up, ffn2_w_down):
    for l in range(DEPTH):
        x = x + 0.5 * swiglu_ffn(x, ffn1_norm[l], ffn1_w_gate[l], ffn1_w_up[l], ffn1_w_down[l])
        h = rms_norm(x, mix_norm[l])
        p = h @ w_in[l]
        p_a, p_b, p_c, p_d = jnp.split(p, _splits(P_SIZES), axis=-1)
        y_a = pool_mixer(p_a, pool_w[l], pool_scale[l])
        y_b = rwkv7_mixer(p_b, rwkv_mu[l], rwkv_w0[l], rwkv_w_up[l], rwkv_a0[l], rwkv_a_up[l],
                          rwkv_g_up[l], rwkv_k_k[l], rwkv_k_a[l], rwkv_r_k[l], rwkv_ln_w[l],
                          rwkv_ln_b[l])
        y_c = nsa_mixer(p_c, nsa_q_norm[l], nsa_k_norm[l], nsa_cmp_pos[l], nsa_cmp_k_w1[l],
                        nsa_cmp_k_w2[l], nsa_cmp_v_w1[l], nsa_cmp_v_w2[l])
        y_d = short_conv_mixer(p_d, conv_w[l])
        y = jnp.concatenate([y_a, y_b.astype(y_a.dtype), y_c.astype(y_a.dtype), y_d], axis=-1)
        x = x + y @ w_out[l]
        x = x + 0.5 * swiglu_ffn(x, ffn2_norm[l], ffn2_w_gate[l], ffn2_w_up[l], ffn2_w_down[l])
    return x
```

```python
import functools

import jax
import jax.numpy as jnp
from jax import lax
from jax.experimental import pallas as pl
from jax.experimental.pallas import tpu as pltpu

F32 = jnp.float32
BF16 = jnp.bfloat16

N_MIXERS = 4
HEAD_DIM = 64
N_HEADS = 4
G_WIDTH = N_HEADS * HEAD_DIM
RMS_EPS = 1e-6
POOL_WINDOWS = (2, 4, 8, 16)
POOL_HALO = 16
CONV_HALO = 8
RWKV_W_RANK, RWKV_A_RANK, RWKV_G_RANK = 64, 32, 64
RWKV_LN_EPS = 64e-5
RWKV_CHUNK = 64
NSA_CMP_LEN = 32
NSA_CMP_STRIDE = 16
NSA_SEL_BLOCK = 64
NSA_TOP_N = 16
NSA_WINDOW = 512
NSA_FORCE_BONUS = 1e4
NSA_NEG = -1e9

PB_COLS = 1024
PC_COLS = 768
PD_COLS = 768
P_PAD = G_WIDTH + PB_COLS + PC_COLS + PD_COLS

V7X_VMEM_BYTES = 64 * 1024 * 1024
VMEM_LIMIT = V7X_VMEM_BYTES - 8 * 1024 * 1024
LANES = 128

FFN_ROWS = 512
FFN_COLS = 256
MIX_ROWS = 512
ATT_ROWS = 128


def _params(n_axes):
    return pltpu.CompilerParams(dimension_semantics=("arbitrary",) * n_axes,
                                vmem_limit_bytes=VMEM_LIMIT)


def _const_spec(shape):
    nd = len(shape)
    return pl.BlockSpec(shape, lambda *_: (0,) * nd, pipeline_mode=pl.Buffered(1))


def _dot(a, b):
    return jnp.dot(a.astype(BF16), b.astype(BF16), preferred_element_type=F32)


def _dot_nt(a, b):
    return lax.dot_general(a.astype(BF16), b.astype(BF16), (((1,), (1,)), ((), ())),
                           preferred_element_type=F32)


def _split(x):
    hi = x.astype(BF16)
    lo = (x - hi.astype(F32)).astype(BF16)
    return hi, lo


def _dot3(a, b):
    ah, al = _split(a)
    bh, bl = _split(b)
    return (jnp.dot(ah, bh, preferred_element_type=F32)
            + jnp.dot(al, bh, preferred_element_type=F32)
            + jnp.dot(ah, bl, preferred_element_type=F32))


def _dot3_nt(a, b):
    ah, al = _split(a)
    bh, bl = _split(b)
    dn = (((1,), (1,)), ((), ()))
    return (lax.dot_general(ah, bh, dn, preferred_element_type=F32)
            + lax.dot_general(al, bh, dn, preferred_element_type=F32)
            + lax.dot_general(ah, bl, dn, preferred_element_type=F32))


def _dot2_exact_rhs(a, b_bf16):
    ah, al = _split(a)
    return (jnp.dot(ah, b_bf16, preferred_element_type=F32)
            + jnp.dot(al, b_bf16, preferred_element_type=F32))


def _dot2_exact_lhs(a_bf16, b):
    bh, bl = _split(b)
    return (jnp.dot(a_bf16, bh, preferred_element_type=F32)
            + jnp.dot(a_bf16, bl, preferred_element_type=F32))


def _rms_rows(x, g):
    return x * lax.rsqrt(jnp.mean(x * x, axis=-1, keepdims=True) + RMS_EPS) * g


def _sigmoid(x):
    return 1.0 / (1.0 + jnp.exp(-x))


def _head_ones():
    r = lax.broadcasted_iota(jnp.int32, (G_WIDTH, G_WIDTH), 0) // HEAD_DIM
    c = lax.broadcasted_iota(jnp.int32, (G_WIDTH, G_WIDTH), 1) // HEAD_DIM
    return r == c


def _ffn_body(x_in, g_ref, wg_ref, wu_ref, wd_ref, o_ref, acc_ref):
    h = _rms_rows(x_in, g_ref[...]).astype(BF16)
    acc_ref[...] = jnp.zeros_like(acc_ref)
    n_chunks = wg_ref.shape[0]

    def step(c, carry):
        gate = jnp.dot(h, wg_ref[c], preferred_element_type=F32)
        up = jnp.dot(h, wu_ref[c], preferred_element_type=F32)
        act = (gate * _sigmoid(gate) * up).astype(BF16)
        acc_ref[...] += jnp.dot(act, wd_ref[c], preferred_element_type=F32)
        return carry

    lax.fori_loop(0, n_chunks, step, 0)
    o_ref[...] = x_in + 0.5 * acc_ref[...]


def _ffn_kernel(x_ref, g_ref, wg_ref, wu_ref, wd_ref, o_ref, acc_ref):
    _ffn_body(x_ref[...], g_ref, wg_ref, wu_ref, wd_ref, o_ref, acc_ref)


def _out_ffn_kernel(x_ref, ya_ref, yb_ref, yc_ref, yd_ref, wo_ref,
                    g_ref, wg_ref, wu_ref, wd_ref, o_ref, acc_ref):
    x1 = x_ref[...]
    for i, y_ref in enumerate((ya_ref, yb_ref, yc_ref, yd_ref)):
        x1 = x1 + _dot(y_ref[...], wo_ref[i])
    _ffn_body(x1, g_ref, wg_ref, wu_ref, wd_ref, o_ref, acc_ref)


def _ffn_weights(w_gate, w_up, w_down):
    d, f = w_gate.shape
    n = f // FFN_COLS
    wg = w_gate.astype(BF16).reshape(d, n, FFN_COLS).transpose(1, 0, 2)
    wu = w_up.astype(BF16).reshape(d, n, FFN_COLS).transpose(1, 0, 2)
    wd = w_down.astype(BF16).reshape(n, FFN_COLS, d)
    return wg, wu, wd


def _ffn_call(x2d, g, w_gate, w_up, w_down, mix=None, w_out=None):
    n_tok, d = x2d.shape
    wg, wu, wd = _ffn_weights(w_gate, w_up, w_down)
    row_spec = pl.BlockSpec((FFN_ROWS, d), lambda i: (i, 0))
    w_specs = [_const_spec((1, d)), _const_spec(wg.shape), _const_spec(wu.shape),
               _const_spec(wd.shape)]
    w_args = [g.reshape(1, d), wg, wu, wd]
    if mix is None:
        kern, in_specs, args = _ffn_kernel, [row_spec] + w_specs, [x2d] + w_args
    else:
        y_spec = pl.BlockSpec((FFN_ROWS, G_WIDTH), lambda i: (i, 0))
        wo = w_out.astype(BF16).reshape(N_MIXERS, G_WIDTH, d)
        kern = _out_ffn_kernel
        in_specs = [row_spec] + [y_spec] * N_MIXERS + [_const_spec(wo.shape)] + w_specs
        args = [x2d] + list(mix) + [wo] + w_args
    return pl.pallas_call(
        kern,
        grid=(n_tok // FFN_ROWS,),
        in_specs=in_specs,
        out_specs=row_spec,
        out_shape=jax.ShapeDtypeStruct((n_tok, d), F32),
        scratch_shapes=[pltpu.VMEM((FFN_ROWS, d), F32)],
        compiler_params=_params(1),
    )(*args)


def _mixin_kernel(x_ref, g_ref, w_ref, poolw_ref, pools_ref, convw_ref,
                  ya_ref, pb_ref, pc_ref, yd_ref, pa_ext, z_ext):
    si = pl.program_id(1)
    rows = x_ref.shape[1]
    h = _rms_rows(x_ref[0], g_ref[...]).astype(BF16)
    c0, c1, c2 = G_WIDTH, G_WIDTH + PB_COLS, G_WIDTH + PB_COLS + PC_COLS
    pb_ref[0] = jnp.dot(h, w_ref[:, c0:c1], preferred_element_type=F32)
    pc_ref[0] = jnp.dot(h, w_ref[:, c1:c2], preferred_element_type=F32)

    @pl.when(si == 0)
    def _():
        pa_ext[0:POOL_HALO, :] = jnp.zeros((POOL_HALO, G_WIDTH), F32)
        z_ext[0:CONV_HALO, :] = jnp.zeros((CONV_HALO, G_WIDTH), F32)

    u = jnp.dot(h, w_ref[:, 0:c0], preferred_element_type=F32)
    pa_ext[POOL_HALO:, :] = u
    lane_group = lax.broadcasted_iota(jnp.int32, (rows, G_WIDTH), 1) // (G_WIDTH // len(POOL_WINDOWS))
    pos = si * rows + lax.broadcasted_iota(jnp.int32, (rows, G_WIDTH), 0)
    total = u
    for k in range(1, max(POOL_WINDOWS)):
        first_group = sum(1 for w in POOL_WINDOWS if w <= k)
        shifted = pa_ext[POOL_HALO - k:POOL_HALO - k + rows, :]
        total = total + jnp.where(lane_group >= first_group, shifted, 0.0)
    window = jnp.left_shift(2, lane_group)
    cnt = jnp.minimum(pos + 1, window).astype(F32)
    pooled = total / cnt
    ya = _dot(pooled - u, poolw_ref[...]) * pools_ref[...]
    ya_ref[0] = ya
    pa_ext[0:POOL_HALO, :] = pa_ext[rows:rows + POOL_HALO, :]

    pd = jnp.dot(h, w_ref[:, c2:c2 + PD_COLS], preferred_element_type=F32)
    cu = pd[:, 0:G_WIDTH]
    cb = pd[:, G_WIDTH:2 * G_WIDTH]
    cc = pd[:, 2 * G_WIDTH:3 * G_WIDTH]
    z = cc * cu
    z_ext[CONV_HALO:, :] = z
    y = (convw_ref[0:1, :] * z_ext[CONV_HALO - 2:CONV_HALO - 2 + rows, :]
         + convw_ref[1:2, :] * z_ext[CONV_HALO - 1:CONV_HALO - 1 + rows, :]
         + convw_ref[2:3, :] * z)
    yd_ref[0] = cb * y
    z_ext[0:CONV_HALO, :] = z_ext[rows:rows + CONV_HALO, :]


def _pad_cols(w, width):
    return jnp.pad(w, ((0, 0), (0, width - w.shape[1])))


def _w_in_padded(w_in):
    g = G_WIDTH
    o = 0
    w_a = w_in[:, o:o + g]; o += g
    rwkv_cols = 3 * g + RWKV_W_RANK + RWKV_A_RANK + RWKV_G_RANK
    w_b = _pad_cols(w_in[:, o:o + rwkv_cols], PB_COLS); o += rwkv_cols
    q = w_in[:, o:o + g]; o += g
    kc, vc, ksl, vsl, kwn, vwn = [w_in[:, o + i * HEAD_DIM:o + (i + 1) * HEAD_DIM] for i in range(6)]
    o += 6 * HEAD_DIM
    gates = w_in[:, o:o + 3 * N_HEADS]; o += 3 * N_HEADS
    w_c = jnp.concatenate([q, kc, vc, ksl, kwn, vsl, vwn, _pad_cols(gates, LANES)], axis=1)
    w_d = w_in[:, o:o + PD_COLS]
    return jnp.concatenate([w_a, w_b, w_c, w_d], axis=1).astype(BF16)


def _mixin_call(x, g, w_in, pool_w, pool_scale, conv_w):
    b, s, d = x.shape
    w = _w_in_padded(w_in)
    n_groups = len(POOL_WINDOWS)
    pool_ch = G_WIDTH // n_groups
    poolw = jnp.zeros((G_WIDTH, G_WIDTH), F32)
    for gi in range(n_groups):
        poolw = poolw.at[gi * pool_ch:(gi + 1) * pool_ch, gi * pool_ch:(gi + 1) * pool_ch].set(pool_w[gi])
    convw = jnp.pad(conv_w, ((0, 8 - conv_w.shape[0]), (0, 0)))

    def out_spec(c):
        return pl.BlockSpec((1, MIX_ROWS, c), lambda bi, si: (bi, si, 0))

    return pl.pallas_call(
        _mixin_kernel,
        grid=(b, s // MIX_ROWS),
        in_specs=[pl.BlockSpec((1, MIX_ROWS, d), lambda bi, si: (bi, si, 0)),
                  _const_spec((1, d)), _const_spec(w.shape),
                  _const_spec((G_WIDTH, G_WIDTH)), _const_spec((1, G_WIDTH)),
                  _const_spec((8, G_WIDTH))],
        out_specs=[out_spec(G_WIDTH), out_spec(PB_COLS), out_spec(PC_COLS), out_spec(G_WIDTH)],
        out_shape=[jax.ShapeDtypeStruct((b, s, G_WIDTH), F32),
                   jax.ShapeDtypeStruct((b, s, PB_COLS), F32),
                   jax.ShapeDtypeStruct((b, s, PC_COLS), F32),
                   jax.ShapeDtypeStruct((b, s, G_WIDTH), F32)],
        scratch_shapes=[pltpu.VMEM((MIX_ROWS + POOL_HALO, G_WIDTH), F32),
                        pltpu.VMEM((MIX_ROWS + CONV_HALO, G_WIDTH), F32)],
        compiler_params=_params(2),
    )(x, g.reshape(1, d), w, poolw.astype(BF16), pool_scale.reshape(1, G_WIDTH), convw)


def _block_diag(x, mask):
    return jnp.where(mask, jnp.concatenate([x] * N_HEADS, axis=0), 0.0)


def _rwkv_kernel(pb_ref, mu_ref, w0_ref, wup_ref, a0_ref, aup_ref, gup_ref, kk_ref, ka_ref,
                 rk_ref, lnw_ref, lnb_ref, o_ref,
                 ext, state, r_s, k_s, v_s, lw_s, a_s, b_s, y_s):
    si = pl.program_id(1)
    rows = pb_ref.shape[1]
    g_w = G_WIDTH
    chunk = RWKV_CHUNK

    @pl.when(si == 0)
    def _():
        ext[0:8, :] = jnp.zeros((8, PB_COLS), F32)
        state[...] = jnp.zeros_like(state)

    p = pb_ref[0]
    ext[8:, :] = p
    prev = ext[7:7 + rows, :]
    ps = p + mu_ref[...] * (prev - p)
    ext[0:8, :] = ext[rows:rows + 8, :]

    head_mask = _head_ones()
    head_ones = head_mask.astype(BF16)

    def head_sum(t):
        return _dot2_exact_rhs(t, head_ones)

    r = ps[:, 0:g_w]
    k = ps[:, g_w:2 * g_w]
    v = ps[:, 2 * g_w:3 * g_w]
    tail = ps[:, 3 * g_w:4 * g_w]
    w = w0_ref[...] + _dot(jnp.tanh(tail), wup_ref[...])
    lw = -jnp.exp(-0.5) * _sigmoid(w)
    a = _sigmoid(a0_ref[...] + _dot(tail, aup_ref[...]))
    gate = _dot(_sigmoid(tail), gup_ref[...])
    kk = k * kk_ref[...]
    kk = kk * lax.rsqrt(jnp.maximum(head_sum(kk * kk), 1e-24))
    k2 = k * (1.0 + (a - 1.0) * ka_ref[...])
    bonus = head_sum(r * k2 * rk_ref[...]) * v
    r_s[...] = r
    k_s[...] = k2
    v_s[...] = v
    lw_s[...] = lw
    a_s[...] = -kk
    b_s[...] = kk * a

    row_i = lax.broadcasted_iota(jnp.int32, (chunk, g_w), 0)
    col_j = lax.broadcasted_iota(jnp.int32, (chunk, g_w), 1) % chunk
    strict_lower = row_i > col_j
    lower = row_i >= col_j
    tri = (lax.broadcasted_iota(jnp.int32, (chunk, chunk), 0)
           >= lax.broadcasted_iota(jnp.int32, (chunk, chunk), 1)).astype(BF16)
    n_doublings = chunk.bit_length() - 1

    def chunk_step(c, carry):
        sl = pl.ds(pl.multiple_of(c * chunk, chunk), chunk)
        r_c, k_c, v_c, lw_c, a_c, b_c = (t[sl, :] for t in (r_s, k_s, v_s, lw_s, a_s, b_s))
        cw = _dot2_exact_lhs(tri, lw_c)
        cw_last = cw[chunk - 1:chunk, :]
        a_t = a_c * jnp.exp(cw - lw_c)
        r_t = r_c * jnp.exp(cw)
        e_inv = jnp.exp(-cw)
        b_t = b_c * e_inv
        k_t = k_c * e_inv
        e_fut = jnp.exp(cw_last - cw)
        bd_v = _block_diag(v_c, head_mask)

        pair = _dot3_nt(jnp.concatenate([a_t, r_t], axis=0),
                        jnp.concatenate([_block_diag(b_t, head_mask), _block_diag(k_t, head_mask)], axis=0))
        l_ab = jnp.where(strict_lower, pair[0:chunk, 0:g_w], 0.0)
        l_ak = jnp.where(strict_lower, pair[0:chunk, g_w:2 * g_w], 0.0)
        m_rb = jnp.where(lower, pair[chunk:2 * chunk, 0:g_w], 0.0)
        m_rk = jnp.where(lower, pair[chunk:2 * chunk, g_w:2 * g_w], 0.0)

        x1 = a_t
        x2 = _dot3(l_ak, bd_v)
        l_pow = l_ab
        for it in range(n_doublings):
            blocks = [_block_diag(x1, head_mask), _block_diag(x2, head_mask)]
            if it < n_doublings - 1:
                blocks.append(_block_diag(l_pow, head_mask))
            res = _dot3(l_pow, jnp.concatenate(blocks, axis=1))
            x1 = x1 + res[:, 0:g_w]
            x2 = x2 + res[:, g_w:2 * g_w]
            if it < n_doublings - 1:
                l_pow = res[:, 2 * g_w:3 * g_w]

        q = _dot3(m_rb, jnp.concatenate([_block_diag(x1, head_mask), _block_diag(x2, head_mask)], axis=1))
        q1 = r_t + q[:, 0:g_w]
        q2 = q[:, g_w:2 * g_w] + _dot3(m_rk, bd_v)

        s0 = state[...]
        us = _dot3_nt(jnp.concatenate([x1, q1], axis=0), s0)
        u = us[0:chunk] + x2
        y_s[sl, :] = us[chunk:2 * chunk] + q2
        uv_t = jnp.transpose(jnp.concatenate([u, v_c], axis=0))
        upd = _dot3(uv_t, jnp.concatenate([b_c * e_fut, k_c * e_fut], axis=0))
        state[...] = s0 * jnp.exp(cw_last) + jnp.where(head_mask, upd, 0.0)
        return carry

    lax.fori_loop(0, rows // chunk, chunk_step, 0)

    y = y_s[...]
    inv_n = 1.0 / HEAD_DIM
    mean = head_sum(y) * inv_n
    dev = y - mean
    var = head_sum(dev * dev) * inv_n
    yn = dev * lax.rsqrt(var + RWKV_LN_EPS) * lnw_ref[...] + lnb_ref[...]
    o_ref[0] = (yn + bonus) * gate


def _rwkv_call(pb, mu, w0, w_up, a0, a_up, g_up, k_k, k_a, r_k, ln_w, ln_b):
    b, s, _ = pb.shape
    g_w = G_WIDTH
    row = lambda t: t.reshape(1, g_w)
    o_a = RWKV_W_RANK
    o_g = o_a + RWKV_A_RANK
    wup = jnp.zeros((g_w, g_w), F32).at[0:o_a].set(w_up).astype(BF16)
    aup = jnp.zeros((g_w, g_w), F32).at[o_a:o_g].set(a_up).astype(BF16)
    gup = jnp.zeros((g_w, g_w), F32).at[o_g:o_g + RWKV_G_RANK].set(g_up).astype(BF16)
    mu_p = _pad_cols(mu.reshape(1, -1), PB_COLS)
    vec = _const_spec((1, g_w))
    mat = _const_spec((g_w, g_w))
    seq = pltpu.VMEM((MIX_ROWS, g_w), F32)
    return pl.pallas_call(
        _rwkv_kernel,
        grid=(b, s // MIX_ROWS),
        in_specs=[pl.BlockSpec((1, MIX_ROWS, PB_COLS), lambda bi, si: (bi, si, 0)),
                  _const_spec((1, PB_COLS)), vec, mat, vec, mat, mat, vec, vec, vec, vec, vec],
        out_specs=pl.BlockSpec((1, MIX_ROWS, g_w), lambda bi, si: (bi, si, 0)),
        out_shape=jax.ShapeDtypeStruct((b, s, g_w), F32),
        scratch_shapes=[pltpu.VMEM((MIX_ROWS + 8, PB_COLS), F32),
                        pltpu.VMEM((g_w, g_w), F32),
                        seq, seq, seq, seq, seq, seq, seq],
        compiler_params=_params(2),
    )(pb, mu_p, row(w0), wup, row(a0), aup, gup, row(k_k), row(k_a), row(r_k), row(ln_w), row(ln_b))


def _gelu_tanh(x):
    return 0.5 * x * (1.0 + jnp.tanh(0.7978845608028654 * (x + 0.044715 * x * x * x)))


def _nsa_compress_kernel(kc_ref, vc_ref, pos_ref, kw1_ref, kw2_ref, vw1_ref, vw2t_ref, kn_ref,
                         kcmp_ref, vcmpt_ref):
    half = kw1_ref.shape[0] // 2
    n_rows = kc_ref.shape[1]

    def hidden(x, w1_ref):
        first = _dot(x + pos_ref[0:1, :], w1_ref[0:half, :])
        second = _dot(x + pos_ref[1:2, :], w1_ref[half:2 * half, :])
        return _gelu_tanh(first + pltpu.roll(second, n_rows - 1, 0))

    k_cmp = _dot(hidden(kc_ref[0], kw1_ref), kw2_ref[...])
    kcmp_ref[0] = _rms_rows(k_cmp, kn_ref[...]).astype(BF16)
    vcmpt_ref[0] = _dot_nt(vw2t_ref[...], hidden(vc_ref[0], vw1_ref)).astype(BF16)


def _nsa_compress_call(kcx, vcx, pos, kw1, kw2, vw1, vw2, k_norm0):
    b, n_chunks, width = kcx.shape
    hid = kw1.shape[1]
    pos2 = jnp.pad(pos.reshape(2, width), ((0, 6), (0, 0)))
    blk = pl.BlockSpec((1, n_chunks, width), lambda bi: (bi, 0, 0))
    return pl.pallas_call(
        _nsa_compress_kernel,
        grid=(b,),
        in_specs=[blk, blk, _const_spec((8, width)),
                  _const_spec((2 * width, hid)), _const_spec((hid, HEAD_DIM)),
                  _const_spec((2 * width, hid)), _const_spec((HEAD_DIM, hid)),
                  _const_spec((1, HEAD_DIM))],
        out_specs=[pl.BlockSpec((1, n_chunks, HEAD_DIM), lambda bi: (bi, 0, 0)),
                   pl.BlockSpec((1, HEAD_DIM, n_chunks), lambda bi: (bi, 0, 0))],
        out_shape=[jax.ShapeDtypeStruct((b, n_chunks, HEAD_DIM), BF16),
                   jax.ShapeDtypeStruct((b, HEAD_DIM, n_chunks), BF16)],
        compiler_params=_params(1),
    )(kcx, vcx, pos2, kw1.astype(BF16), kw2.astype(BF16), vw1.astype(BF16),
      vw2.T.astype(BF16), k_norm0.reshape(1, HEAD_DIM))


def _nsa_kv_kernel(k_ref, v_ref, kn_ref, kk_ref, vt_ref):
    x = k_ref[0]
    first = lax.broadcasted_iota(jnp.int32, x.shape, 1) < HEAD_DIM
    sq = x * x
    ms_first = jnp.sum(jnp.where(first, sq, 0.0), axis=-1, keepdims=True) * (1.0 / HEAD_DIM)
    ms_second = jnp.sum(jnp.where(first, 0.0, sq), axis=-1, keepdims=True) * (1.0 / HEAD_DIM)
    inv = jnp.where(first, lax.rsqrt(ms_first + RMS_EPS), lax.rsqrt(ms_second + RMS_EPS))
    kk_ref[0] = (x * inv * kn_ref[...]).astype(BF16)
    v = v_ref[0]
    for j in range(vt_ref.shape[1]):
        vt_ref[0, j] = jnp.transpose(v[j * ATT_ROWS:(j + 1) * ATT_ROWS, :]).astype(BF16)


def _nsa_kv_call(pc, k_norm_sel, k_norm_win):
    b, s, _ = pc.shape
    kn = jnp.concatenate([k_norm_sel, k_norm_win]).reshape(1, 2 * HEAD_DIM)
    per_step = MIX_ROWS // ATT_ROWS
    return pl.pallas_call(
        _nsa_kv_kernel,
        grid=(b, s // MIX_ROWS),
        in_specs=[pl.BlockSpec((1, MIX_ROWS, LANES), lambda bi, si: (bi, si, 3)),
                  pl.BlockSpec((1, MIX_ROWS, LANES), lambda bi, si: (bi, si, 4)),
                  _const_spec((1, LANES))],
        out_specs=[pl.BlockSpec((1, MIX_ROWS, LANES), lambda bi, si: (bi, si, 0)),
                   pl.BlockSpec((1, per_step, LANES, ATT_ROWS), lambda bi, si: (bi, si, 0, 0))],
        out_shape=[jax.ShapeDtypeStruct((b, s, LANES), BF16),
                   jax.ShapeDtypeStruct((b, s // ATT_ROWS, LANES, ATT_ROWS), BF16)],
        compiler_params=_params(2),
    )(pc, pc, kn)


def _nsa_attn_kernel(q_ref, gate_ref, qn_ref, kc_ref, vct_ref, kk_ref, vt_ref, o_ref):
    qi = pl.program_id(1)
    tq = ATT_ROWS
    hw = N_HEADS * tq
    q0 = qi * tq
    n_cmp_rows = kc_ref.shape[1]
    n_sel = kk_ref.shape[1] // NSA_SEL_BLOCK

    head_ones = _head_ones().astype(BF16)
    q = q_ref[0]
    ms = _dot2_exact_rhs(q * q, head_ones) * (1.0 / HEAD_DIM)
    qn = q * lax.rsqrt(ms + RMS_EPS) * qn_ref[...] * (HEAD_DIM ** -0.5)
    qt = jnp.transpose(qn)
    q4t = jnp.concatenate([qt[h * HEAD_DIM:(h + 1) * HEAD_DIM, :] for h in range(N_HEADS)],
                          axis=1).astype(BF16)
    zero_half = jnp.zeros_like(q4t)
    q_sel = jnp.concatenate([q4t, zero_half], axis=0)
    q_win = jnp.concatenate([zero_half, q4t], axis=0)

    lane = lax.broadcasted_iota(jnp.int32, (1, hw), 1)
    slope = jnp.exp2(-2.0 * (lane // tq + 1).astype(F32))
    t_lane = lane % tq

    s_c = jnp.dot(kc_ref[0], q4t, preferred_element_type=F32)
    n_idx = lax.broadcasted_iota(jnp.int32, (n_cmp_rows, hw), 0)
    dist_c = (q0 + t_lane) - (n_idx * NSA_CMP_STRIDE + (NSA_CMP_LEN - 1))
    valid_c = dist_c >= 0
    s_c = jnp.where(valid_c, s_c - slope * dist_c.astype(F32), NSA_NEG)
    m_c = jnp.max(s_c, axis=0, keepdims=True)
    p_c = jnp.where(valid_c, jnp.exp(s_c - m_c), 0.0)
    l_c = jnp.sum(p_c, axis=0, keepdims=True)
    p_c = p_c / jnp.maximum(l_c, 1e-30)
    o_cmp = jnp.dot(vct_ref[0], p_c.astype(BF16), preferred_element_type=F32)

    p_heads = p_c[:, 0:tq]
    for h in range(1, N_HEADS):
        p_heads = p_heads + p_c[:, h * tq:(h + 1) * tq]
    per_sel = NSA_SEL_BLOCK // NSA_CMP_STRIDE
    j_ov = lax.broadcasted_iota(jnp.int32, (n_sel, n_cmp_rows), 0)
    n_ov = lax.broadcasted_iota(jnp.int32, (n_sel, n_cmp_rows), 1)
    overlap_t = ((n_ov >= per_sel * j_ov - (NSA_CMP_LEN // NSA_CMP_STRIDE - 1))
                 & (n_ov <= per_sel * j_ov + per_sel - 1)).astype(BF16)
    p_hi = p_heads.astype(BF16)
    p_rest = p_heads - p_hi.astype(F32)
    p_mid = p_rest.astype(BF16)
    p_lo = (p_rest - p_mid.astype(F32)).astype(BF16)
    imp = (jnp.dot(overlap_t, p_hi, preferred_element_type=F32)
           + jnp.dot(overlap_t, p_mid, preferred_element_type=F32)
           + jnp.dot(overlap_t, p_lo, preferred_element_type=F32))
    j_idx = lax.broadcasted_iota(jnp.int32, (n_sel, tq), 0)
    cur = (q0 + lax.broadcasted_iota(jnp.int32, (1, tq), 1)) // NSA_SEL_BLOCK
    forced = (j_idx == 0) | (j_idx == cur) | (j_idx == cur - 1)
    imp = jnp.where(j_idx <= cur, imp + jnp.where(forced, NSA_FORCE_BONUS, 0.0), -1.0)
    rank = jnp.zeros((n_sel, tq), F32)
    for jp in range(n_sel):
        other = imp[jp:jp + 1, :]
        ahead = (other > imp) | ((other == imp) & (j_idx > jp))
        rank = rank + jnp.where(ahead, 1.0, 0.0)
    chosen = jnp.where(rank < float(min(NSA_TOP_N, n_sel)), 1.0, 0.0)

    s_sub = lax.broadcasted_iota(jnp.int32, (tq, 1), 0)
    base = t_lane - s_sub
    first_block = lax.broadcasted_iota(jnp.int32, (tq, tq), 0) < NSA_SEL_BLOCK

    def attend(kt, carry, q_pad, v_rows, mask_fn):
        m, l, acc = carry
        k0 = pl.multiple_of(kt * tq, tq)
        s = jnp.dot(kk_ref[0, pl.ds(k0, tq), :], q_pad, preferred_element_type=F32)
        dist = base + (q0 - k0)
        mask = mask_fn(kt, dist)
        s = jnp.where(mask, s - slope * dist.astype(F32), NSA_NEG)
        m_new = jnp.maximum(m, jnp.max(s, axis=0, keepdims=True))
        alpha = jnp.exp(m - m_new)
        p = jnp.where(mask, jnp.exp(s - m_new), 0.0)
        l = alpha * l + jnp.sum(p, axis=0, keepdims=True)
        v_t = vt_ref[0, kt][v_rows[0]:v_rows[1], :]
        acc = alpha * acc + jnp.dot(v_t, p.astype(BF16), preferred_element_type=F32)
        return m_new, l, acc

    def sel_mask(kt, dist):
        blocks_per_tile = tq // NSA_SEL_BLOCK
        row0 = jnp.max(jnp.where(j_idx == blocks_per_tile * kt, chosen, 0.0), axis=0, keepdims=True)
        row1 = jnp.max(jnp.where(j_idx == blocks_per_tile * kt + 1, chosen, 0.0), axis=0, keepdims=True)
        tile = jnp.where(first_block, row0, row1)
        return (jnp.concatenate([tile] * N_HEADS, axis=1) > 0.5) & (dist >= 0)

    def win_mask(kt, dist):
        return (dist >= 0) & (dist < NSA_WINDOW)

    init = (jnp.full((1, hw), NSA_NEG, F32), jnp.zeros((1, hw), F32), jnp.zeros((HEAD_DIM, hw), F32))
    _, l_s, acc_s = lax.fori_loop(
        0, qi + 1, lambda kt, c: attend(kt, c, q_sel, (0, HEAD_DIM), sel_mask), init)
    o_sel = acc_s / l_s
    _, l_w, acc_w = lax.fori_loop(
        jnp.maximum(qi - NSA_WINDOW // tq, 0), qi + 1,
        lambda kt, c: attend(kt, c, q_win, (HEAD_DIM, 2 * HEAD_DIM), win_mask), init)
    o_win = acc_w / l_w

    g_t = jnp.transpose(_sigmoid(gate_ref[0]))

    def gate_row(c):
        return jnp.concatenate([g_t[3 * h + c:3 * h + c + 1, :] for h in range(N_HEADS)], axis=1)

    o = gate_row(0) * o_cmp + gate_row(1) * o_sel + gate_row(2) * o_win
    o_hd = jnp.concatenate([o[:, h * tq:(h + 1) * tq] for h in range(N_HEADS)], axis=0)
    o_ref[0] = jnp.transpose(o_hd)


def _nsa_attn_call(pc, q_norm, kcmp, vcmpt, kk, vt):
    b, s, _ = pc.shape
    n_cmp_rows = kcmp.shape[1]
    qn = jnp.tile(q_norm, N_HEADS).reshape(1, G_WIDTH)
    return pl.pallas_call(
        _nsa_attn_kernel,
        grid=(b, s // ATT_ROWS),
        in_specs=[pl.BlockSpec((1, ATT_ROWS, G_WIDTH), lambda bi, qi: (bi, qi, 0)),
                  pl.BlockSpec((1, ATT_ROWS, LANES), lambda bi, qi: (bi, qi, 5)),
                  _const_spec((1, G_WIDTH)),
                  pl.BlockSpec((1, n_cmp_rows, HEAD_DIM), lambda bi, qi: (bi, 0, 0)),
                  pl.BlockSpec((1, HEAD_DIM, n_cmp_rows), lambda bi, qi: (bi, 0, 0)),
                  pl.BlockSpec((1, s, LANES), lambda bi, qi: (bi, 0, 0)),
                  pl.BlockSpec((1, s // ATT_ROWS, LANES, ATT_ROWS), lambda bi, qi: (bi, 0, 0, 0))],
        out_specs=pl.BlockSpec((1, ATT_ROWS, G_WIDTH), lambda bi, qi: (bi, qi, 0)),
        out_shape=jax.ShapeDtypeStruct((b, s, G_WIDTH), F32),
        compiler_params=_params(2),
    )(pc, pc, qn, kcmp, vcmpt, kk, vt)


def _nsa_call(pc, q_norm, k_norm, cmp_pos, kw1, kw2, vw1, vw2):
    b, s, _ = pc.shape
    n_chunks = s // NSA_CMP_STRIDE
    width = NSA_CMP_STRIDE * HEAD_DIM
    kcx = pc[..., G_WIDTH:G_WIDTH + HEAD_DIM].reshape(b, n_chunks, width)
    vcx = pc[..., G_WIDTH + HEAD_DIM:G_WIDTH + 2 * HEAD_DIM].reshape(b, n_chunks, width)
    kcmp, vcmpt = _nsa_compress_call(kcx, vcx, cmp_pos, kw1, kw2, vw1, vw2, k_norm[0])
    kk, vt = _nsa_kv_call(pc, k_norm[1], k_norm[2])
    return _nsa_attn_call(pc, q_norm, kcmp, vcmpt, kk, vt)


def kernel(x, ffn1_norm, ffn1_w_gate, ffn1_w_up, ffn1_w_down, mix_norm, w_in, pool_w, pool_scale, rwkv_mu, rwkv_w0, rwkv_w_up, rwkv_a0, rwkv_a_up, rwkv_g_up, rwkv_k_k, rwkv_k_a, rwkv_r_k, rwkv_ln_w, rwkv_ln_b, nsa_q_norm, nsa_k_norm, nsa_cmp_pos, nsa_cmp_k_w1, nsa_cmp_k_w2, nsa_cmp_v_w1, nsa_cmp_v_w2, conv_w, w_out, ffn2_norm, ffn2_w_gate, ffn2_w_up, ffn2_w_down):
    b, s, d = x.shape
    n_tok = b * s
    x2d = x.reshape(n_tok, d)
    for l in range(ffn1_norm.shape[0]):
        x2d = _ffn_call(x2d, ffn1_norm[l], ffn1_w_gate[l], ffn1_w_up[l], ffn1_w_down[l])
        ya, pb, pc, yd = _mixin_call(x2d.reshape(b, s, d), mix_norm[l], w_in[l], pool_w[l],
                                     pool_scale[l], conv_w[l])
        yb = _rwkv_call(pb, rwkv_mu[l], rwkv_w0[l], rwkv_w_up[l], rwkv_a0[l], rwkv_a_up[l],
                        rwkv_g_up[l], rwkv_k_k[l], rwkv_k_a[l], rwkv_r_k[l], rwkv_ln_w[l],
                        rwkv_ln_b[l])
        yc = _nsa_call(pc, nsa_q_norm[l], nsa_k_norm[l], nsa_cmp_pos[l], nsa_cmp_k_w1[l],
                       nsa_cmp_k_w2[l], nsa_cmp_v_w1[l], nsa_cmp_v_w2[l])
        mix = tuple(t.reshape(n_tok, G_WIDTH) for t in (ya, yb, yc, yd))
        x2d = _ffn_call(x2d, ffn2_norm[l], ffn2_w_gate[l], ffn2_w_up[l], ffn2_w_down[l],
                        mix=mix, w_out=w_out[l])
    return x2d.reshape(b, s, d)
```

```python
import functools

import jax
import jax.numpy as jnp
from jax import lax
from jax.experimental import pallas as pl
from jax.experimental.pallas import tpu as pltpu

F32 = jnp.float32
BF16 = jnp.bfloat16

N_MIXERS = 4
HEAD_DIM = 64
N_HEADS = 4
G_WIDTH = N_HEADS * HEAD_DIM
RMS_EPS = 1e-6
POOL_WINDOWS = (2, 4, 8, 16)
POOL_HALO = 16
CONV_HALO = 8
RWKV_W_RANK, RWKV_A_RANK, RWKV_G_RANK = 64, 32, 64
RWKV_LN_EPS = 64e-5
RWKV_CHUNK = 64
RWKV_GROUP = 4
NSA_CMP_LEN = 32
NSA_CMP_STRIDE = 16
NSA_SEL_BLOCK = 64
NSA_TOP_N = 16
NSA_WINDOW = 512
NSA_FORCE_BONUS = 1e4
NSA_NEG = -1e9
SCORE_FLOOR = 0.5 * NSA_NEG
ALIBI_SPLIT = 64

PB_COLS = 1024
PC_COLS = 768
PD_COLS = 768
P_PAD = G_WIDTH + PB_COLS + PC_COLS + PD_COLS

V7X_VMEM_BYTES = 64 * 1024 * 1024
VMEM_LIMIT = V7X_VMEM_BYTES - 8 * 1024 * 1024
LANES = 128

FFN_ROWS = 512
FFN_COLS = 256
MIX_ROWS = 512
ATT_ROWS = 128
KEY_STEP = 2 * ATT_ROWS


def _params(n_axes):
    return pltpu.CompilerParams(dimension_semantics=("arbitrary",) * n_axes,
                                vmem_limit_bytes=VMEM_LIMIT)


def _const_spec(shape):
    nd = len(shape)
    return pl.BlockSpec(shape, lambda *_: (0,) * nd, pipeline_mode=pl.Buffered(1))


def _dot(a, b):
    return jnp.dot(a.astype(BF16), b.astype(BF16), preferred_element_type=F32)


def _dot_nt(a, b):
    return lax.dot_general(a.astype(BF16), b.astype(BF16), (((1,), (1,)), ((), ())),
                           preferred_element_type=F32)


def _split(x):
    hi = x.astype(BF16)
    lo = (x - hi.astype(F32)).astype(BF16)
    return hi, lo


def _dot2_exact_rhs(a, b_bf16):
    ah, al = _split(a)
    return (jnp.dot(ah, b_bf16, preferred_element_type=F32)
            + jnp.dot(al, b_bf16, preferred_element_type=F32))


def _dot2_exact_lhs(a_bf16, b):
    bh, bl = _split(b)
    return (jnp.dot(a_bf16, bh, preferred_element_type=F32)
            + jnp.dot(a_bf16, bl, preferred_element_type=F32))


def _rms_rows(x, g):
    return x * lax.rsqrt(jnp.mean(x * x, axis=-1, keepdims=True) + RMS_EPS) * g


def _sigmoid(x):
    return 1.0 / (1.0 + jnp.exp(-x))


def _head_ones():
    r = lax.broadcasted_iota(jnp.int32, (G_WIDTH, G_WIDTH), 0) // HEAD_DIM
    c = lax.broadcasted_iota(jnp.int32, (G_WIDTH, G_WIDTH), 1) // HEAD_DIM
    return r == c


def _ffn_body(x_in, g_ref, wg_ref, wu_ref, wd_ref, o_ref, acc_ref):
    h = _rms_rows(x_in, g_ref[...]).astype(BF16)
    for c in range(wg_ref.shape[0]):
        gate = jnp.dot(h, wg_ref[c], preferred_element_type=F32)
        up = jnp.dot(h, wu_ref[c], preferred_element_type=F32)
        act = (gate * _sigmoid(gate) * up).astype(BF16)
        down = jnp.dot(act, wd_ref[c], preferred_element_type=F32)
        if c == 0:
            acc_ref[...] = down
        else:
            acc_ref[...] += down
    o_ref[...] = x_in + 0.5 * acc_ref[...]


def _ffn_kernel(x_ref, g_ref, wg_ref, wu_ref, wd_ref, o_ref, acc_ref):
    _ffn_body(x_ref[...], g_ref, wg_ref, wu_ref, wd_ref, o_ref, acc_ref)


def _out_ffn_kernel(x_ref, ya_ref, yb_ref, yc_ref, yd_ref, wo_ref,
                    g_ref, wg_ref, wu_ref, wd_ref, o_ref, acc_ref):
    x1 = x_ref[...]
    for i, y_ref in enumerate((ya_ref, yb_ref, yc_ref, yd_ref)):
        x1 = x1 + _dot(y_ref[...], wo_ref[i])
    _ffn_body(x1, g_ref, wg_ref, wu_ref, wd_ref, o_ref, acc_ref)


def _ffn_weights(w_gate, w_up, w_down):
    d, f = w_gate.shape
    n = f // FFN_COLS
    wg = w_gate.astype(BF16).reshape(d, n, FFN_COLS).transpose(1, 0, 2)
    wu = w_up.astype(BF16).reshape(d, n, FFN_COLS).transpose(1, 0, 2)
    wd = w_down.astype(BF16).reshape(n, FFN_COLS, d)
    return wg, wu, wd


def _ffn_call(x2d, g, w_gate, w_up, w_down, mix=None, w_out=None):
    n_tok, d = x2d.shape
    wg, wu, wd = _ffn_weights(w_gate, w_up, w_down)
    row_spec = pl.BlockSpec((FFN_ROWS, d), lambda i: (i, 0))
    w_specs = [_const_spec((1, d)), _const_spec(wg.shape), _const_spec(wu.shape),
               _const_spec(wd.shape)]
    w_args = [g.reshape(1, d), wg, wu, wd]
    if mix is None:
        kern, in_specs, args = _ffn_kernel, [row_spec] + w_specs, [x2d] + w_args
    else:
        y_spec = pl.BlockSpec((FFN_ROWS, G_WIDTH), lambda i: (i, 0))
        wo = w_out.astype(BF16).reshape(N_MIXERS, G_WIDTH, d)
        kern = _out_ffn_kernel
        in_specs = [row_spec] + [y_spec] * N_MIXERS + [_const_spec(wo.shape)] + w_specs
        args = [x2d] + list(mix) + [wo] + w_args
    return pl.pallas_call(
        kern,
        grid=(n_tok // FFN_ROWS,),
        in_specs=in_specs,
        out_specs=row_spec,
        out_shape=jax.ShapeDtypeStruct((n_tok, d), F32),
        scratch_shapes=[pltpu.VMEM((FFN_ROWS, d), F32)],
        compiler_params=_params(1),
    )(*args)


def _mixin_kernel(x_ref, g_ref, w_ref, poolw_ref, pools_ref, convw_ref,
                  ya_ref, pb_ref, pc_ref, yd_ref, pa_ext, z_ext):
    si = pl.program_id(1)
    rows = x_ref.shape[1]
    h = _rms_rows(x_ref[0], g_ref[...]).astype(BF16)
    c0, c1, c2 = G_WIDTH, G_WIDTH + PB_COLS, G_WIDTH + PB_COLS + PC_COLS
    pb_ref[0] = jnp.dot(h, w_ref[:, c0:c1], preferred_element_type=F32)
    pc_ref[0] = jnp.dot(h, w_ref[:, c1:c2], preferred_element_type=F32)

    @pl.when(si == 0)
    def _():
        pa_ext[0:POOL_HALO, :] = jnp.zeros((POOL_HALO, G_WIDTH), F32)
        z_ext[0:CONV_HALO, :] = jnp.zeros((CONV_HALO, G_WIDTH), F32)

    u = jnp.dot(h, w_ref[:, 0:c0], preferred_element_type=F32)
    pa_ext[POOL_HALO:, :] = u
    lane_group = lax.broadcasted_iota(jnp.int32, (rows, G_WIDTH), 1) // (G_WIDTH // len(POOL_WINDOWS))
    pos = si * rows + lax.broadcasted_iota(jnp.int32, (rows, G_WIDTH), 0)
    total = u
    for k in range(1, max(POOL_WINDOWS)):
        first_group = sum(1 for w in POOL_WINDOWS if w <= k)
        shifted = pa_ext[POOL_HALO - k:POOL_HALO - k + rows, :]
        total = total + jnp.where(lane_group >= first_group, shifted, 0.0)
    window = jnp.left_shift(2, lane_group)
    cnt = jnp.minimum(pos + 1, window).astype(F32)
    pooled = total / cnt
    ya = _dot(pooled - u, poolw_ref[...]) * pools_ref[...]
    ya_ref[0] = ya
    pa_ext[0:POOL_HALO, :] = pa_ext[rows:rows + POOL_HALO, :]

    pd = jnp.dot(h, w_ref[:, c2:c2 + PD_COLS], preferred_element_type=F32)
    cu = pd[:, 0:G_WIDTH]
    cb = pd[:, G_WIDTH:2 * G_WIDTH]
    cc = pd[:, 2 * G_WIDTH:3 * G_WIDTH]
    z = cc * cu
    z_ext[CONV_HALO:, :] = z
    y = (convw_ref[0:1, :] * z_ext[CONV_HALO - 2:CONV_HALO - 2 + rows, :]
         + convw_ref[1:2, :] * z_ext[CONV_HALO - 1:CONV_HALO - 1 + rows, :]
         + convw_ref[2:3, :] * z)
    yd_ref[0] = cb * y
    z_ext[0:CONV_HALO, :] = z_ext[rows:rows + CONV_HALO, :]


def _pad_cols(w, width):
    return jnp.pad(w, ((0, 0), (0, width - w.shape[1])))


def _w_in_padded(w_in):
    g = G_WIDTH
    o = 0
    w_a = w_in[:, o:o + g]; o += g
    rwkv_cols = 3 * g + RWKV_W_RANK + RWKV_A_RANK + RWKV_G_RANK
    w_b = _pad_cols(w_in[:, o:o + rwkv_cols], PB_COLS); o += rwkv_cols
    q = w_in[:, o:o + g]; o += g
    kc, vc, ksl, vsl, kwn, vwn = [w_in[:, o + i * HEAD_DIM:o + (i + 1) * HEAD_DIM] for i in range(6)]
    o += 6 * HEAD_DIM
    gates = w_in[:, o:o + 3 * N_HEADS]; o += 3 * N_HEADS
    w_c = jnp.concatenate([q, kc, vc, ksl, kwn, vsl, vwn, _pad_cols(gates, LANES)], axis=1)
    w_d = w_in[:, o:o + PD_COLS]
    return jnp.concatenate([w_a, w_b, w_c, w_d], axis=1).astype(BF16)


def _mixin_call(x, g, w_in, pool_w, pool_scale, conv_w):
    b, s, d = x.shape
    w = _w_in_padded(w_in)
    n_groups = len(POOL_WINDOWS)
    pool_ch = G_WIDTH // n_groups
    poolw = jnp.zeros((G_WIDTH, G_WIDTH), F32)
    for gi in range(n_groups):
        poolw = poolw.at[gi * pool_ch:(gi + 1) * pool_ch, gi * pool_ch:(gi + 1) * pool_ch].set(pool_w[gi])
    convw = jnp.pad(conv_w, ((0, 8 - conv_w.shape[0]), (0, 0)))

    def out_spec(c):
        return pl.BlockSpec((1, MIX_ROWS, c), lambda bi, si: (bi, si, 0))

    return pl.pallas_call(
        _mixin_kernel,
        grid=(b, s // MIX_ROWS),
        in_specs=[pl.BlockSpec((1, MIX_ROWS, d), lambda bi, si: (bi, si, 0)),
                  _const_spec((1, d)), _const_spec(w.shape),
                  _const_spec((G_WIDTH, G_WIDTH)), _const_spec((1, G_WIDTH)),
                  _const_spec((8, G_WIDTH))],
        out_specs=[out_spec(G_WIDTH), out_spec(PB_COLS), out_spec(PC_COLS), out_spec(G_WIDTH)],
        out_shape=[jax.ShapeDtypeStruct((b, s, G_WIDTH), F32),
                   jax.ShapeDtypeStruct((b, s, PB_COLS), F32),
                   jax.ShapeDtypeStruct((b, s, PC_COLS), F32),
                   jax.ShapeDtypeStruct((b, s, G_WIDTH), F32)],
        scratch_shapes=[pltpu.VMEM((MIX_ROWS + POOL_HALO, G_WIDTH), F32),
                        pltpu.VMEM((MIX_ROWS + CONV_HALO, G_WIDTH), F32)],
        compiler_params=_params(2),
    )(x, g.reshape(1, d), w, poolw.astype(BF16), pool_scale.reshape(1, G_WIDTH), convw)


def _block_diag(x, mask01):
    return jnp.concatenate([x] * N_HEADS, axis=0) * mask01


def _rwkv_kernel(pb_ref, mu_ref, w0_ref, wup_ref, a0_ref, aup_ref, gup_ref, kk_ref, ka_ref,
                 rk_ref, lnw_ref, lnb_ref, o_ref,
                 ext, state, r_s, k_s, v_s, lw_s, a_s, b_s, y_s):
    si = pl.program_id(1)
    group, seq_rows = pb_ref.shape[0], pb_ref.shape[1]
    g_w = G_WIDTH
    chunk = RWKV_CHUNK

    @pl.when(si == 0)
    def _():
        ext[:, 0:8, :] = jnp.zeros((group, 8, PB_COLS), F32)
        state[...] = jnp.zeros_like(state)

    shifted = []
    for b in range(group):
        p_b = pb_ref[b]
        ext[b, 8:, :] = p_b
        prev = ext[b, 7:7 + seq_rows, :]
        shifted.append(p_b + mu_ref[...] * (prev - p_b))
        ext[b, 0:8, :] = ext[b, seq_rows:seq_rows + 8, :]
    ps = jnp.concatenate(shifted, axis=0)

    head_mask = _head_ones()
    head_ones = head_mask.astype(BF16)

    def head_sum(t):
        return _dot2_exact_rhs(t, head_ones)

    r = ps[:, 0:g_w]
    k = ps[:, g_w:2 * g_w]
    v = ps[:, 2 * g_w:3 * g_w]
    tail = ps[:, 3 * g_w:4 * g_w]
    w = w0_ref[...] + _dot(jnp.tanh(tail), wup_ref[...])
    lw = -jnp.exp(-0.5) * _sigmoid(w)
    a = _sigmoid(a0_ref[...] + _dot(tail, aup_ref[...]))
    gate = _dot(_sigmoid(tail), gup_ref[...])
    kk = k * kk_ref[...]
    kk = kk * lax.rsqrt(jnp.maximum(head_sum(kk * kk), 1e-24))
    k2 = k * (1.0 + (a - 1.0) * ka_ref[...])
    bonus = head_sum(r * k2 * rk_ref[...]) * v
    r_s[...] = r
    k_s[...] = k2
    v_s[...] = v
    lw_s[...] = lw
    a_s[...] = -kk
    b_s[...] = kk * a

    row_i = lax.broadcasted_iota(jnp.int32, (chunk, g_w), 0)
    col_j = lax.broadcasted_iota(jnp.int32, (chunk, g_w), 1) % chunk
    strict_lower = row_i > col_j
    lower = row_i >= col_j
    tri = (lax.broadcasted_iota(jnp.int32, (chunk, chunk), 0)
           >= lax.broadcasted_iota(jnp.int32, (chunk, chunk), 1)).astype(BF16)
    n_doublings = chunk.bit_length() - 1

    head_mask16 = head_ones

    def bd(x):
        return _block_diag(x.astype(BF16), head_mask16)

    def mm(a, b16):
        return jnp.dot(a.astype(BF16), b16, preferred_element_type=F32)

    ids = range(group)

    def group_local(c):
        sls = [pl.ds(pl.multiple_of(i * seq_rows + c * chunk, chunk), chunk) for i in ids]
        r_c, k_c, v_c, lw_c, a_c, b_c = ([t[sl, :] for sl in sls]
                                         for t in (r_s, k_s, v_s, lw_s, a_s, b_s))
        cw = [_dot2_exact_lhs(tri, lw_c[i]) for i in ids]
        cw_last = [cw[i][chunk - 1:chunk, :] for i in ids]
        a_t = [a_c[i] * jnp.exp(cw[i] - lw_c[i]) for i in ids]
        r_t = [r_c[i] * jnp.exp(cw[i]) for i in ids]
        e_inv = [jnp.exp(-cw[i]) for i in ids]
        bd_v = [bd(v_c[i]) for i in ids]
        pair = [lax.dot_general(
            jnp.concatenate([a_t[i], r_t[i]], axis=0).astype(BF16),
            jnp.concatenate([bd(b_c[i] * e_inv[i]), bd(k_c[i] * e_inv[i])], axis=0),
            (((1,), (1,)), ((), ())), preferred_element_type=F32) for i in ids]
        l_ak = [jnp.where(strict_lower, pair[i][0:chunk, g_w:2 * g_w], 0.0) for i in ids]
        m_rb = [jnp.where(lower, pair[i][chunk:2 * chunk, 0:g_w], 0.0) for i in ids]
        m_rk = [jnp.where(lower, pair[i][chunk:2 * chunk, g_w:2 * g_w], 0.0) for i in ids]

        x1 = a_t
        x2 = [mm(l_ak[i], bd_v[i]) for i in ids]
        l_pow = [jnp.where(strict_lower, pair[i][0:chunk, 0:g_w], 0.0) for i in ids]
        for it in range(n_doublings):
            last = it == n_doublings - 1
            res = [mm(l_pow[i], jnp.concatenate(
                [bd(x1[i]), bd(x2[i])] + ([] if last else [bd(l_pow[i])]), axis=1)) for i in ids]
            x1 = [x1[i] + res[i][:, 0:g_w] for i in ids]
            x2 = [x2[i] + res[i][:, g_w:2 * g_w] for i in ids]
            if not last:
                l_pow = [res[i][:, 2 * g_w:3 * g_w] for i in ids]

        q = [mm(m_rb[i], jnp.concatenate([bd(x1[i]), bd(x2[i])], axis=1)) for i in ids]
        q2 = [q[i][:, g_w:2 * g_w] + mm(m_rk[i], bd_v[i]) for i in ids]
        lhs = [jnp.concatenate([x1[i], r_t[i] + q[i][:, 0:g_w]], axis=0).astype(BF16) for i in ids]
        rhs = []
        for i in ids:
            e_fut = jnp.exp(cw_last[i] - cw[i])
            rhs.append(jnp.concatenate([b_c[i] * e_fut, k_c[i] * e_fut], axis=0).astype(BF16))
        decay = [jnp.exp(cw_last[i]) for i in ids]

        s0 = [state[i] for i in ids]
        us = [lax.dot_general(lhs[i], s0[i].astype(BF16), (((1,), (1,)), ((), ())),
                              preferred_element_type=F32) for i in ids]
        uv_t = []
        for i in ids:
            y_s[sls[i], :] = us[i][chunk:2 * chunk] + q2[i]
            u = us[i][0:chunk] + x2[i]
            uv_t.append(jnp.transpose(jnp.concatenate([u, v_c[i]], axis=0)).astype(BF16))
        upd = [jnp.dot(uv_t[i], rhs[i], preferred_element_type=F32) for i in ids]
        for i in ids:
            state[i] = s0[i] * decay[i] + jnp.where(head_mask, upd[i], 0.0)

    def chunk_step(c, carry):
        group_local(c)
        return carry

    lax.fori_loop(0, seq_rows // chunk, chunk_step, 0)

    y = y_s[...]
    inv_n = 1.0 / HEAD_DIM
    mean = head_sum(y) * inv_n
    dev = y - mean
    var = head_sum(dev * dev) * inv_n
    yn = dev * lax.rsqrt(var + RWKV_LN_EPS) * lnw_ref[...] + lnb_ref[...]
    out = (yn + bonus) * gate
    for b in range(group):
        o_ref[b] = out[b * seq_rows:(b + 1) * seq_rows, :]


def _rwkv_call(pb, mu, w0, w_up, a0, a_up, g_up, k_k, k_a, r_k, ln_w, ln_b):
    b, s, _ = pb.shape
    g_w = G_WIDTH
    row = lambda t: t.reshape(1, g_w)
    o_a = RWKV_W_RANK
    o_g = o_a + RWKV_A_RANK
    wup = jnp.zeros((g_w, g_w), F32).at[0:o_a].set(w_up).astype(BF16)
    aup = jnp.zeros((g_w, g_w), F32).at[o_a:o_g].set(a_up).astype(BF16)
    gup = jnp.zeros((g_w, g_w), F32).at[o_g:o_g + RWKV_G_RANK].set(g_up).astype(BF16)
    mu_p = _pad_cols(mu.reshape(1, -1), PB_COLS)
    vec = _const_spec((1, g_w))
    mat = _const_spec((g_w, g_w))
    group = min(RWKV_GROUP, b)
    seq_rows = MIX_ROWS // group
    seq = pltpu.VMEM((MIX_ROWS, g_w), F32)
    return pl.pallas_call(
        _rwkv_kernel,
        grid=(b // group, s // seq_rows),
        in_specs=[pl.BlockSpec((group, seq_rows, PB_COLS), lambda bi, si: (bi, si, 0)),
                  _const_spec((1, PB_COLS)), vec, mat, vec, mat, mat, vec, vec, vec, vec, vec],
        out_specs=pl.BlockSpec((group, seq_rows, g_w), lambda bi, si: (bi, si, 0)),
        out_shape=jax.ShapeDtypeStruct((b, s, g_w), F32),
        scratch_shapes=[pltpu.VMEM((group, seq_rows + 8, PB_COLS), F32),
                        pltpu.VMEM((group, g_w, g_w), F32),
                        seq, seq, seq, seq, seq, seq, seq],
        compiler_params=_params(2),
    )(pb, mu_p, row(w0), wup, row(a0), aup, gup, row(k_k), row(k_a), row(r_k), row(ln_w), row(ln_b))


def _gelu_tanh(x):
    return 0.5 * x * (1.0 + jnp.tanh(0.7978845608028654 * (x + 0.044715 * x * x * x)))


def _pos_features(pos, width):
    lane = lax.broadcasted_iota(jnp.int32, (pos.shape[0], width), 1)
    feat = jnp.where(lane == 0, pos // ALIBI_SPLIT,
                     jnp.where(lane == 1, pos % ALIBI_SPLIT, jnp.where(lane == 2, 1, 0)))
    return feat.astype(F32).astype(BF16)


def _nsa_compress_kernel(kc_ref, vc_ref, pos_ref, kw1_ref, kw2_ref, vw1_ref, vw2t_ref, kn_ref,
                         kcmp_ref, vcmpt_ref):
    half = kw1_ref.shape[0] // 2
    n_rows = kc_ref.shape[1]

    def hidden(x, w1_ref):
        first = _dot(x + pos_ref[0:1, :], w1_ref[0:half, :])
        second = _dot(x + pos_ref[1:2, :], w1_ref[half:2 * half, :])
        return _gelu_tanh(first + pltpu.roll(second, n_rows - 1, 0))

    k_cmp = _dot(hidden(kc_ref[0], kw1_ref), kw2_ref[...])
    kcmp_ref[0, :, 0:HEAD_DIM] = _rms_rows(k_cmp, kn_ref[...]).astype(BF16)
    last_token = (lax.broadcasted_iota(jnp.int32, (n_rows, 1), 0) * NSA_CMP_STRIDE
                  + (NSA_CMP_LEN - 1))
    kcmp_ref[0, :, HEAD_DIM:2 * HEAD_DIM] = _pos_features(last_token, HEAD_DIM)
    vcmpt_ref[0] = _dot_nt(vw2t_ref[...], hidden(vc_ref[0], vw1_ref)).astype(BF16)


def _nsa_compress_call(kcx, vcx, pos, kw1, kw2, vw1, vw2, k_norm0):
    b, n_chunks, width = kcx.shape
    hid = kw1.shape[1]
    pos2 = jnp.pad(pos.reshape(2, width), ((0, 6), (0, 0)))
    blk = pl.BlockSpec((1, n_chunks, width), lambda bi: (bi, 0, 0))
    return pl.pallas_call(
        _nsa_compress_kernel,
        grid=(b,),
        in_specs=[blk, blk, _const_spec((8, width)),
                  _const_spec((2 * width, hid)), _const_spec((hid, HEAD_DIM)),
                  _const_spec((2 * width, hid)), _const_spec((HEAD_DIM, hid)),
                  _const_spec((1, HEAD_DIM))],
        out_specs=[pl.BlockSpec((1, n_chunks, 2 * HEAD_DIM), lambda bi: (bi, 0, 0)),
                   pl.BlockSpec((1, HEAD_DIM, n_chunks), lambda bi: (bi, 0, 0))],
        out_shape=[jax.ShapeDtypeStruct((b, n_chunks, 2 * HEAD_DIM), BF16),
                   jax.ShapeDtypeStruct((b, HEAD_DIM, n_chunks), BF16)],
        compiler_params=_params(1),
    )(kcx, vcx, pos2, kw1.astype(BF16), kw2.astype(BF16), vw1.astype(BF16),
      vw2.T.astype(BF16), k_norm0.reshape(1, HEAD_DIM))


def _nsa_kv_kernel(k_ref, v_ref, kn_ref, kk_ref, vt_ref):
    x = k_ref[0]
    first = lax.broadcasted_iota(jnp.int32, x.shape, 1) < HEAD_DIM
    sq = x * x
    ms_first = jnp.sum(jnp.where(first, sq, 0.0), axis=-1, keepdims=True) * (1.0 / HEAD_DIM)
    ms_second = jnp.sum(jnp.where(first, 0.0, sq), axis=-1, keepdims=True) * (1.0 / HEAD_DIM)
    inv = jnp.where(first, lax.rsqrt(ms_first + RMS_EPS), lax.rsqrt(ms_second + RMS_EPS))
    kk_ref[0, :, 0:LANES] = (x * inv * kn_ref[...]).astype(BF16)
    rows = x.shape[0]
    pos = pl.program_id(1) * rows + lax.broadcasted_iota(jnp.int32, (rows, 1), 0)
    kk_ref[0, :, LANES:2 * LANES] = _pos_features(pos, LANES)
    v = v_ref[0]
    for j in range(vt_ref.shape[1]):
        vt_ref[0, j] = jnp.transpose(v[j * KEY_STEP:(j + 1) * KEY_STEP, :]).astype(BF16)


def _nsa_kv_call(pc, k_norm_sel, k_norm_win):
    b, s, _ = pc.shape
    kn = jnp.concatenate([k_norm_sel, k_norm_win]).reshape(1, 2 * HEAD_DIM)
    per_step = MIX_ROWS // KEY_STEP
    return pl.pallas_call(
        _nsa_kv_kernel,
        grid=(b, s // MIX_ROWS),
        in_specs=[pl.BlockSpec((1, MIX_ROWS, LANES), lambda bi, si: (bi, si, 3)),
                  pl.BlockSpec((1, MIX_ROWS, LANES), lambda bi, si: (bi, si, 4)),
                  _const_spec((1, LANES))],
        out_specs=[pl.BlockSpec((1, MIX_ROWS, 2 * LANES), lambda bi, si: (bi, si, 0)),
                   pl.BlockSpec((1, per_step, LANES, KEY_STEP), lambda bi, si: (bi, si, 0, 0))],
        out_shape=[jax.ShapeDtypeStruct((b, s, 2 * LANES), BF16),
                   jax.ShapeDtypeStruct((b, s // KEY_STEP, LANES, KEY_STEP), BF16)],
        compiler_params=_params(2),
    )(pc, pc, kn)


def _col_reduce(x, pair_op, reduce_fn, slab=32):
    parts = [x[i:i + slab] for i in range(0, x.shape[0], slab)]
    while len(parts) > 1:
        parts = [pair_op(parts[i], parts[i + 1]) for i in range(0, len(parts), 2)]
    return reduce_fn(parts[0], axis=0, keepdims=True)


def _nsa_attn_kernel(q_ref, gate_ref, qn_ref, kc_ref, vct_ref, kk_ref, vt_ref, o_ref, s_ref):
    qi = pl.program_id(1)
    tq = ATT_ROWS
    hw = N_HEADS * tq
    q0 = qi * tq
    n_cmp_rows = kc_ref.shape[1]
    n_sel = kk_ref.shape[1] // NSA_SEL_BLOCK

    head_ones = _head_ones().astype(BF16)
    q = q_ref[0]
    ms = _dot2_exact_rhs(q * q, head_ones) * (1.0 / HEAD_DIM)
    qn = q * lax.rsqrt(ms + RMS_EPS) * qn_ref[...] * (HEAD_DIM ** -0.5)
    qt = jnp.transpose(qn)
    q4t = jnp.concatenate([qt[h * HEAD_DIM:(h + 1) * HEAD_DIM, :] for h in range(N_HEADS)],
                          axis=1).astype(BF16)
    lane = lax.broadcasted_iota(jnp.int32, (1, hw), 1)
    slope = jnp.exp2(-2.0 * (lane // tq + 1).astype(F32))
    t_lane = lane % tq

    f_row = lax.broadcasted_iota(jnp.int32, (HEAD_DIM, hw), 0)
    q_feat = jnp.where(f_row == 0, ALIBI_SPLIT * slope,
                       jnp.where(f_row == 1, slope,
                                 jnp.where(f_row == 2, -slope * q0.astype(F32), 0.0))).astype(BF16)
    zero_half = jnp.zeros_like(q4t)
    q_cmp = jnp.concatenate([q4t, q_feat], axis=0)
    q_sel = jnp.concatenate([q4t, zero_half, q_feat, zero_half], axis=0)
    q_win = jnp.concatenate([zero_half, q4t, q_feat, zero_half], axis=0)

    n_idx = lax.broadcasted_iota(jnp.int32, (n_cmp_rows, hw), 0)
    valid_c = (q0 + t_lane) >= (n_idx * NSA_CMP_STRIDE + (NSA_CMP_LEN - 1))
    s_c = (jnp.dot(kc_ref[0], q_cmp, preferred_element_type=F32)
           + jnp.where(valid_c, 0.0, NSA_NEG))
    m_c = jnp.maximum(jnp.max(s_c, axis=0, keepdims=True), SCORE_FLOOR)
    p_c = jnp.exp(s_c - m_c)
    l_c = jnp.sum(p_c, axis=0, keepdims=True)
    p_c = p_c * (1.0 / jnp.maximum(l_c, 1e-30))
    o_cmp = jnp.dot(vct_ref[0], p_c.astype(BF16), preferred_element_type=F32)

    p_heads = p_c[:, 0:tq]
    for h in range(1, N_HEADS):
        p_heads = p_heads + p_c[:, h * tq:(h + 1) * tq]
    per_sel = NSA_SEL_BLOCK // NSA_CMP_STRIDE
    j_ov = lax.broadcasted_iota(jnp.int32, (n_sel, n_cmp_rows), 0)
    n_ov = lax.broadcasted_iota(jnp.int32, (n_sel, n_cmp_rows), 1)
    overlap_t = ((n_ov >= per_sel * j_ov - (NSA_CMP_LEN // NSA_CMP_STRIDE - 1))
                 & (n_ov <= per_sel * j_ov + per_sel - 1)).astype(BF16)
    p_hi = p_heads.astype(BF16)
    p_rest = p_heads - p_hi.astype(F32)
    p_mid = p_rest.astype(BF16)
    p_lo = (p_rest - p_mid.astype(F32)).astype(BF16)
    imp = (jnp.dot(overlap_t, p_hi, preferred_element_type=F32)
           + jnp.dot(overlap_t, p_mid, preferred_element_type=F32)
           + jnp.dot(overlap_t, p_lo, preferred_element_type=F32))
    j_idx = lax.broadcasted_iota(jnp.int32, (n_sel, tq), 0)
    cur = (q0 + lax.broadcasted_iota(jnp.int32, (1, tq), 1)) // NSA_SEL_BLOCK
    forced = (j_idx == 0) | (j_idx == cur) | (j_idx == cur - 1)
    imp = jnp.where(j_idx <= cur, imp + jnp.where(forced, NSA_FORCE_BONUS, 0.0), -1.0)
    rank = jnp.zeros((n_sel, tq), F32)
    for jp in range(n_sel):
        other = imp[jp:jp + 1, :]
        ahead = (other > imp) | ((other == imp) & (j_idx > jp))
        rank = rank + jnp.where(ahead, 1.0, 0.0)
    chosen = jnp.where(rank < float(min(NSA_TOP_N, n_sel)), 0.0, NSA_NEG)

    key_i = lax.broadcasted_iota(jnp.int32, (tq, tq), 0)
    query_i = lax.broadcasted_iota(jnp.int32, (tq, tq), 1)
    causal = jnp.where(query_i >= key_i, 0.0, NSA_NEG)
    window_start = jnp.where(key_i > query_i, 0.0, NSA_NEG)
    tiles_per_step = KEY_STEP // tq
    blocks_per_tile = tq // NSA_SEL_BLOCK
    win_tiles = NSA_WINDOW // tq
    last_step = qi // tiles_per_step

    def raw_scores(p, q_pad):
        k0 = pl.multiple_of(p * KEY_STEP, KEY_STEP)
        return jnp.dot(kk_ref[0, pl.ds(k0, KEY_STEP), :], q_pad, preferred_element_type=F32)

    def add_bias(raw, bias):
        return raw + jnp.concatenate([bias] * N_HEADS, axis=1)

    def absorb(p, m, l, v_rows):
        s = s_ref[...]
        m_new = jnp.maximum(m, _col_reduce(s, jnp.maximum, jnp.max))
        alpha = jnp.exp(m - m_new)
        prob = jnp.exp(s - m_new)
        l = alpha * l + _col_reduce(prob, jnp.add, jnp.sum)
        v_t = vt_ref[0, p][v_rows[0]:v_rows[1], :]
        return m_new, l, alpha, jnp.dot(v_t, prob.astype(BF16), preferred_element_type=F32)

    def branch(first, last, q_pad, v_rows, bias_fn):
        def body(p, carry):
            m, l, acc = carry
            raw_next = raw_scores(p + 1, q_pad)
            m, l, alpha, pv = absorb(p, m, l, v_rows)
            s_ref[...] = add_bias(raw_next, bias_fn(p + 1))
            return m, l, alpha * acc + pv

        s_ref[...] = add_bias(raw_scores(first, q_pad), bias_fn(first))
        init = (jnp.full((1, hw), SCORE_FLOOR, F32), jnp.zeros((1, hw), F32),
                jnp.zeros((HEAD_DIM, hw), F32))
        m, l, acc = lax.fori_loop(first, last, body, init)
        _, l, alpha, pv = absorb(last, m, l, v_rows)
        return (alpha * acc + pv) * (1.0 / l)

    def sel_tile_bias(kt):
        rows = [jnp.max(jnp.where(j_idx == blocks_per_tile * kt + i, chosen, NSA_NEG),
                        axis=0, keepdims=True) for i in range(blocks_per_tile)]
        bias = rows[-1]
        for i in range(blocks_per_tile - 2, -1, -1):
            bias = jnp.where(key_i < (i + 1) * NSA_SEL_BLOCK, rows[i], bias)
        return bias

    def sel_step_bias(p):
        tiles = []
        for i in range(tiles_per_step):
            kt = p * tiles_per_step + i
            tiles.append(sel_tile_bias(kt) + jnp.where(kt == qi, causal, 0.0)
                         + jnp.where(kt > qi, NSA_NEG, 0.0))
        return jnp.concatenate(tiles, axis=0)

    def win_step_bias(p):
        tiles = []
        for i in range(tiles_per_step):
            behind = qi - (p * tiles_per_step + i)
            bias = jnp.where(behind == 0, causal, jnp.where(behind == win_tiles, window_start, 0.0))
            tiles.append(bias + jnp.where((behind >= 0) & (behind <= win_tiles), 0.0, NSA_NEG))
        return jnp.concatenate(tiles, axis=0)

    o_sel = branch(0, last_step, q_sel, (0, HEAD_DIM), sel_step_bias)
    first_win_step = jnp.maximum(qi - win_tiles, 0) // tiles_per_step
    o_win = branch(first_win_step, last_step, q_win, (HEAD_DIM, 2 * HEAD_DIM), win_step_bias)

    g_t = jnp.transpose(_sigmoid(gate_ref[0]))

    def gate_row(c):
        return jnp.concatenate([g_t[3 * h + c:3 * h + c + 1, :] for h in range(N_HEADS)], axis=1)

    o = gate_row(0) * o_cmp + gate_row(1) * o_sel + gate_row(2) * o_win
    o_hd = jnp.concatenate([o[:, h * tq:(h + 1) * tq] for h in range(N_HEADS)], axis=0)
    o_ref[0] = jnp.transpose(o_hd)


def _nsa_attn_call(pc, q_norm, kcmp, vcmpt, kk, vt):
    b, s, _ = pc.shape
    n_cmp_rows = kcmp.shape[1]
    qn = jnp.tile(q_norm, N_HEADS).reshape(1, G_WIDTH)
    return pl.pallas_call(
        _nsa_attn_kernel,
        grid=(b, s // ATT_ROWS),
        in_specs=[pl.BlockSpec((1, ATT_ROWS, G_WIDTH), lambda bi, qi: (bi, qi, 0)),
                  pl.BlockSpec((1, ATT_ROWS, LANES), lambda bi, qi: (bi, qi, 5)),
                  _const_spec((1, G_WIDTH)),
                  pl.BlockSpec((1, n_cmp_rows, 2 * HEAD_DIM), lambda bi, qi: (bi, 0, 0)),
                  pl.BlockSpec((1, HEAD_DIM, n_cmp_rows), lambda bi, qi: (bi, 0, 0)),
                  pl.BlockSpec((1, s, 2 * LANES), lambda bi, qi: (bi, 0, 0)),
                  pl.BlockSpec((1, s // KEY_STEP, LANES, KEY_STEP), lambda bi, qi: (bi, 0, 0, 0))],
        out_specs=pl.BlockSpec((1, ATT_ROWS, G_WIDTH), lambda bi, qi: (bi, qi, 0)),
        out_shape=jax.ShapeDtypeStruct((b, s, G_WIDTH), F32),
        scratch_shapes=[pltpu.VMEM((KEY_STEP, N_HEADS * ATT_ROWS), F32)],
        compiler_params=_params(2),
    )(pc, pc, qn, kcmp, vcmpt, kk, vt)


def _nsa_call(pc, q_norm, k_norm, cmp_pos, kw1, kw2, vw1, vw2):
    b, s, _ = pc.shape
    n_chunks = s // NSA_CMP_STRIDE
    width = NSA_CMP_STRIDE * HEAD_DIM
    kcx = pc[..., G_WIDTH:G_WIDTH + HEAD_DIM].reshape(b, n_chunks, width)
    vcx = pc[..., G_WIDTH + HEAD_DIM:G_WIDTH + 2 * HEAD_DIM].reshape(b, n_chunks, width)
    kcmp, vcmpt = _nsa_compress_call(kcx, vcx, cmp_pos, kw1, kw2, vw1, vw2, k_norm[0])
    kk, vt = _nsa_kv_call(pc, k_norm[1], k_norm[2])
    return _nsa_attn_call(pc, q_norm, kcmp, vcmpt, kk, vt)


def kernel(x, ffn1_norm, ffn1_w_gate, ffn1_w_up, ffn1_w_down, mix_norm, w_in, pool_w, pool_scale, rwkv_mu, rwkv_w0, rwkv_w_up, rwkv_a0, rwkv_a_up, rwkv_g_up, rwkv_k_k, rwkv_k_a, rwkv_r_k, rwkv_ln_w, rwkv_ln_b, nsa_q_norm, nsa_k_norm, nsa_cmp_pos, nsa_cmp_k_w1, nsa_cmp_k_w2, nsa_cmp_v_w1, nsa_cmp_v_w2, conv_w, w_out, ffn2_norm, ffn2_w_gate, ffn2_w_up, ffn2_w_down):
    b, s, d = x.shape
    n_tok = b * s
    x2d = x.reshape(n_tok, d)
    for l in range(ffn1_norm.shape[0]):
        x2d = _ffn_call(x2d, ffn1_norm[l], ffn1_w_gate[l], ffn1_w_up[l], ffn1_w_down[l])
        ya, pb, pc, yd = _mixin_call(x2d.reshape(b, s, d), mix_norm[l], w_in[l], pool_w[l],
                                     pool_scale[l], conv_w[l])
        yb = _rwkv_call(pb, rwkv_mu[l], rwkv_w0[l], rwkv_w_up[l], rwkv_a0[l], rwkv_a_up[l],
                        rwkv_g_up[l], rwkv_k_k[l], rwkv_k_a[l], rwkv_r_k[l], rwkv_ln_w[l],
                        rwkv_ln_b[l])
        yc = _nsa_call(pc, nsa_q_norm[l], nsa_k_norm[l], nsa_cmp_pos[l], nsa_cmp_k_w1[l],
                       nsa_cmp_k_w2[l], nsa_cmp_v_w1[l], nsa_cmp_v_w2[l])
        mix = tuple(t.reshape(n_tok, G_WIDTH) for t in (ya, yb, yc, yd))
        x2d = _ffn_call(x2d, ffn2_norm[l], ffn2_w_gate[l], ffn2_w_up[l], ffn2_w_down[l],
                        mix=mix, w_out=w_out[l])
    return x2d.reshape(b, s, d)
```

```python
import functools

import jax
import jax.numpy as jnp
from jax import lax
from jax.experimental import pallas as pl
from jax.experimental.pallas import tpu as pltpu

F32 = jnp.float32
BF16 = jnp.bfloat16

N_MIXERS = 4
HEAD_DIM = 64
N_HEADS = 4
G_WIDTH = N_HEADS * HEAD_DIM
RMS_EPS = 1e-6
POOL_WINDOWS = (2, 4, 8, 16)
POOL_HALO = 16
CONV_HALO = 8
RWKV_W_RANK, RWKV_A_RANK, RWKV_G_RANK = 64, 32, 64
RWKV_LN_EPS = 64e-5
RWKV_CHUNK = 64
RWKV_GROUP = 4
NSA_CMP_LEN = 32
NSA_CMP_STRIDE = 16
NSA_SEL_BLOCK = 64
NSA_TOP_N = 16
NSA_WINDOW = 512
NSA_FORCE_BONUS = 1e4
NSA_NEG = -1e9
SCORE_FLOOR = 0.5 * NSA_NEG
ALIBI_SPLIT = 64

PB_COLS = 1024
PC_COLS = 768
PD_COLS = 768
P_PAD = G_WIDTH + PB_COLS + PC_COLS + PD_COLS

V7X_VMEM_BYTES = 64 * 1024 * 1024
VMEM_LIMIT = V7X_VMEM_BYTES - 8 * 1024 * 1024
LANES = 128

FFN_ROWS = 512
FFN_COLS = 256
MIX_ROWS = 512
ATT_ROWS = 128
KEY_STEP = 2 * ATT_ROWS


def _params(n_axes):
    return pltpu.CompilerParams(dimension_semantics=("arbitrary",) * n_axes,
                                vmem_limit_bytes=VMEM_LIMIT)


def _const_spec(shape):
    nd = len(shape)
    return pl.BlockSpec(shape, lambda *_: (0,) * nd, pipeline_mode=pl.Buffered(1))


def _dot(a, b):
    return jnp.dot(a.astype(BF16), b.astype(BF16), preferred_element_type=F32)


def _dot_nt(a, b):
    return lax.dot_general(a.astype(BF16), b.astype(BF16), (((1,), (1,)), ((), ())),
                           preferred_element_type=F32)


def _split(x):
    hi = x.astype(BF16)
    lo = (x - hi.astype(F32)).astype(BF16)
    return hi, lo


def _dot2_exact_rhs(a, b_bf16):
    ah, al = _split(a)
    return (jnp.dot(ah, b_bf16, preferred_element_type=F32)
            + jnp.dot(al, b_bf16, preferred_element_type=F32))


def _dot2_exact_lhs(a_bf16, b):
    bh, bl = _split(b)
    return (jnp.dot(a_bf16, bh, preferred_element_type=F32)
            + jnp.dot(a_bf16, bl, preferred_element_type=F32))


def _rms_rows(x, g):
    return x * lax.rsqrt(jnp.mean(x * x, axis=-1, keepdims=True) + RMS_EPS) * g


def _sigmoid(x):
    return 1.0 / (1.0 + jnp.exp(-x))


def _head_ones():
    r = lax.broadcasted_iota(jnp.int32, (G_WIDTH, G_WIDTH), 0) // HEAD_DIM
    c = lax.broadcasted_iota(jnp.int32, (G_WIDTH, G_WIDTH), 1) // HEAD_DIM
    return r == c


def _ffn_body(x_in, g_ref, wg_ref, wu_ref, wd_ref, o_ref, acc_ref):
    h = _rms_rows(x_in, g_ref[...]).astype(BF16)
    for c in range(wg_ref.shape[1] // FFN_COLS):
        cols = slice(c * FFN_COLS, (c + 1) * FFN_COLS)
        gate = jnp.dot(h, wg_ref[:, cols], preferred_element_type=F32)
        up = jnp.dot(h, wu_ref[:, cols], preferred_element_type=F32)
        act = (gate * _sigmoid(gate) * up).astype(BF16)
        down = jnp.dot(act, wd_ref[cols, :], preferred_element_type=F32)
        if c == 0:
            acc_ref[...] = down
        else:
            acc_ref[...] += down
    o_ref[...] = x_in + 0.5 * acc_ref[...]


def _ffn_kernel(x_ref, g_ref, wg_ref, wu_ref, wd_ref, o_ref, acc_ref):
    _ffn_body(x_ref[...], g_ref, wg_ref, wu_ref, wd_ref, o_ref, acc_ref)


def _out_ffn_kernel(x_ref, ya_ref, yb_ref, yc_ref, yd_ref, wo_ref,
                    g_ref, wg_ref, wu_ref, wd_ref, o_ref, acc_ref):
    x1 = x_ref[...]
    for i, y_ref in enumerate((ya_ref, yb_ref, yc_ref, yd_ref)):
        x1 = x1 + _dot(y_ref[...], wo_ref[i])
    _ffn_body(x1, g_ref, wg_ref, wu_ref, wd_ref, o_ref, acc_ref)


def _ffn_call(x2d, g, w_gate, w_up, w_down, mix=None, w_out=None):
    n_tok, d = x2d.shape
    wg, wu, wd = w_gate.astype(BF16), w_up.astype(BF16), w_down.astype(BF16)
    row_spec = pl.BlockSpec((FFN_ROWS, d), lambda i: (i, 0))
    w_specs = [_const_spec((1, d)), _const_spec(wg.shape), _const_spec(wu.shape),
               _const_spec(wd.shape)]
    w_args = [g.reshape(1, d), wg, wu, wd]
    if mix is None:
        kern, in_specs, args = _ffn_kernel, [row_spec] + w_specs, [x2d] + w_args
    else:
        y_spec = pl.BlockSpec((FFN_ROWS, G_WIDTH), lambda i: (i, 0))
        wo = w_out.astype(BF16).reshape(N_MIXERS, G_WIDTH, d)
        kern = _out_ffn_kernel
        in_specs = [row_spec] + [y_spec] * N_MIXERS + [_const_spec(wo.shape)] + w_specs
        args = [x2d] + list(mix) + [wo] + w_args
    return pl.pallas_call(
        kern,
        grid=(n_tok // FFN_ROWS,),
        in_specs=in_specs,
        out_specs=row_spec,
        out_shape=jax.ShapeDtypeStruct((n_tok, d), F32),
        scratch_shapes=[pltpu.VMEM((FFN_ROWS, d), F32)],
        compiler_params=_params(1),
    )(*args)


def _mixin_kernel(x_ref, g_ref, w_ref, poolw_ref, pools_ref, convw_ref,
                  ya_ref, pb_ref, pc_ref, yd_ref, pa_ext, z_ext):
    si = pl.program_id(1)
    rows = x_ref.shape[1]
    h = _rms_rows(x_ref[0], g_ref[...]).astype(BF16)
    c0, c1, c2 = G_WIDTH, G_WIDTH + PB_COLS, G_WIDTH + PB_COLS + PC_COLS
    pb_ref[0] = jnp.dot(h, w_ref[:, c0:c1], preferred_element_type=F32)
    pc_ref[0] = jnp.dot(h, w_ref[:, c1:c2], preferred_element_type=F32)

    @pl.when(si == 0)
    def _():
        pa_ext[0:POOL_HALO, :] = jnp.zeros((POOL_HALO, G_WIDTH), F32)
        z_ext[0:CONV_HALO, :] = jnp.zeros((CONV_HALO, G_WIDTH), F32)

    u = jnp.dot(h, w_ref[:, 0:c0], preferred_element_type=F32)
    pa_ext[POOL_HALO:, :] = u
    lane_group = lax.broadcasted_iota(jnp.int32, (rows, G_WIDTH), 1) // (G_WIDTH // len(POOL_WINDOWS))
    pos = si * rows + lax.broadcasted_iota(jnp.int32, (rows, G_WIDTH), 0)
    total = u
    for k in range(1, max(POOL_WINDOWS)):
        first_group = sum(1 for w in POOL_WINDOWS if w <= k)
        shifted = pa_ext[POOL_HALO - k:POOL_HALO - k + rows, :]
        total = total + jnp.where(lane_group >= first_group, shifted, 0.0)
    window = jnp.left_shift(2, lane_group)
    cnt = jnp.minimum(pos + 1, window).astype(F32)
    pooled = total / cnt
    ya = _dot(pooled - u, poolw_ref[...]) * pools_ref[...]
    ya_ref[0] = ya
    pa_ext[0:POOL_HALO, :] = pa_ext[rows:rows + POOL_HALO, :]

    pd = jnp.dot(h, w_ref[:, c2:c2 + PD_COLS], preferred_element_type=F32)
    cu = pd[:, 0:G_WIDTH]
    cb = pd[:, G_WIDTH:2 * G_WIDTH]
    cc = pd[:, 2 * G_WIDTH:3 * G_WIDTH]
    z = cc * cu
    z_ext[CONV_HALO:, :] = z
    y = (convw_ref[0:1, :] * z_ext[CONV_HALO - 2:CONV_HALO - 2 + rows, :]
         + convw_ref[1:2, :] * z_ext[CONV_HALO - 1:CONV_HALO - 1 + rows, :]
         + convw_ref[2:3, :] * z)
    yd_ref[0] = cb * y
    z_ext[0:CONV_HALO, :] = z_ext[rows:rows + CONV_HALO, :]


def _pad_cols(w, width):
    return jnp.pad(w, ((0, 0), (0, width - w.shape[1])))


def _w_in_padded(w_in):
    g = G_WIDTH
    o = 0
    w_a = w_in[:, o:o + g]; o += g
    rwkv_cols = 3 * g + RWKV_W_RANK + RWKV_A_RANK + RWKV_G_RANK
    w_b = _pad_cols(w_in[:, o:o + rwkv_cols], PB_COLS); o += rwkv_cols
    q = w_in[:, o:o + g]; o += g
    kc, vc, ksl, vsl, kwn, vwn = [w_in[:, o + i * HEAD_DIM:o + (i + 1) * HEAD_DIM] for i in range(6)]
    o += 6 * HEAD_DIM
    gates = w_in[:, o:o + 3 * N_HEADS]; o += 3 * N_HEADS
    w_c = jnp.concatenate([q, kc, vc, ksl, kwn, vsl, vwn, _pad_cols(gates, LANES)], axis=1)
    w_d = w_in[:, o:o + PD_COLS]
    return jnp.concatenate([w_a, w_b, w_c, w_d], axis=1).astype(BF16)


def _mixin_call(x, g, w_in, pool_w, pool_scale, conv_w):
    b, s, d = x.shape
    w = _w_in_padded(w_in)
    n_groups = len(POOL_WINDOWS)
    pool_ch = G_WIDTH // n_groups
    poolw = jnp.zeros((G_WIDTH, G_WIDTH), F32)
    for gi in range(n_groups):
        poolw = poolw.at[gi * pool_ch:(gi + 1) * pool_ch, gi * pool_ch:(gi + 1) * pool_ch].set(pool_w[gi])
    convw = jnp.pad(conv_w, ((0, 8 - conv_w.shape[0]), (0, 0)))

    def out_spec(c):
        return pl.BlockSpec((1, MIX_ROWS, c), lambda bi, si: (bi, si, 0))

    return pl.pallas_call(
        _mixin_kernel,
        grid=(b, s // MIX_ROWS),
        in_specs=[pl.BlockSpec((1, MIX_ROWS, d), lambda bi, si: (bi, si, 0)),
                  _const_spec((1, d)), _const_spec(w.shape),
                  _const_spec((G_WIDTH, G_WIDTH)), _const_spec((1, G_WIDTH)),
                  _const_spec((8, G_WIDTH))],
        out_specs=[out_spec(G_WIDTH), out_spec(PB_COLS), out_spec(PC_COLS), out_spec(G_WIDTH)],
        out_shape=[jax.ShapeDtypeStruct((b, s, G_WIDTH), F32),
                   jax.ShapeDtypeStruct((b, s, PB_COLS), F32),
                   jax.ShapeDtypeStruct((b, s, PC_COLS), F32),
                   jax.ShapeDtypeStruct((b, s, G_WIDTH), F32)],
        scratch_shapes=[pltpu.VMEM((MIX_ROWS + POOL_HALO, G_WIDTH), F32),
                        pltpu.VMEM((MIX_ROWS + CONV_HALO, G_WIDTH), F32)],
        compiler_params=_params(2),
    )(x, g.reshape(1, d), w, poolw.astype(BF16), pool_scale.reshape(1, G_WIDTH), convw)


def _block_diag(x, mask01):
    return jnp.concatenate([x] * N_HEADS, axis=0) * mask01


def _rwkv_kernel(pb_ref, mu_ref, w0_ref, wup_ref, a0_ref, aup_ref, gup_ref, kk_ref, ka_ref,
                 rk_ref, lnw_ref, lnb_ref, o_ref,
                 ext, state, r_s, k_s, v_s, lw_s, a_s, b_s, y_s):
    si = pl.program_id(1)
    group, seq_rows = pb_ref.shape[0], pb_ref.shape[1]
    g_w = G_WIDTH
    chunk = RWKV_CHUNK

    @pl.when(si == 0)
    def _():
        ext[:, 0:8, :] = jnp.zeros((group, 8, PB_COLS), F32)
        state[...] = jnp.zeros_like(state)

    shifted = []
    for b in range(group):
        p_b = pb_ref[b]
        ext[b, 8:, :] = p_b
        prev = ext[b, 7:7 + seq_rows, :]
        shifted.append(p_b + mu_ref[...] * (prev - p_b))
        ext[b, 0:8, :] = ext[b, seq_rows:seq_rows + 8, :]
    ps = jnp.concatenate(shifted, axis=0)

    head_mask = _head_ones()
    head_ones = head_mask.astype(BF16)

    def head_sum(t):
        return _dot2_exact_rhs(t, head_ones)

    r = ps[:, 0:g_w]
    k = ps[:, g_w:2 * g_w]
    v = ps[:, 2 * g_w:3 * g_w]
    tail = ps[:, 3 * g_w:4 * g_w]
    w = w0_ref[...] + _dot(jnp.tanh(tail), wup_ref[...])
    lw = -jnp.exp(-0.5) * _sigmoid(w)
    a = _sigmoid(a0_ref[...] + _dot(tail, aup_ref[...]))
    gate = _dot(_sigmoid(tail), gup_ref[...])
    kk = k * kk_ref[...]
    kk = kk * lax.rsqrt(jnp.maximum(head_sum(kk * kk), 1e-24))
    k2 = k * (1.0 + (a - 1.0) * ka_ref[...])
    bonus = head_sum(r * k2 * rk_ref[...]) * v
    r_s[...] = r
    k_s[...] = k2
    v_s[...] = v
    lw_s[...] = lw
    a_s[...] = -kk
    b_s[...] = kk * a

    row_i = lax.broadcasted_iota(jnp.int32, (chunk, g_w), 0)
    col_j = lax.broadcasted_iota(jnp.int32, (chunk, g_w), 1) % chunk
    strict_lower = row_i > col_j
    lower = row_i >= col_j
    tri = (lax.broadcasted_iota(jnp.int32, (chunk, chunk), 0)
           >= lax.broadcasted_iota(jnp.int32, (chunk, chunk), 1)).astype(BF16)
    n_doublings = chunk.bit_length() - 1

    head_mask16 = head_ones

    def bd(x):
        return _block_diag(x.astype(BF16), head_mask16)

    def mm(a, b16):
        return jnp.dot(a.astype(BF16), b16, preferred_element_type=F32)

    ids = range(group)

    def group_local(c):
        sls = [pl.ds(pl.multiple_of(i * seq_rows + c * chunk, chunk), chunk) for i in ids]
        r_c, k_c, v_c, lw_c, a_c, b_c = ([t[sl, :] for sl in sls]
                                         for t in (r_s, k_s, v_s, lw_s, a_s, b_s))
        cw = [_dot2_exact_lhs(tri, lw_c[i]) for i in ids]
        cw_last = [cw[i][chunk - 1:chunk, :] for i in ids]
        a_t = [a_c[i] * jnp.exp(cw[i] - lw_c[i]) for i in ids]
        r_t = [r_c[i] * jnp.exp(cw[i]) for i in ids]
        e_inv = [jnp.exp(-cw[i]) for i in ids]
        bd_v = [bd(v_c[i]) for i in ids]
        pair = [lax.dot_general(
            jnp.concatenate([a_t[i], r_t[i]], axis=0).astype(BF16),
            jnp.concatenate([bd(b_c[i] * e_inv[i]), bd(k_c[i] * e_inv[i])], axis=0),
            (((1,), (1,)), ((), ())), preferred_element_type=F32) for i in ids]
        l_ak = [jnp.where(strict_lower, pair[i][0:chunk, g_w:2 * g_w], 0.0) for i in ids]
        m_rb = [jnp.where(lower, pair[i][chunk:2 * chunk, 0:g_w], 0.0) for i in ids]
        m_rk = [jnp.where(lower, pair[i][chunk:2 * chunk, g_w:2 * g_w], 0.0) for i in ids]

        x1 = a_t
        x2 = [mm(l_ak[i], bd_v[i]) for i in ids]
        l_pow = [jnp.where(strict_lower, pair[i][0:chunk, 0:g_w], 0.0) for i in ids]
        for it in range(n_doublings):
            last = it == n_doublings - 1
            res = [mm(l_pow[i], jnp.concatenate(
                [bd(x1[i]), bd(x2[i])] + ([] if last else [bd(l_pow[i])]), axis=1)) for i in ids]
            x1 = [x1[i] + res[i][:, 0:g_w] for i in ids]
            x2 = [x2[i] + res[i][:, g_w:2 * g_w] for i in ids]
            if not last:
                l_pow = [res[i][:, 2 * g_w:3 * g_w] for i in ids]

        q = [mm(m_rb[i], jnp.concatenate([bd(x1[i]), bd(x2[i])], axis=1)) for i in ids]
        q2 = [q[i][:, g_w:2 * g_w] + mm(m_rk[i], bd_v[i]) for i in ids]
        lhs = [jnp.concatenate([x1[i], r_t[i] + q[i][:, 0:g_w]], axis=0).astype(BF16) for i in ids]
        rhs = []
        for i in ids:
            e_fut = jnp.exp(cw_last[i] - cw[i])
            rhs.append(jnp.concatenate([b_c[i] * e_fut, k_c[i] * e_fut], axis=0).astype(BF16))
        decay = [jnp.exp(cw_last[i]) for i in ids]

        s0 = [state[i] for i in ids]
        us = [lax.dot_general(lhs[i], s0[i].astype(BF16), (((1,), (1,)), ((), ())),
                              preferred_element_type=F32) for i in ids]
        uv_t = []
        for i in ids:
            y_s[sls[i], :] = us[i][chunk:2 * chunk] + q2[i]
            u = us[i][0:chunk] + x2[i]
            uv_t.append(jnp.transpose(jnp.concatenate([u, v_c[i]], axis=0)).astype(BF16))
        upd = [jnp.dot(uv_t[i], rhs[i], preferred_element_type=F32) for i in ids]
        for i in ids:
            state[i] = s0[i] * decay[i] + jnp.where(head_mask, upd[i], 0.0)

    def chunk_step(c, carry):
        group_local(c)
        return carry

    lax.fori_loop(0, seq_rows // chunk, chunk_step, 0)

    y = y_s[...]
    inv_n = 1.0 / HEAD_DIM
    mean = head_sum(y) * inv_n
    dev = y - mean
    var = head_sum(dev * dev) * inv_n
    yn = dev * lax.rsqrt(var + RWKV_LN_EPS) * lnw_ref[...] + lnb_ref[...]
    out = (yn + bonus) * gate
    for b in range(group):
        o_ref[b] = out[b * seq_rows:(b + 1) * seq_rows, :]


def _rwkv_call(pb, mu, w0, w_up, a0, a_up, g_up, k_k, k_a, r_k, ln_w, ln_b):
    b, s, _ = pb.shape
    g_w = G_WIDTH
    row = lambda t: t.reshape(1, g_w)
    o_a = RWKV_W_RANK
    o_g = o_a + RWKV_A_RANK
    wup = jnp.zeros((g_w, g_w), F32).at[0:o_a].set(w_up).astype(BF16)
    aup = jnp.zeros((g_w, g_w), F32).at[o_a:o_g].set(a_up).astype(BF16)
    gup = jnp.zeros((g_w, g_w), F32).at[o_g:o_g + RWKV_G_RANK].set(g_up).astype(BF16)
    mu_p = _pad_cols(mu.reshape(1, -1), PB_COLS)
    vec = _const_spec((1, g_w))
    mat = _const_spec((g_w, g_w))
    group = min(RWKV_GROUP, b)
    seq_rows = MIX_ROWS // group
    seq = pltpu.VMEM((MIX_ROWS, g_w), F32)
    return pl.pallas_call(
        _rwkv_kernel,
        grid=(b // group, s // seq_rows),
        in_specs=[pl.BlockSpec((group, seq_rows, PB_COLS), lambda bi, si: (bi, si, 0)),
                  _const_spec((1, PB_COLS)), vec, mat, vec, mat, mat, vec, vec, vec, vec, vec],
        out_specs=pl.BlockSpec((group, seq_rows, g_w), lambda bi, si: (bi, si, 0)),
        out_shape=jax.ShapeDtypeStruct((b, s, g_w), F32),
        scratch_shapes=[pltpu.VMEM((group, seq_rows + 8, PB_COLS), F32),
                        pltpu.VMEM((group, g_w, g_w), F32),
                        seq, seq, seq, seq, seq, seq, seq],
        compiler_params=_params(2),
    )(pb, mu_p, row(w0), wup, row(a0), aup, gup, row(k_k), row(k_a), row(r_k), row(ln_w), row(ln_b))


def _gelu_tanh(x):
    return 0.5 * x * (1.0 + jnp.tanh(0.7978845608028654 * (x + 0.044715 * x * x * x)))


def _pos_features(pos, width):
    lane = lax.broadcasted_iota(jnp.int32, (pos.shape[0], width), 1)
    feat = jnp.where(lane == 0, pos // ALIBI_SPLIT,
                     jnp.where(lane == 1, pos % ALIBI_SPLIT, jnp.where(lane == 2, 1, 0)))
    return feat.astype(F32).astype(BF16)


def _nsa_compress_kernel(kc_ref, vc_ref, pos_ref, kw1_ref, kw2_ref, vw1_ref, vw2t_ref, kn_ref,
                         kcmp_ref, vcmpt_ref):
    half = kw1_ref.shape[0] // 2
    n_rows = kc_ref.shape[1]

    def hidden(x, w1_ref):
        first = _dot(x + pos_ref[0:1, :], w1_ref[0:half, :])
        second = _dot(x + pos_ref[1:2, :], w1_ref[half:2 * half, :])
        return _gelu_tanh(first + pltpu.roll(second, n_rows - 1, 0))

    k_cmp = _dot(hidden(kc_ref[0], kw1_ref), kw2_ref[...])
    kcmp_ref[0, :, 0:HEAD_DIM] = _rms_rows(k_cmp, kn_ref[...]).astype(BF16)
    last_token = (lax.broadcasted_iota(jnp.int32, (n_rows, 1), 0) * NSA_CMP_STRIDE
                  + (NSA_CMP_LEN - 1))
    kcmp_ref[0, :, HEAD_DIM:2 * HEAD_DIM] = _pos_features(last_token, HEAD_DIM)
    vcmpt_ref[0] = _dot_nt(vw2t_ref[...], hidden(vc_ref[0], vw1_ref)).astype(BF16)


def _nsa_compress_call(kcx, vcx, pos, kw1, kw2, vw1, vw2, k_norm0):
    b, n_chunks, width = kcx.shape
    hid = kw1.shape[1]
    pos2 = jnp.pad(pos.reshape(2, width), ((0, 6), (0, 0)))
    blk = pl.BlockSpec((1, n_chunks, width), lambda bi: (bi, 0, 0))
    return pl.pallas_call(
        _nsa_compress_kernel,
        grid=(b,),
        in_specs=[blk, blk, _const_spec((8, width)),
                  _const_spec((2 * width, hid)), _const_spec((hid, HEAD_DIM)),
                  _const_spec((2 * width, hid)), _const_spec((HEAD_DIM, hid)),
                  _const_spec((1, HEAD_DIM))],
        out_specs=[pl.BlockSpec((1, n_chunks, 2 * HEAD_DIM), lambda bi: (bi, 0, 0)),
                   pl.BlockSpec((1, HEAD_DIM, n_chunks), lambda bi: (bi, 0, 0))],
        out_shape=[jax.ShapeDtypeStruct((b, n_chunks, 2 * HEAD_DIM), BF16),
                   jax.ShapeDtypeStruct((b, HEAD_DIM, n_chunks), BF16)],
        compiler_params=_params(1),
    )(kcx, vcx, pos2, kw1.astype(BF16), kw2.astype(BF16), vw1.astype(BF16),
      vw2.T.astype(BF16), k_norm0.reshape(1, HEAD_DIM))


def _nsa_kv_kernel(k_ref, v_ref, kn_ref, kk_ref, vt_ref):
    x = k_ref[0]
    first = lax.broadcasted_iota(jnp.int32, x.shape, 1) < HEAD_DIM
    sq = x * x
    ms_first = jnp.sum(jnp.where(first, sq, 0.0), axis=-1, keepdims=True) * (1.0 / HEAD_DIM)
    ms_second = jnp.sum(jnp.where(first, 0.0, sq), axis=-1, keepdims=True) * (1.0 / HEAD_DIM)
    inv = jnp.where(first, lax.rsqrt(ms_first + RMS_EPS), lax.rsqrt(ms_second + RMS_EPS))
    kk_ref[0, :, 0:LANES] = (x * inv * kn_ref[...]).astype(BF16)
    rows = x.shape[0]
    pos = pl.program_id(1) * rows + lax.broadcasted_iota(jnp.int32, (rows, 1), 0)
    kk_ref[0, :, LANES:2 * LANES] = _pos_features(pos, LANES)
    v = v_ref[0]
    for j in range(vt_ref.shape[1]):
        vt_ref[0, j] = jnp.transpose(v[j * KEY_STEP:(j + 1) * KEY_STEP, :]).astype(BF16)


def _nsa_kv_call(pc, k_norm_sel, k_norm_win):
    b, s, _ = pc.shape
    kn = jnp.concatenate([k_norm_sel, k_norm_win]).reshape(1, 2 * HEAD_DIM)
    per_step = MIX_ROWS // KEY_STEP
    return pl.pallas_call(
        _nsa_kv_kernel,
        grid=(b, s // MIX_ROWS),
        in_specs=[pl.BlockSpec((1, MIX_ROWS, LANES), lambda bi, si: (bi, si, 3)),
                  pl.BlockSpec((1, MIX_ROWS, LANES), lambda bi, si: (bi, si, 4)),
                  _const_spec((1, LANES))],
        out_specs=[pl.BlockSpec((1, MIX_ROWS, 2 * LANES), lambda bi, si: (bi, si, 0)),
                   pl.BlockSpec((1, per_step, LANES, KEY_STEP), lambda bi, si: (bi, si, 0, 0))],
        out_shape=[jax.ShapeDtypeStruct((b, s, 2 * LANES), BF16),
                   jax.ShapeDtypeStruct((b, s // KEY_STEP, LANES, KEY_STEP), BF16)],
        compiler_params=_params(2),
    )(pc, pc, kn)


def _col_reduce(x, pair_op, reduce_fn, slab=32):
    parts = [x[i:i + slab] for i in range(0, x.shape[0], slab)]
    while len(parts) > 1:
        parts = [pair_op(parts[i], parts[i + 1]) for i in range(0, len(parts), 2)]
    return reduce_fn(parts[0], axis=0, keepdims=True)


def _nsa_attn_kernel(q_ref, gate_ref, qn_ref, kc_ref, vct_ref, kk_ref, vt_ref, o_ref, s_ref):
    qi = pl.program_id(1)
    tq = ATT_ROWS
    hw = N_HEADS * tq
    q0 = qi * tq
    n_cmp_rows = kc_ref.shape[1]
    n_sel = kk_ref.shape[1] // NSA_SEL_BLOCK

    head_ones = _head_ones().astype(BF16)
    q = q_ref[0]
    ms = _dot2_exact_rhs(q * q, head_ones) * (1.0 / HEAD_DIM)
    qn = q * lax.rsqrt(ms + RMS_EPS) * qn_ref[...] * (HEAD_DIM ** -0.5)
    qt = jnp.transpose(qn)
    q4t = jnp.concatenate([qt[h * HEAD_DIM:(h + 1) * HEAD_DIM, :] for h in range(N_HEADS)],
                          axis=1).astype(BF16)
    lane = lax.broadcasted_iota(jnp.int32, (1, hw), 1)
    slope = jnp.exp2(-2.0 * (lane // tq + 1).astype(F32))
    t_lane = lane % tq

    f_row = lax.broadcasted_iota(jnp.int32, (HEAD_DIM, hw), 0)
    q_feat = jnp.where(f_row == 0, ALIBI_SPLIT * slope,
                       jnp.where(f_row == 1, slope,
                                 jnp.where(f_row == 2, -slope * q0.astype(F32), 0.0))).astype(BF16)
    zero_half = jnp.zeros_like(q4t)
    q_cmp = jnp.concatenate([q4t, q_feat], axis=0)
    q_sel = jnp.concatenate([q4t, zero_half, q_feat, zero_half], axis=0)
    q_win = jnp.concatenate([zero_half, q4t, q_feat, zero_half], axis=0)

    n_idx = lax.broadcasted_iota(jnp.int32, (n_cmp_rows, hw), 0)
    valid_c = (q0 + t_lane) >= (n_idx * NSA_CMP_STRIDE + (NSA_CMP_LEN - 1))
    s_c = (jnp.dot(kc_ref[0], q_cmp, preferred_element_type=F32)
           + jnp.where(valid_c, 0.0, NSA_NEG))
    m_c = jnp.maximum(jnp.max(s_c, axis=0, keepdims=True), SCORE_FLOOR)
    p_c = jnp.exp(s_c - m_c)
    l_c = jnp.sum(p_c, axis=0, keepdims=True)
    p_c = p_c * (1.0 / jnp.maximum(l_c, 1e-30))
    o_cmp = jnp.dot(vct_ref[0], p_c.astype(BF16), preferred_element_type=F32)

    p_heads = p_c[:, 0:tq]
    for h in range(1, N_HEADS):
        p_heads = p_heads + p_c[:, h * tq:(h + 1) * tq]
    per_sel = NSA_SEL_BLOCK // NSA_CMP_STRIDE
    j_ov = lax.broadcasted_iota(jnp.int32, (n_sel, n_cmp_rows), 0)
    n_ov = lax.broadcasted_iota(jnp.int32, (n_sel, n_cmp_rows), 1)
    overlap_t = ((n_ov >= per_sel * j_ov - (NSA_CMP_LEN // NSA_CMP_STRIDE - 1))
                 & (n_ov <= per_sel * j_ov + per_sel - 1)).astype(BF16)
    p_hi = p_heads.astype(BF16)
    p_rest = p_heads - p_hi.astype(F32)
    p_mid = p_rest.astype(BF16)
    p_lo = (p_rest - p_mid.astype(F32)).astype(BF16)
    imp = (jnp.dot(overlap_t, p_hi, preferred_element_type=F32)
           + jnp.dot(overlap_t, p_mid, preferred_element_type=F32)
           + jnp.dot(overlap_t, p_lo, preferred_element_type=F32))
    j_idx = lax.broadcasted_iota(jnp.int32, (n_sel, tq), 0)
    cur = (q0 + lax.broadcasted_iota(jnp.int32, (1, tq), 1)) // NSA_SEL_BLOCK
    forced = (j_idx == 0) | (j_idx == cur) | (j_idx == cur - 1)
    imp = jnp.where(j_idx <= cur, imp + jnp.where(forced, NSA_FORCE_BONUS, 0.0), -1.0)
    rank = jnp.zeros((n_sel, tq), F32)
    for jp in range(n_sel):
        other = imp[jp:jp + 1, :]
        ahead = (other > imp) | ((other == imp) & (j_idx > jp))
        rank = rank + jnp.where(ahead, 1.0, 0.0)
    chosen = jnp.where(rank < float(min(NSA_TOP_N, n_sel)), 0.0, NSA_NEG)

    key_i = lax.broadcasted_iota(jnp.int32, (tq, tq), 0)
    query_i = lax.broadcasted_iota(jnp.int32, (tq, tq), 1)
    causal = jnp.where(query_i >= key_i, 0.0, NSA_NEG)
    window_start = jnp.where(key_i > query_i, 0.0, NSA_NEG)
    tiles_per_step = KEY_STEP // tq
    blocks_per_tile = tq // NSA_SEL_BLOCK
    win_tiles = NSA_WINDOW // tq
    last_step = qi // tiles_per_step

    def branch(n_steps, step_of, q_pad, v_rows, bias_fn):
        def prefetch(i, slot):
            p = step_of(i)
            k0 = pl.multiple_of(p * KEY_STEP, KEY_STEP)
            raw = jnp.dot(kk_ref[0, pl.ds(k0, KEY_STEP), :], q_pad, preferred_element_type=F32)
            s = raw + jnp.concatenate([bias_fn(p)] * N_HEADS, axis=1)
            s_ref[slot] = s
            return _col_reduce(s, jnp.maximum, jnp.max)

        def absorb(i, slot, s_max, m, l, acc):
            m_new = jnp.maximum(m, s_max)
            alpha = jnp.exp(m - m_new)
            prob = jnp.exp(s_ref[slot] - m_new)
            l = alpha * l + _col_reduce(prob, jnp.add, jnp.sum)
            v_t = vt_ref[0, step_of(i)][v_rows[0]:v_rows[1], :]
            pv = jnp.dot(v_t, prob.astype(BF16), preferred_element_type=F32)
            return m_new, l, alpha * acc + pv

        def pair(i2, carry):
            max0, m, l, acc = carry
            i = 2 * i2
            max1 = prefetch(i + 1, 1)
            m, l, acc = absorb(i, 0, max0, m, l, acc)
            max0 = prefetch(i + 2, 0)
            m, l, acc = absorb(i + 1, 1, max1, m, l, acc)
            return max0, m, l, acc

        init = (prefetch(0, 0), jnp.full((1, hw), SCORE_FLOOR, F32), jnp.zeros((1, hw), F32),
                jnp.zeros((HEAD_DIM, hw), F32))
        n_pairs = (n_steps - 1) // 2
        max0, m, l, acc = lax.fori_loop(0, n_pairs, pair, init)
        i = 2 * n_pairs

        def two_left(m, l, acc):
            max1 = prefetch(i + 1, 1)
            m, l, acc = absorb(i, 0, max0, m, l, acc)
            return absorb(i + 1, 1, max1, m, l, acc)

        def one_left(m, l, acc):
            return absorb(i, 0, max0, m, l, acc)

        _, l, acc = lax.cond(n_steps - i == 2, two_left, one_left, m, l, acc)
        return acc * (1.0 / l)

    def sel_tile_bias(kt):
        rows = [jnp.max(jnp.where(j_idx == blocks_per_tile * kt + i, chosen, NSA_NEG),
                        axis=0, keepdims=True) for i in range(blocks_per_tile)]
        bias = rows[-1]
        for i in range(blocks_per_tile - 2, -1, -1):
            bias = jnp.where(key_i < (i + 1) * NSA_SEL_BLOCK, rows[i], bias)
        return bias

    def sel_step_bias(p):
        tiles = []
        for i in range(tiles_per_step):
            kt = p * tiles_per_step + i
            tiles.append(sel_tile_bias(kt) + jnp.where(kt == qi, causal, 0.0)
                         + jnp.where(kt > qi, NSA_NEG, 0.0))
        return jnp.concatenate(tiles, axis=0)

    def win_step_bias(p):
        tiles = []
        for i in range(tiles_per_step):
            behind = qi - (p * tiles_per_step + i)
            bias = jnp.where(behind == 0, causal, jnp.where(behind == win_tiles, window_start, 0.0))
            tiles.append(bias + jnp.where((behind >= 0) & (behind <= win_tiles), 0.0, NSA_NEG))
        return jnp.concatenate(tiles, axis=0)

    first_win_step = jnp.maximum(qi - win_tiles, 0) // tiles_per_step
    o_win = branch(last_step - first_win_step + 1, lambda i: first_win_step + i,
                   q_win, (HEAD_DIM, 2 * HEAD_DIM), win_step_bias)

    blocks_per_step = KEY_STEP // NSA_SEL_BLOCK
    block_id = lax.broadcasted_iota(jnp.int32, (n_sel, 1), 0)
    block_used = jnp.max(chosen, axis=1, keepdims=True) > SCORE_FLOOR
    earliest = jnp.min(jnp.where(block_used & (block_id >= blocks_per_step), block_id, n_sel),
                       axis=0, keepdims=True)[0, 0]
    first_sel_step = jnp.clip(earliest // blocks_per_step, 1, jnp.maximum(last_step, 1))
    n_sel_steps = jnp.where(last_step >= 1, last_step - first_sel_step + 2, 1)
    o_sel = branch(n_sel_steps, lambda i: jnp.where(i == 0, 0, first_sel_step + i - 1),
                   q_sel, (0, HEAD_DIM), sel_step_bias)

    g_t = jnp.transpose(_sigmoid(gate_ref[0]))

    def gate_row(c):
        return jnp.concatenate([g_t[3 * h + c:3 * h + c + 1, :] for h in range(N_HEADS)], axis=1)

    o = gate_row(0) * o_cmp + gate_row(1) * o_sel + gate_row(2) * o_win
    o_hd = jnp.concatenate([o[:, h * tq:(h + 1) * tq] for h in range(N_HEADS)], axis=0)
    o_ref[0] = jnp.transpose(o_hd)


def _nsa_attn_call(pc, q_norm, kcmp, vcmpt, kk, vt):
    b, s, _ = pc.shape
    n_cmp_rows = kcmp.shape[1]
    qn = jnp.tile(q_norm, N_HEADS).reshape(1, G_WIDTH)
    return pl.pallas_call(
        _nsa_attn_kernel,
        grid=(b, s // ATT_ROWS),
        in_specs=[pl.BlockSpec((1, ATT_ROWS, G_WIDTH), lambda bi, qi: (bi, qi, 0)),
                  pl.BlockSpec((1, ATT_ROWS, LANES), lambda bi, qi: (bi, qi, 5)),
                  _const_spec((1, G_WIDTH)),
                  pl.BlockSpec((1, n_cmp_rows, 2 * HEAD_DIM), lambda bi, qi: (bi, 0, 0)),
                  pl.BlockSpec((1, HEAD_DIM, n_cmp_rows), lambda bi, qi: (bi, 0, 0)),
                  pl.BlockSpec((1, s, 2 * LANES), lambda bi, qi: (bi, 0, 0)),
                  pl.BlockSpec((1, s // KEY_STEP, LANES, KEY_STEP), lambda bi, qi: (bi, 0, 0, 0))],
        out_specs=pl.BlockSpec((1, ATT_ROWS, G_WIDTH), lambda bi, qi: (bi, qi, 0)),
        out_shape=jax.ShapeDtypeStruct((b, s, G_WIDTH), F32),
        scratch_shapes=[pltpu.VMEM((2, KEY_STEP, N_HEADS * ATT_ROWS), F32)],
        compiler_params=_params(2),
    )(pc, pc, qn, kcmp, vcmpt, kk, vt)


def _nsa_call(pc, q_norm, k_norm, cmp_pos, kw1, kw2, vw1, vw2):
    b, s, _ = pc.shape
    n_chunks = s // NSA_CMP_STRIDE
    width = NSA_CMP_STRIDE * HEAD_DIM
    kcx = pc[..., G_WIDTH:G_WIDTH + HEAD_DIM].reshape(b, n_chunks, width)
    vcx = pc[..., G_WIDTH + HEAD_DIM:G_WIDTH + 2 * HEAD_DIM].reshape(b, n_chunks, width)
    kcmp, vcmpt = _nsa_compress_call(kcx, vcx, cmp_pos, kw1, kw2, vw1, vw2, k_norm[0])
    kk, vt = _nsa_kv_call(pc, k_norm[1], k_norm[2])
    return _nsa_attn_call(pc, q_norm, kcmp, vcmpt, kk, vt)


def kernel(x, ffn1_norm, ffn1_w_gate, ffn1_w_up, ffn1_w_down, mix_norm, w_in, pool_w, pool_scale, rwkv_mu, rwkv_w0, rwkv_w_up, rwkv_a0, rwkv_a_up, rwkv_g_up, rwkv_k_k, rwkv_k_a, rwkv_r_k, rwkv_ln_w, rwkv_ln_b, nsa_q_norm, nsa_k_norm, nsa_cmp_pos, nsa_cmp_k_w1, nsa_cmp_k_w2, nsa_cmp_v_w1, nsa_cmp_v_w2, conv_w, w_out, ffn2_norm, ffn2_w_gate, ffn2_w_up, ffn2_w_down):
    b, s, d = x.shape
    n_tok = b * s
    x2d = x.reshape(n_tok, d)
    for l in range(ffn1_norm.shape[0]):
        x2d = _ffn_call(x2d, ffn1_norm[l], ffn1_w_gate[l], ffn1_w_up[l], ffn1_w_down[l])
        ya, pb, pc, yd = _mixin_call(x2d.reshape(b, s, d), mix_norm[l], w_in[l], pool_w[l],
                                     pool_scale[l], conv_w[l])
        yb = _rwkv_call(pb, rwkv_mu[l], rwkv_w0[l], rwkv_w_up[l], rwkv_a0[l], rwkv_a_up[l],
                        rwkv_g_up[l], rwkv_k_k[l], rwkv_k_a[l], rwkv_r_k[l], rwkv_ln_w[l],
                        rwkv_ln_b[l])
        yc = _nsa_call(pc, nsa_q_norm[l], nsa_k_norm[l], nsa_cmp_pos[l], nsa_cmp_k_w1[l],
                       nsa_cmp_k_w2[l], nsa_cmp_v_w1[l], nsa_cmp_v_w2[l])
        mix = tuple(t.reshape(n_tok, G_WIDTH) for t in (ya, yb, yc, yd))
        x2d = _ffn_call(x2d, ffn2_norm[l], ffn2_w_gate[l], ffn2_w_up[l], ffn2_w_down[l],
                        mix=mix, w_out=w_out[l])
    return x2d.reshape(b, s, d)
```

```python
import functools

import jax
import jax.numpy as jnp
from jax import lax
from jax.experimental import pallas as pl
from jax.experimental.pallas import tpu as pltpu

F32 = jnp.float32
BF16 = jnp.bfloat16

N_MIXERS = 4
HEAD_DIM = 64
N_HEADS = 4
G_WIDTH = N_HEADS * HEAD_DIM
RMS_EPS = 1e-6
POOL_WINDOWS = (2, 4, 8, 16)
POOL_HALO = 16
CONV_HALO = 8
RWKV_W_RANK, RWKV_A_RANK, RWKV_G_RANK = 64, 32, 64
RWKV_LN_EPS = 64e-5
RWKV_CHUNK = 64
RWKV_GROUP = 4
NSA_CMP_LEN = 32
NSA_CMP_STRIDE = 16
NSA_SEL_BLOCK = 64
NSA_TOP_N = 16
NSA_WINDOW = 512
NSA_FORCE_BONUS = 1e4
NSA_NEG = -1e9
SCORE_FLOOR = 0.5 * NSA_NEG
ALIBI_SPLIT = 64

PB_COLS = 1024
PC_COLS = 768
PD_COLS = 768
P_PAD = G_WIDTH + PB_COLS + PC_COLS + PD_COLS

V7X_VMEM_BYTES = 64 * 1024 * 1024
VMEM_LIMIT = V7X_VMEM_BYTES - 8 * 1024 * 1024
LANES = 128

FFN_ROWS = 512
FFN_COLS = 256
MIX_ROWS = 512
ATT_ROWS = 128
KEY_STEP = 2 * ATT_ROWS


def _params(n_axes):
    return pltpu.CompilerParams(dimension_semantics=("arbitrary",) * n_axes,
                                vmem_limit_bytes=VMEM_LIMIT)


def _const_spec(shape):
    nd = len(shape)
    return pl.BlockSpec(shape, lambda *_: (0,) * nd, pipeline_mode=pl.Buffered(1))


def _dot(a, b):
    return jnp.dot(a.astype(BF16), b.astype(BF16), preferred_element_type=F32)


def _dot_nt(a, b):
    return lax.dot_general(a.astype(BF16), b.astype(BF16), (((1,), (1,)), ((), ())),
                           preferred_element_type=F32)


def _split(x):
    hi = x.astype(BF16)
    lo = (x - hi.astype(F32)).astype(BF16)
    return hi, lo


def _dot2_exact_rhs(a, b_bf16):
    ah, al = _split(a)
    return (jnp.dot(ah, b_bf16, preferred_element_type=F32)
            + jnp.dot(al, b_bf16, preferred_element_type=F32))


def _dot2_exact_lhs(a_bf16, b):
    bh, bl = _split(b)
    return (jnp.dot(a_bf16, bh, preferred_element_type=F32)
            + jnp.dot(a_bf16, bl, preferred_element_type=F32))


def _rms_rows(x, g):
    return x * lax.rsqrt(jnp.mean(x * x, axis=-1, keepdims=True) + RMS_EPS) * g


def _sigmoid(x):
    return 1.0 / (1.0 + jnp.exp(-x))


def _head_ones():
    r = lax.broadcasted_iota(jnp.int32, (G_WIDTH, G_WIDTH), 0) // HEAD_DIM
    c = lax.broadcasted_iota(jnp.int32, (G_WIDTH, G_WIDTH), 1) // HEAD_DIM
    return r == c


def _ffn_body(x_in, g_ref, wg_ref, wu_ref, wd_ref, o_ref, acc_ref):
    h = _rms_rows(x_in, g_ref[...]).astype(BF16)
    for c in range(wg_ref.shape[1] // FFN_COLS):
        cols = slice(c * FFN_COLS, (c + 1) * FFN_COLS)
        gate = jnp.dot(h, wg_ref[:, cols], preferred_element_type=F32)
        up = jnp.dot(h, wu_ref[:, cols], preferred_element_type=F32)
        act = (gate * _sigmoid(gate) * up).astype(BF16)
        down = jnp.dot(act, wd_ref[cols, :], preferred_element_type=F32)
        if c == 0:
            acc_ref[...] = down
        else:
            acc_ref[...] += down
    o_ref[...] = x_in + 0.5 * acc_ref[...]


def _ffn_kernel(x_ref, g_ref, wg_ref, wu_ref, wd_ref, o_ref, acc_ref):
    _ffn_body(x_ref[...], g_ref, wg_ref, wu_ref, wd_ref, o_ref, acc_ref)


def _out_ffn_kernel(x_ref, ya_ref, yb_ref, yc_ref, yd_ref, wo_ref,
                    g_ref, wg_ref, wu_ref, wd_ref, o_ref, acc_ref):
    x1 = x_ref[...]
    for i, y_ref in enumerate((ya_ref, yb_ref, yc_ref, yd_ref)):
        x1 = x1 + _dot(y_ref[...], wo_ref[i])
    _ffn_body(x1, g_ref, wg_ref, wu_ref, wd_ref, o_ref, acc_ref)


def _ffn_call(x2d, g, w_gate, w_up, w_down, mix=None, w_out=None):
    n_tok, d = x2d.shape
    wg, wu, wd = w_gate.astype(BF16), w_up.astype(BF16), w_down.astype(BF16)
    row_spec = pl.BlockSpec((FFN_ROWS, d), lambda i: (i, 0))
    w_specs = [_const_spec((1, d)), _const_spec(wg.shape), _const_spec(wu.shape),
               _const_spec(wd.shape)]
    w_args = [g.reshape(1, d), wg, wu, wd]
    if mix is None:
        kern, in_specs, args = _ffn_kernel, [row_spec] + w_specs, [x2d] + w_args
    else:
        y_spec = pl.BlockSpec((FFN_ROWS, G_WIDTH), lambda i: (i, 0))
        wo = w_out.astype(BF16).reshape(N_MIXERS, G_WIDTH, d)
        kern = _out_ffn_kernel
        in_specs = [row_spec] + [y_spec] * N_MIXERS + [_const_spec(wo.shape)] + w_specs
        args = [x2d] + list(mix) + [wo] + w_args
    return pl.pallas_call(
        kern,
        grid=(n_tok // FFN_ROWS,),
        in_specs=in_specs,
        out_specs=row_spec,
        out_shape=jax.ShapeDtypeStruct((n_tok, d), F32),
        scratch_shapes=[pltpu.VMEM((FFN_ROWS, d), F32)],
        compiler_params=_params(1),
    )(*args)


def _mixin_kernel(x_ref, g_ref, w_ref, poolw_ref, pools_ref, convw_ref, kn_ref,
                  ya_ref, pb_ref, q_ref, gate_ref, kcvc_ref, kk_ref, vt_ref, yd_ref,
                  pa_ext, z_ext, kcvc_s):
    si = pl.program_id(1)
    rows = x_ref.shape[1]

    @pl.when(si == 0)
    def _():
        pa_ext[0:POOL_HALO, :] = jnp.zeros((POOL_HALO, G_WIDTH), F32)
        z_ext[0:CONV_HALO, :] = jnp.zeros((CONV_HALO, G_WIDTH), F32)

    c0, c1, c2 = G_WIDTH, G_WIDTH + PB_COLS, G_WIDTH + PB_COLS + PC_COLS
    half = rows // 2
    u_parts, pd_parts, pc_parts = [], [], []
    for r0 in (0, half):
        h = _rms_rows(x_ref[0, r0:r0 + half, :], g_ref[...]).astype(BF16)
        u_parts.append(jnp.dot(h, w_ref[:, 0:c0], preferred_element_type=F32))
        pd_parts.append(jnp.dot(h, w_ref[:, c2:c2 + PD_COLS], preferred_element_type=F32))
        pb_ref[0, r0:r0 + half, :] = jnp.dot(h, w_ref[:, c0:c1], preferred_element_type=F32)
        pc_parts.append(jnp.dot(h, w_ref[:, c1:c2], preferred_element_type=F32))
    u = jnp.concatenate(u_parts, axis=0)
    pd = jnp.concatenate(pd_parts, axis=0)
    pc = jnp.concatenate(pc_parts, axis=0)

    q_ref[0] = pc[:, 0:G_WIDTH]
    gate_ref[0] = pc[:, 5 * LANES:6 * LANES]
    kcvc_s[...] = pc[:, 2 * LANES:3 * LANES]
    for t in range(NSA_CMP_STRIDE):
        kcvc_ref[0, :, t * LANES:(t + 1) * LANES] = kcvc_s[pl.ds(t, rows // NSA_CMP_STRIDE,
                                                                stride=NSA_CMP_STRIDE), :]
    kx = pc[:, 3 * LANES:4 * LANES]
    first = lax.broadcasted_iota(jnp.int32, kx.shape, 1) < HEAD_DIM
    sq = kx * kx
    ms_first = jnp.sum(jnp.where(first, sq, 0.0), axis=-1, keepdims=True) * (1.0 / HEAD_DIM)
    ms_second = jnp.sum(jnp.where(first, 0.0, sq), axis=-1, keepdims=True) * (1.0 / HEAD_DIM)
    inv = jnp.where(first, lax.rsqrt(ms_first + RMS_EPS), lax.rsqrt(ms_second + RMS_EPS))
    kk_ref[0, :, 0:LANES] = (kx * inv * kn_ref[...]).astype(BF16)
    key_pos = si * rows + lax.broadcasted_iota(jnp.int32, (rows, 1), 0)
    kk_ref[0, :, LANES:2 * LANES] = _pos_features(key_pos, LANES)
    vx = pc[:, 4 * LANES:5 * LANES]
    for j in range(vt_ref.shape[1]):
        vt_ref[0, j] = jnp.transpose(vx[j * KEY_STEP:(j + 1) * KEY_STEP, :]).astype(BF16)

    pa_ext[POOL_HALO:, :] = u
    lane_group = lax.broadcasted_iota(jnp.int32, (rows, G_WIDTH), 1) // (G_WIDTH // len(POOL_WINDOWS))
    pos = si * rows + lax.broadcasted_iota(jnp.int32, (rows, G_WIDTH), 0)
    total = u
    for k in range(1, max(POOL_WINDOWS)):
        first_group = sum(1 for w in POOL_WINDOWS if w <= k)
        shifted = pa_ext[POOL_HALO - k:POOL_HALO - k + rows, :]
        total = total + jnp.where(lane_group >= first_group, shifted, 0.0)
    window = jnp.left_shift(2, lane_group)
    cnt = jnp.minimum(pos + 1, window).astype(F32)
    pooled = total / cnt
    ya = _dot(pooled - u, poolw_ref[...]) * pools_ref[...]
    ya_ref[0] = ya
    pa_ext[0:POOL_HALO, :] = pa_ext[rows:rows + POOL_HALO, :]

    cu = pd[:, 0:G_WIDTH]
    cb = pd[:, G_WIDTH:2 * G_WIDTH]
    cc = pd[:, 2 * G_WIDTH:3 * G_WIDTH]
    z = cc * cu
    z_ext[CONV_HALO:, :] = z
    y = (convw_ref[0:1, :] * z_ext[CONV_HALO - 2:CONV_HALO - 2 + rows, :]
         + convw_ref[1:2, :] * z_ext[CONV_HALO - 1:CONV_HALO - 1 + rows, :]
         + convw_ref[2:3, :] * z)
    yd_ref[0] = cb * y
    z_ext[0:CONV_HALO, :] = z_ext[rows:rows + CONV_HALO, :]


def _pad_cols(w, width):
    return jnp.pad(w, ((0, 0), (0, width - w.shape[1])))


def _w_in_padded(w_in):
    g = G_WIDTH
    o = 0
    w_a = w_in[:, o:o + g]; o += g
    rwkv_cols = 3 * g + RWKV_W_RANK + RWKV_A_RANK + RWKV_G_RANK
    w_b = _pad_cols(w_in[:, o:o + rwkv_cols], PB_COLS); o += rwkv_cols
    q = w_in[:, o:o + g]; o += g
    kc, vc, ksl, vsl, kwn, vwn = [w_in[:, o + i * HEAD_DIM:o + (i + 1) * HEAD_DIM] for i in range(6)]
    o += 6 * HEAD_DIM
    gates = w_in[:, o:o + 3 * N_HEADS]; o += 3 * N_HEADS
    w_c = jnp.concatenate([q, kc, vc, ksl, kwn, vsl, vwn, _pad_cols(gates, LANES)], axis=1)
    w_d = w_in[:, o:o + PD_COLS]
    return jnp.concatenate([w_a, w_b, w_c, w_d], axis=1).astype(BF16)


def _mixin_call(x, g, w_in, pool_w, pool_scale, conv_w, k_norm_sel, k_norm_win):
    b, s, d = x.shape
    w = _w_in_padded(w_in)
    kn = jnp.concatenate([k_norm_sel, k_norm_win]).reshape(1, LANES)
    cmp_rows = MIX_ROWS // NSA_CMP_STRIDE
    key_steps = MIX_ROWS // KEY_STEP
    n_groups = len(POOL_WINDOWS)
    pool_ch = G_WIDTH // n_groups
    poolw = jnp.zeros((G_WIDTH, G_WIDTH), F32)
    for gi in range(n_groups):
        poolw = poolw.at[gi * pool_ch:(gi + 1) * pool_ch, gi * pool_ch:(gi + 1) * pool_ch].set(pool_w[gi])
    convw = jnp.pad(conv_w, ((0, 8 - conv_w.shape[0]), (0, 0)))

    def out_spec(c):
        return pl.BlockSpec((1, MIX_ROWS, c), lambda bi, si: (bi, si, 0))

    return pl.pallas_call(
        _mixin_kernel,
        grid=(b, s // MIX_ROWS),
        in_specs=[pl.BlockSpec((1, MIX_ROWS, d), lambda bi, si: (bi, si, 0)),
                  _const_spec((1, d)), _const_spec(w.shape),
                  _const_spec((G_WIDTH, G_WIDTH)), _const_spec((1, G_WIDTH)),
                  _const_spec((8, G_WIDTH)), _const_spec((1, LANES))],
        out_specs=[out_spec(G_WIDTH), out_spec(PB_COLS), out_spec(G_WIDTH), out_spec(LANES),
                   pl.BlockSpec((1, cmp_rows, NSA_CMP_STRIDE * LANES), lambda bi, si: (bi, si, 0)),
                   out_spec(2 * LANES),
                   pl.BlockSpec((1, key_steps, LANES, KEY_STEP), lambda bi, si: (bi, si, 0, 0)),
                   out_spec(G_WIDTH)],
        out_shape=[jax.ShapeDtypeStruct((b, s, G_WIDTH), F32),
                   jax.ShapeDtypeStruct((b, s, PB_COLS), F32),
                   jax.ShapeDtypeStruct((b, s, G_WIDTH), F32),
                   jax.ShapeDtypeStruct((b, s, LANES), F32),
                   jax.ShapeDtypeStruct((b, s // NSA_CMP_STRIDE, NSA_CMP_STRIDE * LANES), F32),
                   jax.ShapeDtypeStruct((b, s, 2 * LANES), BF16),
                   jax.ShapeDtypeStruct((b, s // KEY_STEP, LANES, KEY_STEP), BF16),
                   jax.ShapeDtypeStruct((b, s, G_WIDTH), F32)],
        scratch_shapes=[pltpu.VMEM((MIX_ROWS + POOL_HALO, G_WIDTH), F32),
                        pltpu.VMEM((MIX_ROWS + CONV_HALO, G_WIDTH), F32),
                        pltpu.VMEM((MIX_ROWS, LANES), F32)],
        compiler_params=_params(2),
    )(x, g.reshape(1, d), w, poolw.astype(BF16), pool_scale.reshape(1, G_WIDTH), convw, kn)


def _block_diag(x, mask01):
    return jnp.concatenate([x] * N_HEADS, axis=0) * mask01


def _rwkv_kernel(pb_ref, mu_ref, w0_ref, wup_ref, a0_ref, aup_ref, gup_ref, kk_ref, ka_ref,
                 rk_ref, lnw_ref, lnb_ref, o_ref,
                 ext, state, r_s, k_s, v_s, lw_s, a_s, b_s, y_s):
    si = pl.program_id(1)
    group, seq_rows = pb_ref.shape[0], pb_ref.shape[1]
    g_w = G_WIDTH
    chunk = RWKV_CHUNK

    @pl.when(si == 0)
    def _():
        ext[:, 0:8, :] = jnp.zeros((group, 8, PB_COLS), F32)
        state[...] = jnp.zeros_like(state)

    shifted = []
    for b in range(group):
        p_b = pb_ref[b]
        ext[b, 8:, :] = p_b
        prev = ext[b, 7:7 + seq_rows, :]
        shifted.append(p_b + mu_ref[...] * (prev - p_b))
        ext[b, 0:8, :] = ext[b, seq_rows:seq_rows + 8, :]
    ps = jnp.concatenate(shifted, axis=0)

    head_mask = _head_ones()
    head_ones = head_mask.astype(BF16)

    def head_sum(t):
        return _dot2_exact_rhs(t, head_ones)

    r = ps[:, 0:g_w]
    k = ps[:, g_w:2 * g_w]
    v = ps[:, 2 * g_w:3 * g_w]
    tail = ps[:, 3 * g_w:4 * g_w]
    w = w0_ref[...] + _dot(jnp.tanh(tail), wup_ref[...])
    lw = -jnp.exp(-0.5) * _sigmoid(w)
    a = _sigmoid(a0_ref[...] + _dot(tail, aup_ref[...]))
    gate = _dot(_sigmoid(tail), gup_ref[...])
    kk = k * kk_ref[...]
    kk = kk * lax.rsqrt(jnp.maximum(head_sum(kk * kk), 1e-24))
    k2 = k * (1.0 + (a - 1.0) * ka_ref[...])
    bonus = head_sum(r * k2 * rk_ref[...]) * v
    r_s[...] = r
    k_s[...] = k2
    v_s[...] = v
    lw_s[...] = lw
    a_s[...] = -kk
    b_s[...] = kk * a

    row_i = lax.broadcasted_iota(jnp.int32, (chunk, g_w), 0)
    col_j = lax.broadcasted_iota(jnp.int32, (chunk, g_w), 1) % chunk
    strict_lower = row_i > col_j
    lower = row_i >= col_j
    eye = jnp.where(row_i == col_j, 1.0, 0.0)
    tri = (lax.broadcasted_iota(jnp.int32, (chunk, chunk), 0)
           >= lax.broadcasted_iota(jnp.int32, (chunk, chunk), 1)).astype(BF16)
    n_doublings = chunk.bit_length() - 1

    head_mask16 = head_ones

    def bd(x):
        return _block_diag(x.astype(BF16), head_mask16)

    def mm(a, b16):
        return jnp.dot(a.astype(BF16), b16, preferred_element_type=F32)

    ids = range(group)

    def group_local(c):
        sls = [pl.ds(pl.multiple_of(i * seq_rows + c * chunk, chunk), chunk) for i in ids]
        r_c, k_c, v_c, lw_c, a_c, b_c = ([t[sl, :] for sl in sls]
                                         for t in (r_s, k_s, v_s, lw_s, a_s, b_s))
        cw = [_dot2_exact_lhs(tri, lw_c[i]) for i in ids]
        cw_last = [cw[i][chunk - 1:chunk, :] for i in ids]
        a_t = [a_c[i] * jnp.exp(cw[i] - lw_c[i]) for i in ids]
        r_t = [r_c[i] * jnp.exp(cw[i]) for i in ids]
        e_inv = [jnp.exp(-cw[i]) for i in ids]
        bd_v = [bd(v_c[i]) for i in ids]
        pair = [lax.dot_general(
            jnp.concatenate([a_t[i], r_t[i]], axis=0).astype(BF16),
            jnp.concatenate([bd(b_c[i] * e_inv[i]), bd(k_c[i] * e_inv[i])], axis=0),
            (((1,), (1,)), ((), ())), preferred_element_type=F32) for i in ids]
        l_ak = [jnp.where(strict_lower, pair[i][0:chunk, g_w:2 * g_w], 0.0) for i in ids]
        m_rb = [jnp.where(lower, pair[i][chunk:2 * chunk, 0:g_w], 0.0) for i in ids]
        m_rk = [jnp.where(lower, pair[i][chunk:2 * chunk, g_w:2 * g_w], 0.0) for i in ids]

        z0 = [mm(l_ak[i], bd_v[i]) for i in ids]
        l_pow = [jnp.where(strict_lower, pair[i][0:chunk, 0:g_w], 0.0) for i in ids]
        t_inv = [eye + l_pow[i] for i in ids]
        l_pow = [mm(l_pow[i], bd(l_pow[i])) for i in ids]
        for it in range(1, n_doublings):
            last = it == n_doublings - 1
            res = [mm(l_pow[i], jnp.concatenate(
                [bd(t_inv[i])] + ([] if last else [bd(l_pow[i])]), axis=1)) for i in ids]
            t_inv = [t_inv[i] + res[i][:, 0:g_w] for i in ids]
            if not last:
                l_pow = [res[i][:, g_w:2 * g_w] for i in ids]
        sol = [mm(t_inv[i], jnp.concatenate([bd(a_t[i]), bd(z0[i])], axis=1)) for i in ids]
        x1 = [sol[i][:, 0:g_w] for i in ids]
        x2 = [sol[i][:, g_w:2 * g_w] for i in ids]

        q = [mm(m_rb[i], jnp.concatenate([bd(x1[i]), bd(x2[i])], axis=1)) for i in ids]
        q2 = [q[i][:, g_w:2 * g_w] + mm(m_rk[i], bd_v[i]) for i in ids]
        lhs = [jnp.concatenate([x1[i], r_t[i] + q[i][:, 0:g_w]], axis=0).astype(BF16) for i in ids]
        rhs = []
        for i in ids:
            e_fut = jnp.exp(cw_last[i] - cw[i])
            rhs.append(jnp.concatenate([b_c[i] * e_fut, k_c[i] * e_fut], axis=0).astype(BF16))
        decay = [jnp.exp(cw_last[i]) for i in ids]

        s0 = [state[i] for i in ids]
        us = [lax.dot_general(lhs[i], s0[i].astype(BF16), (((1,), (1,)), ((), ())),
                              preferred_element_type=F32) for i in ids]
        uv_t = []
        for i in ids:
            y_s[sls[i], :] = us[i][chunk:2 * chunk] + q2[i]
            u = us[i][0:chunk] + x2[i]
            uv_t.append(jnp.transpose(jnp.concatenate([u, v_c[i]], axis=0)).astype(BF16))
        upd = [jnp.dot(uv_t[i], rhs[i], preferred_element_type=F32) for i in ids]
        for i in ids:
            state[i] = s0[i] * decay[i] + jnp.where(head_mask, upd[i], 0.0)

    def chunk_step(c, carry):
        group_local(c)
        return carry

    lax.fori_loop(0, seq_rows // chunk, chunk_step, 0)

    y = y_s[...]
    inv_n = 1.0 / HEAD_DIM
    mean = head_sum(y) * inv_n
    dev = y - mean
    var = head_sum(dev * dev) * inv_n
    yn = dev * lax.rsqrt(var + RWKV_LN_EPS) * lnw_ref[...] + lnb_ref[...]
    out = (yn + bonus) * gate
    for b in range(group):
        o_ref[b] = out[b * seq_rows:(b + 1) * seq_rows, :]


def _rwkv_call(pb, mu, w0, w_up, a0, a_up, g_up, k_k, k_a, r_k, ln_w, ln_b):
    b, s, _ = pb.shape
    g_w = G_WIDTH
    row = lambda t: t.reshape(1, g_w)
    o_a = RWKV_W_RANK
    o_g = o_a + RWKV_A_RANK
    wup = jnp.zeros((g_w, g_w), F32).at[0:o_a].set(w_up).astype(BF16)
    aup = jnp.zeros((g_w, g_w), F32).at[o_a:o_g].set(a_up).astype(BF16)
    gup = jnp.zeros((g_w, g_w), F32).at[o_g:o_g + RWKV_G_RANK].set(g_up).astype(BF16)
    mu_p = _pad_cols(mu.reshape(1, -1), PB_COLS)
    vec = _const_spec((1, g_w))
    mat = _const_spec((g_w, g_w))
    group = min(RWKV_GROUP, b)
    seq_rows = MIX_ROWS // group
    seq = pltpu.VMEM((MIX_ROWS, g_w), F32)
    return pl.pallas_call(
        _rwkv_kernel,
        grid=(b // group, s // seq_rows),
        in_specs=[pl.BlockSpec((group, seq_rows, PB_COLS), lambda bi, si: (bi, si, 0)),
                  _const_spec((1, PB_COLS)), vec, mat, vec, mat, mat, vec, vec, vec, vec, vec],
        out_specs=pl.BlockSpec((group, seq_rows, g_w), lambda bi, si: (bi, si, 0)),
        out_shape=jax.ShapeDtypeStruct((b, s, g_w), F32),
        scratch_shapes=[pltpu.VMEM((group, seq_rows + 8, PB_COLS), F32),
                        pltpu.VMEM((group, g_w, g_w), F32),
                        seq, seq, seq, seq, seq, seq, seq],
        compiler_params=_params(2),
    )(pb, mu_p, row(w0), wup, row(a0), aup, gup, row(k_k), row(k_a), row(r_k), row(ln_w), row(ln_b))


def _gelu_tanh(x):
    return 0.5 * x * (1.0 + jnp.tanh(0.7978845608028654 * (x + 0.044715 * x * x * x)))


def _pos_features(pos, width):
    lane = lax.broadcasted_iota(jnp.int32, (pos.shape[0], width), 1)
    feat = jnp.where(lane == 0, pos // ALIBI_SPLIT,
                     jnp.where(lane == 1, pos % ALIBI_SPLIT, jnp.where(lane == 2, 1, 0)))
    return feat.astype(F32).astype(BF16)


def _nsa_compress_kernel(x_ref, pos_ref, w_first_ref, w_second_ref, kw2_ref, vw2t_ref, kn_ref,
                         kcmp_ref, vcmpt_ref):
    n_rows = x_ref.shape[1]
    hid = kw2_ref.shape[0]
    x = x_ref[0]
    first = _dot(x + pos_ref[0:1, :], w_first_ref[...])
    second = _dot(x + pos_ref[1:2, :], w_second_ref[...])
    hidden = _gelu_tanh(first + pltpu.roll(second, n_rows - 1, 0))
    k_cmp = _dot(hidden[:, 0:hid], kw2_ref[...])
    kcmp_ref[0, :, 0:HEAD_DIM] = _rms_rows(k_cmp, kn_ref[...]).astype(BF16)
    last_token = (lax.broadcasted_iota(jnp.int32, (n_rows, 1), 0) * NSA_CMP_STRIDE
                  + (NSA_CMP_LEN - 1))
    kcmp_ref[0, :, HEAD_DIM:2 * HEAD_DIM] = _pos_features(last_token, HEAD_DIM)
    vcmpt_ref[0] = _dot_nt(vw2t_ref[...], hidden[:, hid:2 * hid]).astype(BF16)


def _interleave_kv(k_part, v_part):
    zeros = jnp.zeros_like(k_part)
    top = jnp.concatenate([k_part, zeros], axis=2)
    bottom = jnp.concatenate([zeros, v_part], axis=2)
    return jnp.concatenate([top, bottom], axis=1).reshape(NSA_CMP_STRIDE * LANES, -1)


def _nsa_compress_call(kcvc, pos, kw1, kw2, vw1, vw2, k_norm0):
    b, n_chunks, width = kcvc.shape
    hid = kw1.shape[1]
    split = lambda w1, half: w1.reshape(2, NSA_CMP_STRIDE, HEAD_DIM, hid)[half]
    w_first = _interleave_kv(split(kw1, 0), split(vw1, 0)).astype(BF16)
    w_second = _interleave_kv(split(kw1, 1), split(vw1, 1)).astype(BF16)
    pos_halves = pos.reshape(2, NSA_CMP_STRIDE, HEAD_DIM)
    pos2 = jnp.concatenate([pos_halves, pos_halves], axis=2).reshape(2, width)
    pos2 = jnp.pad(pos2, ((0, 6), (0, 0)))
    return pl.pallas_call(
        _nsa_compress_kernel,
        grid=(b,),
        in_specs=[pl.BlockSpec((1, n_chunks, width), lambda bi: (bi, 0, 0)), _const_spec((8, width)),
                  _const_spec((width, 2 * hid)), _const_spec((width, 2 * hid)),
                  _const_spec((hid, HEAD_DIM)), _const_spec((HEAD_DIM, hid)),
                  _const_spec((1, HEAD_DIM))],
        out_specs=[pl.BlockSpec((1, n_chunks, 2 * HEAD_DIM), lambda bi: (bi, 0, 0)),
                   pl.BlockSpec((1, HEAD_DIM, n_chunks), lambda bi: (bi, 0, 0))],
        out_shape=[jax.ShapeDtypeStruct((b, n_chunks, 2 * HEAD_DIM), BF16),
                   jax.ShapeDtypeStruct((b, HEAD_DIM, n_chunks), BF16)],
        compiler_params=_params(1),
    )(kcvc, pos2, w_first, w_second, kw2.astype(BF16), vw2.T.astype(BF16),
      k_norm0.reshape(1, HEAD_DIM))


def _col_reduce(x, pair_op, reduce_fn, slab=32):
    parts = [x[i:i + slab] for i in range(0, x.shape[0], slab)]
    while len(parts) > 1:
        parts = [pair_op(parts[i], parts[i + 1]) for i in range(0, len(parts), 2)]
    return reduce_fn(parts[0], axis=0, keepdims=True)


def _nsa_attn_kernel(q_ref, gate_ref, qn_ref, kc_ref, vct_ref, kk_ref, vt_ref, o_ref, s_ref):
    qi = pl.program_id(1)
    tq = ATT_ROWS
    hw = N_HEADS * tq
    q0 = qi * tq
    n_cmp_rows = kc_ref.shape[1]
    n_sel = kk_ref.shape[1] // NSA_SEL_BLOCK

    head_ones = _head_ones().astype(BF16)
    q = q_ref[0]
    ms = _dot2_exact_rhs(q * q, head_ones) * (1.0 / HEAD_DIM)
    qn = q * lax.rsqrt(ms + RMS_EPS) * qn_ref[...] * (HEAD_DIM ** -0.5)
    qt = jnp.transpose(qn)
    q4t = jnp.concatenate([qt[h * HEAD_DIM:(h + 1) * HEAD_DIM, :] for h in range(N_HEADS)],
                          axis=1).astype(BF16)
    lane = lax.broadcasted_iota(jnp.int32, (1, hw), 1)
    slope = jnp.exp2(-2.0 * (lane // tq + 1).astype(F32))
    t_lane = lane % tq

    f_row = lax.broadcasted_iota(jnp.int32, (HEAD_DIM, hw), 0)
    q_feat = jnp.where(f_row == 0, ALIBI_SPLIT * slope,
                       jnp.where(f_row == 1, slope,
                                 jnp.where(f_row == 2, -slope * q0.astype(F32), 0.0))).astype(BF16)
    zero_half = jnp.zeros_like(q4t)
    q_cmp = jnp.concatenate([q4t, q_feat], axis=0)
    q_sel = jnp.concatenate([q4t, zero_half, q_feat, zero_half], axis=0)
    q_win = jnp.concatenate([zero_half, q4t, q_feat, zero_half], axis=0)

    n_idx = lax.broadcasted_iota(jnp.int32, (n_cmp_rows, hw), 0)
    valid_c = (q0 + t_lane) >= (n_idx * NSA_CMP_STRIDE + (NSA_CMP_LEN - 1))
    s_c = (jnp.dot(kc_ref[0], q_cmp, preferred_element_type=F32)
           + jnp.where(valid_c, 0.0, NSA_NEG))
    m_c = jnp.maximum(jnp.max(s_c, axis=0, keepdims=True), SCORE_FLOOR)
    p_c = jnp.exp(s_c - m_c)
    l_c = jnp.sum(p_c, axis=0, keepdims=True)
    p_c = p_c * (1.0 / jnp.maximum(l_c, 1e-30))
    o_cmp = jnp.dot(vct_ref[0], p_c.astype(BF16), preferred_element_type=F32)

    p_heads = p_c[:, 0:tq]
    for h in range(1, N_HEADS):
        p_heads = p_heads + p_c[:, h * tq:(h + 1) * tq]
    per_sel = NSA_SEL_BLOCK // NSA_CMP_STRIDE
    j_ov = lax.broadcasted_iota(jnp.int32, (n_sel, n_cmp_rows), 0)
    n_ov = lax.broadcasted_iota(jnp.int32, (n_sel, n_cmp_rows), 1)
    overlap_t = ((n_ov >= per_sel * j_ov - (NSA_CMP_LEN // NSA_CMP_STRIDE - 1))
                 & (n_ov <= per_sel * j_ov + per_sel - 1)).astype(BF16)
    p_hi = p_heads.astype(BF16)
    p_rest = p_heads - p_hi.astype(F32)
    p_mid = p_rest.astype(BF16)
    p_lo = (p_rest - p_mid.astype(F32)).astype(BF16)
    imp = (jnp.dot(overlap_t, p_hi, preferred_element_type=F32)
           + jnp.dot(overlap_t, p_mid, preferred_element_type=F32)
           + jnp.dot(overlap_t, p_lo, preferred_element_type=F32))
    j_idx = lax.broadcasted_iota(jnp.int32, (n_sel, tq), 0)
    cur = (q0 + lax.broadcasted_iota(jnp.int32, (1, tq), 1)) // NSA_SEL_BLOCK
    forced = (j_idx == 0) | (j_idx == cur) | (j_idx == cur - 1)
    imp = jnp.where(j_idx <= cur, imp + jnp.where(forced, NSA_FORCE_BONUS, 0.0), -1.0)
    rank = jnp.zeros((n_sel, tq), F32)
    for jp in range(n_sel):
        other = imp[jp:jp + 1, :]
        ahead = (other > imp) | ((other == imp) & (j_idx > jp))
        rank = rank + jnp.where(ahead, 1.0, 0.0)
    chosen = jnp.where(rank < float(min(NSA_TOP_N, n_sel)), 0.0, NSA_NEG)

    key_i = lax.broadcasted_iota(jnp.int32, (tq, tq), 0)
    query_i = lax.broadcasted_iota(jnp.int32, (tq, tq), 1)
    causal = jnp.where(query_i >= key_i, 0.0, NSA_NEG)
    window_start = jnp.where(key_i > query_i, 0.0, NSA_NEG)
    tiles_per_step = KEY_STEP // tq
    blocks_per_tile = tq // NSA_SEL_BLOCK
    win_tiles = NSA_WINDOW // tq
    last_step = qi // tiles_per_step

    def branch(n_steps, step_of, q_pad, v_rows, bias_fn):
        def prefetch(i, slot):
            p = step_of(i)
            k0 = pl.multiple_of(p * KEY_STEP, KEY_STEP)
            raw = jnp.dot(kk_ref[0, pl.ds(k0, KEY_STEP), :], q_pad, preferred_element_type=F32)
            s = raw + jnp.concatenate([bias_fn(p)] * N_HEADS, axis=1)
            s_ref[slot] = s
            return _col_reduce(s, jnp.maximum, jnp.max)

        def absorb(i, slot, s_max, m, l, acc):
            m_new = jnp.maximum(m, s_max)
            alpha = jnp.exp(m - m_new)
            prob = jnp.exp(s_ref[slot] - m_new)
            l = alpha * l + _col_reduce(prob, jnp.add, jnp.sum)
            v_t = vt_ref[0, step_of(i)][v_rows[0]:v_rows[1], :]
            pv = jnp.dot(v_t, prob.astype(BF16), preferred_element_type=F32)
            return m_new, l, alpha * acc + pv

        def pair(i2, carry):
            max0, m, l, acc = carry
            i = 2 * i2
            max1 = prefetch(i + 1, 1)
            m, l, acc = absorb(i, 0, max0, m, l, acc)
            max0 = prefetch(i + 2, 0)
            m, l, acc = absorb(i + 1, 1, max1, m, l, acc)
            return max0, m, l, acc

        init = (prefetch(0, 0), jnp.full((1, hw), SCORE_FLOOR, F32), jnp.zeros((1, hw), F32),
                jnp.zeros((HEAD_DIM, hw), F32))
        n_pairs = (n_steps - 1) // 2
        max0, m, l, acc = lax.fori_loop(0, n_pairs, pair, init)
        i = 2 * n_pairs

        def two_left(m, l, acc):
            max1 = prefetch(i + 1, 1)
            m, l, acc = absorb(i, 0, max0, m, l, acc)
            return absorb(i + 1, 1, max1, m, l, acc)

        def one_left(m, l, acc):
            return absorb(i, 0, max0, m, l, acc)

        _, l, acc = lax.cond(n_steps - i == 2, two_left, one_left, m, l, acc)
        return acc * (1.0 / l)

    def sel_tile_bias(kt):
        rows = [jnp.max(jnp.where(j_idx == blocks_per_tile * kt + i, chosen, NSA_NEG),
                        axis=0, keepdims=True) for i in range(blocks_per_tile)]
        bias = rows[-1]
        for i in range(blocks_per_tile - 2, -1, -1):
            bias = jnp.where(key_i < (i + 1) * NSA_SEL_BLOCK, rows[i], bias)
        return bias

    def sel_step_bias(p):
        tiles = []
        for i in range(tiles_per_step):
            kt = p * tiles_per_step + i
            tiles.append(sel_tile_bias(kt) + jnp.where(kt == qi, causal, 0.0)
                         + jnp.where(kt > qi, NSA_NEG, 0.0))
        return jnp.concatenate(tiles, axis=0)

    def win_step_bias(p):
        tiles = []
        for i in range(tiles_per_step):
            behind = qi - (p * tiles_per_step + i)
            bias = jnp.where(behind == 0, causal, jnp.where(behind == win_tiles, window_start, 0.0))
            tiles.append(bias + jnp.where((behind >= 0) & (behind <= win_tiles), 0.0, NSA_NEG))
        return jnp.concatenate(tiles, axis=0)

    first_win_step = jnp.maximum(qi - win_tiles, 0) // tiles_per_step
    o_win = branch(last_step - first_win_step + 1, lambda i: first_win_step + i,
                   q_win, (HEAD_DIM, 2 * HEAD_DIM), win_step_bias)

    blocks_per_step = KEY_STEP // NSA_SEL_BLOCK
    block_id = lax.broadcasted_iota(jnp.int32, (n_sel, 1), 0)
    block_used = jnp.max(chosen, axis=1, keepdims=True) > SCORE_FLOOR
    earliest = jnp.min(jnp.where(block_used & (block_id >= blocks_per_step), block_id, n_sel),
                       axis=0, keepdims=True)[0, 0]
    first_sel_step = jnp.clip(earliest // blocks_per_step, 1, jnp.maximum(last_step, 1))
    n_sel_steps = jnp.where(last_step >= 1, last_step - first_sel_step + 2, 1)
    o_sel = branch(n_sel_steps, lambda i: jnp.where(i == 0, 0, first_sel_step + i - 1),
                   q_sel, (0, HEAD_DIM), sel_step_bias)

    g_t = jnp.transpose(_sigmoid(gate_ref[0]))

    def gate_row(c):
        return jnp.concatenate([g_t[3 * h + c:3 * h + c + 1, :] for h in range(N_HEADS)], axis=1)

    o = gate_row(0) * o_cmp + gate_row(1) * o_sel + gate_row(2) * o_win
    o_hd = jnp.concatenate([o[:, h * tq:(h + 1) * tq] for h in range(N_HEADS)], axis=0)
    o_ref[0] = jnp.transpose(o_hd)


def _nsa_attn_call(q, gates, q_norm, kcmp, vcmpt, kk, vt):
    b, s, _ = q.shape
    n_cmp_rows = kcmp.shape[1]
    qn = jnp.tile(q_norm, N_HEADS).reshape(1, G_WIDTH)
    return pl.pallas_call(
        _nsa_attn_kernel,
        grid=(b, s // ATT_ROWS),
        in_specs=[pl.BlockSpec((1, ATT_ROWS, G_WIDTH), lambda bi, qi: (bi, qi, 0)),
                  pl.BlockSpec((1, ATT_ROWS, LANES), lambda bi, qi: (bi, qi, 0)),
                  _const_spec((1, G_WIDTH)),
                  pl.BlockSpec((1, n_cmp_rows, 2 * HEAD_DIM), lambda bi, qi: (bi, 0, 0)),
                  pl.BlockSpec((1, HEAD_DIM, n_cmp_rows), lambda bi, qi: (bi, 0, 0)),
                  pl.BlockSpec((1, s, 2 * LANES), lambda bi, qi: (bi, 0, 0)),
                  pl.BlockSpec((1, s // KEY_STEP, LANES, KEY_STEP), lambda bi, qi: (bi, 0, 0, 0))],
        out_specs=pl.BlockSpec((1, ATT_ROWS, G_WIDTH), lambda bi, qi: (bi, qi, 0)),
        out_shape=jax.ShapeDtypeStruct((b, s, G_WIDTH), F32),
        scratch_shapes=[pltpu.VMEM((2, KEY_STEP, N_HEADS * ATT_ROWS), F32)],
        compiler_params=_params(2),
    )(q, gates, qn, kcmp, vcmpt, kk, vt)


def kernel(x, ffn1_norm, ffn1_w_gate, ffn1_w_up, ffn1_w_down, mix_norm, w_in, pool_w, pool_scale, rwkv_mu, rwkv_w0, rwkv_w_up, rwkv_a0, rwkv_a_up, rwkv_g_up, rwkv_k_k, rwkv_k_a, rwkv_r_k, rwkv_ln_w, rwkv_ln_b, nsa_q_norm, nsa_k_norm, nsa_cmp_pos, nsa_cmp_k_w1, nsa_cmp_k_w2, nsa_cmp_v_w1, nsa_cmp_v_w2, conv_w, w_out, ffn2_norm, ffn2_w_gate, ffn2_w_up, ffn2_w_down):
    b, s, d = x.shape
    n_tok = b * s
    x2d = x.reshape(n_tok, d)
    for l in range(ffn1_norm.shape[0]):
        x2d = _ffn_call(x2d, ffn1_norm[l], ffn1_w_gate[l], ffn1_w_up[l], ffn1_w_down[l])
        ya, pb, q, gates, kcvc, kk, vt, yd = _mixin_call(
            x2d.reshape(b, s, d), mix_norm[l], w_in[l], pool_w[l], pool_scale[l], conv_w[l],
            nsa_k_norm[l][1], nsa_k_norm[l][2])
        yb = _rwkv_call(pb, rwkv_mu[l], rwkv_w0[l], rwkv_w_up[l], rwkv_a0[l], rwkv_a_up[l],
                        rwkv_g_up[l], rwkv_k_k[l], rwkv_k_a[l], rwkv_r_k[l], rwkv_ln_w[l],
                        rwkv_ln_b[l])
        kcmp, vcmpt = _nsa_compress_call(kcvc, nsa_cmp_pos[l], nsa_cmp_k_w1[l], nsa_cmp_k_w2[l],
                                         nsa_cmp_v_w1[l], nsa_cmp_v_w2[l], nsa_k_norm[l][0])
        yc = _nsa_attn_call(q, gates, nsa_q_norm[l], kcmp, vcmpt, kk, vt)
        mix = tuple(t.reshape(n_tok, G_WIDTH) for t in (ya, yb, yc, yd))
        x2d = _ffn_call(x2d, ffn2_norm[l], ffn2_w_gate[l], ffn2_w_up[l], ffn2_w_down[l],
                        mix=mix, w_out=w_out[l])
    return x2d.reshape(b, s, d)
```

```python
import functools

import jax
import jax.numpy as jnp
from jax import lax
from jax.experimental import pallas as pl
from jax.experimental.pallas import tpu as pltpu

F32 = jnp.float32
BF16 = jnp.bfloat16

N_MIXERS = 4
HEAD_DIM = 64
N_HEADS = 4
G_WIDTH = N_HEADS * HEAD_DIM
RMS_EPS = 1e-6
POOL_WINDOWS = (2, 4, 8, 16)
POOL_HALO = 16
CONV_HALO = 8
RWKV_W_RANK, RWKV_A_RANK, RWKV_G_RANK = 64, 32, 64
RWKV_LN_EPS = 64e-5
RWKV_CHUNK = 64
RWKV_GROUP = 8
NSA_CMP_LEN = 32
NSA_CMP_STRIDE = 16
NSA_SEL_BLOCK = 64
NSA_TOP_N = 16
NSA_WINDOW = 512
NSA_FORCE_BONUS = 1e4
NSA_NEG = -1e9
SCORE_FLOOR = 0.5 * NSA_NEG
ALIBI_SPLIT = 64

PB_COLS = 1024
PC_COLS = 768
PD_COLS = 768
P_PAD = G_WIDTH + PB_COLS + PC_COLS + PD_COLS

V7X_VMEM_BYTES = 64 * 1024 * 1024
VMEM_LIMIT = V7X_VMEM_BYTES - 8 * 1024 * 1024
LANES = 128

FFN_ROWS = 512
FFN_COLS = 256
MIX_ROWS = 512
ATT_ROWS = 256
KEY_STEP = 256


def _params(n_axes):
    return pltpu.CompilerParams(dimension_semantics=("arbitrary",) * n_axes,
                                vmem_limit_bytes=VMEM_LIMIT)


def _const_spec(shape):
    nd = len(shape)
    return pl.BlockSpec(shape, lambda *_: (0,) * nd, pipeline_mode=pl.Buffered(1))


def _dot(a, b):
    return jnp.dot(a.astype(BF16), b.astype(BF16), preferred_element_type=F32)


def _dot_nt(a, b):
    return lax.dot_general(a.astype(BF16), b.astype(BF16), (((1,), (1,)), ((), ())),
                           preferred_element_type=F32)


def _split(x):
    hi = x.astype(BF16)
    lo = (x - hi.astype(F32)).astype(BF16)
    return hi, lo


def _dot2_exact_rhs(a, b_bf16):
    ah, al = _split(a)
    return (jnp.dot(ah, b_bf16, preferred_element_type=F32)
            + jnp.dot(al, b_bf16, preferred_element_type=F32))


def _dot2_exact_lhs(a_bf16, b):
    bh, bl = _split(b)
    return (jnp.dot(a_bf16, bh, preferred_element_type=F32)
            + jnp.dot(a_bf16, bl, preferred_element_type=F32))


def _rms_rows(x, g):
    return x * lax.rsqrt(jnp.mean(x * x, axis=-1, keepdims=True) + RMS_EPS) * g


def _sigmoid(x):
    return 1.0 / (1.0 + jnp.exp(-x))


def _head_ones():
    r = lax.broadcasted_iota(jnp.int32, (G_WIDTH, G_WIDTH), 0) // HEAD_DIM
    c = lax.broadcasted_iota(jnp.int32, (G_WIDTH, G_WIDTH), 1) // HEAD_DIM
    return r == c


def _ffn_body(x_in, g_ref, wg_ref, wu_ref, wd_ref, o_ref, acc_ref):
    h = _rms_rows(x_in, g_ref[...]).astype(BF16)
    for c in range(wg_ref.shape[1] // FFN_COLS):
        cols = slice(c * FFN_COLS, (c + 1) * FFN_COLS)
        gate = jnp.dot(h, wg_ref[:, cols], preferred_element_type=F32)
        up = jnp.dot(h, wu_ref[:, cols], preferred_element_type=F32)
        act = (gate * _sigmoid(gate) * up).astype(BF16)
        down = jnp.dot(act, wd_ref[cols, :], preferred_element_type=F32)
        if c == 0:
            acc_ref[...] = down
        else:
            acc_ref[...] += down
    o_ref[...] = x_in + 0.5 * acc_ref[...]


def _ffn_kernel(x_ref, g_ref, wg_ref, wu_ref, wd_ref, o_ref, acc_ref):
    _ffn_body(x_ref[...], g_ref, wg_ref, wu_ref, wd_ref, o_ref, acc_ref)


def _out_ffn_kernel(x_ref, ya_ref, yb_ref, yc_ref, yd_ref, wo_ref,
                    g_ref, wg_ref, wu_ref, wd_ref, o_ref, acc_ref):
    x1 = x_ref[...]
    for i, y_ref in enumerate((ya_ref, yb_ref, yc_ref, yd_ref)):
        x1 = x1 + _dot(y_ref[...], wo_ref[i])
    _ffn_body(x1, g_ref, wg_ref, wu_ref, wd_ref, o_ref, acc_ref)


def _ffn_call(x2d, g, w_gate, w_up, w_down, mix=None, w_out=None):
    n_tok, d = x2d.shape
    wg, wu, wd = w_gate.astype(BF16), w_up.astype(BF16), w_down.astype(BF16)
    row_spec = pl.BlockSpec((FFN_ROWS, d), lambda i: (i, 0))
    w_specs = [_const_spec((1, d)), _const_spec(wg.shape), _const_spec(wu.shape),
               _const_spec(wd.shape)]
    w_args = [g.reshape(1, d), wg, wu, wd]
    if mix is None:
        kern, in_specs, args = _ffn_kernel, [row_spec] + w_specs, [x2d] + w_args
    else:
        y_spec = pl.BlockSpec((FFN_ROWS, G_WIDTH), lambda i: (i, 0))
        wo = w_out.astype(BF16).reshape(N_MIXERS, G_WIDTH, d)
        kern = _out_ffn_kernel
        in_specs = [row_spec] + [y_spec] * N_MIXERS + [_const_spec(wo.shape)] + w_specs
        args = [x2d] + list(mix) + [wo] + w_args
    return pl.pallas_call(
        kern,
        grid=(n_tok // FFN_ROWS,),
        in_specs=in_specs,
        out_specs=row_spec,
        out_shape=jax.ShapeDtypeStruct((n_tok, d), F32),
        scratch_shapes=[pltpu.VMEM((FFN_ROWS, d), F32)],
        compiler_params=_params(1),
    )(*args)


def _mixin_kernel(x_ref, g_ref, w_ref, poolw_ref, pools_ref, convw_ref, kn_ref,
                  ya_ref, pb_ref, q_ref, gate_ref, kcvc_ref, kk_ref, vt_ref, yd_ref,
                  pa_ext, z_ext, kcvc_s):
    si = pl.program_id(1)
    rows = x_ref.shape[1]

    @pl.when(si == 0)
    def _():
        pa_ext[0:POOL_HALO, :] = jnp.zeros((POOL_HALO, G_WIDTH), F32)
        z_ext[0:CONV_HALO, :] = jnp.zeros((CONV_HALO, G_WIDTH), F32)

    c0, c1, c2 = G_WIDTH, G_WIDTH + PB_COLS, G_WIDTH + PB_COLS + PC_COLS
    half = rows // 2
    u_parts, pd_parts, pc_parts = [], [], []
    for r0 in (0, half):
        h = _rms_rows(x_ref[0, r0:r0 + half, :], g_ref[...]).astype(BF16)
        u_parts.append(jnp.dot(h, w_ref[:, 0:c0], preferred_element_type=F32))
        pd_parts.append(jnp.dot(h, w_ref[:, c2:c2 + PD_COLS], preferred_element_type=F32))
        pb_ref[0, r0:r0 + half, :] = jnp.dot(h, w_ref[:, c0:c1], preferred_element_type=F32)
        pc_parts.append(jnp.dot(h, w_ref[:, c1:c2], preferred_element_type=F32))
    u = jnp.concatenate(u_parts, axis=0)
    pd = jnp.concatenate(pd_parts, axis=0)
    pc = jnp.concatenate(pc_parts, axis=0)

    q_ref[0] = pc[:, 0:G_WIDTH]
    gate_ref[0] = pc[:, 5 * LANES:6 * LANES]
    kcvc_s[...] = pc[:, 2 * LANES:3 * LANES]
    for t in range(NSA_CMP_STRIDE):
        kcvc_ref[0, :, t * LANES:(t + 1) * LANES] = kcvc_s[pl.ds(t, rows // NSA_CMP_STRIDE,
                                                                stride=NSA_CMP_STRIDE), :]
    kx = pc[:, 3 * LANES:4 * LANES]
    first = lax.broadcasted_iota(jnp.int32, kx.shape, 1) < HEAD_DIM
    sq = kx * kx
    ms_first = jnp.sum(jnp.where(first, sq, 0.0), axis=-1, keepdims=True) * (1.0 / HEAD_DIM)
    ms_second = jnp.sum(jnp.where(first, 0.0, sq), axis=-1, keepdims=True) * (1.0 / HEAD_DIM)
    inv = jnp.where(first, lax.rsqrt(ms_first + RMS_EPS), lax.rsqrt(ms_second + RMS_EPS))
    kk_ref[0, :, 0:LANES] = (kx * inv * kn_ref[...]).astype(BF16)
    key_pos = si * rows + lax.broadcasted_iota(jnp.int32, (rows, 1), 0)
    kk_ref[0, :, LANES:2 * LANES] = _pos_features(key_pos, LANES)
    vx = pc[:, 4 * LANES:5 * LANES]
    for j in range(vt_ref.shape[1]):
        vt_ref[0, j] = jnp.transpose(vx[j * KEY_STEP:(j + 1) * KEY_STEP, :]).astype(BF16)

    pa_ext[POOL_HALO:, :] = u
    lane_group = lax.broadcasted_iota(jnp.int32, (rows, G_WIDTH), 1) // (G_WIDTH // len(POOL_WINDOWS))
    pos = si * rows + lax.broadcasted_iota(jnp.int32, (rows, G_WIDTH), 0)
    total = u
    for k in range(1, max(POOL_WINDOWS)):
        first_group = sum(1 for w in POOL_WINDOWS if w <= k)
        shifted = pa_ext[POOL_HALO - k:POOL_HALO - k + rows, :]
        total = total + jnp.where(lane_group >= first_group, shifted, 0.0)
    window = jnp.left_shift(2, lane_group)
    cnt = jnp.minimum(pos + 1, window).astype(F32)
    pooled = total / cnt
    ya = _dot(pooled - u, poolw_ref[...]) * pools_ref[...]
    ya_ref[0] = ya
    pa_ext[0:POOL_HALO, :] = pa_ext[rows:rows + POOL_HALO, :]

    cu = pd[:, 0:G_WIDTH]
    cb = pd[:, G_WIDTH:2 * G_WIDTH]
    cc = pd[:, 2 * G_WIDTH:3 * G_WIDTH]
    z = cc * cu
    z_ext[CONV_HALO:, :] = z
    y = (convw_ref[0:1, :] * z_ext[CONV_HALO - 2:CONV_HALO - 2 + rows, :]
         + convw_ref[1:2, :] * z_ext[CONV_HALO - 1:CONV_HALO - 1 + rows, :]
         + convw_ref[2:3, :] * z)
    yd_ref[0] = cb * y
    z_ext[0:CONV_HALO, :] = z_ext[rows:rows + CONV_HALO, :]


def _pad_cols(w, width):
    return jnp.pad(w, ((0, 0), (0, width - w.shape[1])))


def _w_in_padded(w_in):
    g = G_WIDTH
    o = 0
    w_a = w_in[:, o:o + g]; o += g
    rwkv_cols = 3 * g + RWKV_W_RANK + RWKV_A_RANK + RWKV_G_RANK
    w_b = _pad_cols(w_in[:, o:o + rwkv_cols], PB_COLS); o += rwkv_cols
    q = w_in[:, o:o + g]; o += g
    kc, vc, ksl, vsl, kwn, vwn = [w_in[:, o + i * HEAD_DIM:o + (i + 1) * HEAD_DIM] for i in range(6)]
    o += 6 * HEAD_DIM
    gates = w_in[:, o:o + 3 * N_HEADS]; o += 3 * N_HEADS
    w_c = jnp.concatenate([q, kc, vc, ksl, kwn, vsl, vwn, _pad_cols(gates, LANES)], axis=1)
    w_d = w_in[:, o:o + PD_COLS]
    return jnp.concatenate([w_a, w_b, w_c, w_d], axis=1).astype(BF16)


def _mixin_call(x, g, w_in, pool_w, pool_scale, conv_w, k_norm_sel, k_norm_win):
    b, s, d = x.shape
    w = _w_in_padded(w_in)
    kn = jnp.concatenate([k_norm_sel, k_norm_win]).reshape(1, LANES)
    cmp_rows = MIX_ROWS // NSA_CMP_STRIDE
    key_steps = MIX_ROWS // KEY_STEP
    n_groups = len(POOL_WINDOWS)
    pool_ch = G_WIDTH // n_groups
    poolw = jnp.zeros((G_WIDTH, G_WIDTH), F32)
    for gi in range(n_groups):
        poolw = poolw.at[gi * pool_ch:(gi + 1) * pool_ch, gi * pool_ch:(gi + 1) * pool_ch].set(pool_w[gi])
    convw = jnp.pad(conv_w, ((0, 8 - conv_w.shape[0]), (0, 0)))

    def out_spec(c):
        return pl.BlockSpec((1, MIX_ROWS, c), lambda bi, si: (bi, si, 0))

    return pl.pallas_call(
        _mixin_kernel,
        grid=(b, s // MIX_ROWS),
        in_specs=[pl.BlockSpec((1, MIX_ROWS, d), lambda bi, si: (bi, si, 0)),
                  _const_spec((1, d)), _const_spec(w.shape),
                  _const_spec((G_WIDTH, G_WIDTH)), _const_spec((1, G_WIDTH)),
                  _const_spec((8, G_WIDTH)), _const_spec((1, LANES))],
        out_specs=[out_spec(G_WIDTH), out_spec(PB_COLS), out_spec(G_WIDTH), out_spec(LANES),
                   pl.BlockSpec((1, cmp_rows, NSA_CMP_STRIDE * LANES), lambda bi, si: (bi, si, 0)),
                   out_spec(2 * LANES),
                   pl.BlockSpec((1, key_steps, LANES, KEY_STEP), lambda bi, si: (bi, si, 0, 0)),
                   out_spec(G_WIDTH)],
        out_shape=[jax.ShapeDtypeStruct((b, s, G_WIDTH), F32),
                   jax.ShapeDtypeStruct((b, s, PB_COLS), F32),
                   jax.ShapeDtypeStruct((b, s, G_WIDTH), F32),
                   jax.ShapeDtypeStruct((b, s, LANES), F32),
                   jax.ShapeDtypeStruct((b, s // NSA_CMP_STRIDE, NSA_CMP_STRIDE * LANES), F32),
                   jax.ShapeDtypeStruct((b, s, 2 * LANES), BF16),
                   jax.ShapeDtypeStruct((b, s // KEY_STEP, LANES, KEY_STEP), BF16),
                   jax.ShapeDtypeStruct((b, s, G_WIDTH), F32)],
        scratch_shapes=[pltpu.VMEM((MIX_ROWS + POOL_HALO, G_WIDTH), F32),
                        pltpu.VMEM((MIX_ROWS + CONV_HALO, G_WIDTH), F32),
                        pltpu.VMEM((MIX_ROWS, LANES), F32)],
        compiler_params=_params(2),
    )(x, g.reshape(1, d), w, poolw.astype(BF16), pool_scale.reshape(1, G_WIDTH), convw, kn)


def _block_diag(x, mask01):
    return jnp.concatenate([x] * N_HEADS, axis=0) * mask01


def _rwkv_kernel(pb_ref, mu_ref, w0_ref, wup_ref, a0_ref, aup_ref, gup_ref, kk_ref, ka_ref,
                 rk_ref, lnw_ref, lnb_ref, o_ref,
                 ext, state, r_s, k_s, v_s, lw_s, a_s, b_s, y_s):
    si = pl.program_id(1)
    group, seq_rows = pb_ref.shape[0], pb_ref.shape[1]
    g_w = G_WIDTH
    chunk = RWKV_CHUNK

    @pl.when(si == 0)
    def _():
        ext[:, 0:8, :] = jnp.zeros((group, 8, PB_COLS), F32)
        state[...] = jnp.zeros_like(state)

    shifted = []
    for b in range(group):
        p_b = pb_ref[b]
        ext[b, 8:, :] = p_b
        prev = ext[b, 7:7 + seq_rows, :]
        shifted.append(p_b + mu_ref[...] * (prev - p_b))
        ext[b, 0:8, :] = ext[b, seq_rows:seq_rows + 8, :]
    ps = jnp.concatenate(shifted, axis=0)

    head_mask = _head_ones()
    head_ones = head_mask.astype(BF16)

    def head_sum(t):
        return _dot2_exact_rhs(t, head_ones)

    r = ps[:, 0:g_w]
    k = ps[:, g_w:2 * g_w]
    v = ps[:, 2 * g_w:3 * g_w]
    tail = ps[:, 3 * g_w:4 * g_w]
    w = w0_ref[...] + _dot(jnp.tanh(tail), wup_ref[...])
    lw = -jnp.exp(-0.5) * _sigmoid(w)
    a = _sigmoid(a0_ref[...] + _dot(tail, aup_ref[...]))
    gate = _dot(_sigmoid(tail), gup_ref[...])
    kk = k * kk_ref[...]
    kk = kk * lax.rsqrt(jnp.maximum(head_sum(kk * kk), 1e-24))
    k2 = k * (1.0 + (a - 1.0) * ka_ref[...])
    bonus = head_sum(r * k2 * rk_ref[...]) * v
    r_s[...] = r
    k_s[...] = k2
    v_s[...] = v
    lw_s[...] = lw
    a_s[...] = -kk
    b_s[...] = kk * a

    row_i = lax.broadcasted_iota(jnp.int32, (chunk, g_w), 0)
    col_j = lax.broadcasted_iota(jnp.int32, (chunk, g_w), 1) % chunk
    strict_lower = row_i > col_j
    lower = row_i >= col_j
    eye = jnp.where(row_i == col_j, 1.0, 0.0)
    tri = (lax.broadcasted_iota(jnp.int32, (chunk, chunk), 0)
           >= lax.broadcasted_iota(jnp.int32, (chunk, chunk), 1)).astype(BF16)
    n_doublings = chunk.bit_length() - 1

    head_mask16 = head_ones

    def bd(x):
        return _block_diag(x.astype(BF16), head_mask16)

    def mm(a, b16):
        return jnp.dot(a.astype(BF16), b16, preferred_element_type=F32)

    ids = range(group)

    def group_local(c):
        sls = [pl.ds(pl.multiple_of(i * seq_rows + c * chunk, chunk), chunk) for i in ids]
        r_c, k_c, v_c, lw_c, a_c, b_c = ([t[sl, :] for sl in sls]
                                         for t in (r_s, k_s, v_s, lw_s, a_s, b_s))
        cw = [_dot2_exact_lhs(tri, lw_c[i]) for i in ids]
        cw_last = [cw[i][chunk - 1:chunk, :] for i in ids]
        a_t = [a_c[i] * jnp.exp(cw[i] - lw_c[i]) for i in ids]
        r_t = [r_c[i] * jnp.exp(cw[i]) for i in ids]
        e_inv = [jnp.exp(-cw[i]) for i in ids]
        bd_v = [bd(v_c[i]) for i in ids]
        pair = [lax.dot_general(
            jnp.concatenate([a_t[i], r_t[i]], axis=0).astype(BF16),
            jnp.concatenate([bd(b_c[i] * e_inv[i]), bd(k_c[i] * e_inv[i])], axis=0),
            (((1,), (1,)), ((), ())), preferred_element_type=F32) for i in ids]
        l_ak = [jnp.where(strict_lower, pair[i][0:chunk, g_w:2 * g_w], 0.0) for i in ids]
        m_rb = [jnp.where(lower, pair[i][chunk:2 * chunk, 0:g_w], 0.0) for i in ids]
        m_rk = [jnp.where(lower, pair[i][chunk:2 * chunk, g_w:2 * g_w], 0.0) for i in ids]

        z0 = [mm(l_ak[i], bd_v[i]) for i in ids]
        l_pow = [jnp.where(strict_lower, pair[i][0:chunk, 0:g_w], 0.0) for i in ids]
        t_inv = [eye + l_pow[i] for i in ids]
        l_pow = [mm(l_pow[i], bd(l_pow[i])) for i in ids]
        for it in range(1, n_doublings):
            last = it == n_doublings - 1
            res = [mm(l_pow[i], jnp.concatenate(
                [bd(t_inv[i])] + ([] if last else [bd(l_pow[i])]), axis=1)) for i in ids]
            t_inv = [t_inv[i] + res[i][:, 0:g_w] for i in ids]
            if not last:
                l_pow = [res[i][:, g_w:2 * g_w] for i in ids]
        sol = [mm(t_inv[i], jnp.concatenate([bd(a_t[i]), bd(z0[i])], axis=1)) for i in ids]
        x1 = [sol[i][:, 0:g_w] for i in ids]
        x2 = [sol[i][:, g_w:2 * g_w] for i in ids]

        q = [mm(m_rb[i], jnp.concatenate([bd(x1[i]), bd(x2[i])], axis=1)) for i in ids]
        q2 = [q[i][:, g_w:2 * g_w] + mm(m_rk[i], bd_v[i]) for i in ids]
        lhs = [jnp.concatenate([x1[i], r_t[i] + q[i][:, 0:g_w]], axis=0).astype(BF16) for i in ids]
        rhs = []
        for i in ids:
            e_fut = jnp.exp(cw_last[i] - cw[i])
            rhs.append(jnp.concatenate([b_c[i] * e_fut, k_c[i] * e_fut], axis=0).astype(BF16))
        decay = [jnp.exp(cw_last[i]) for i in ids]

        s0 = [state[i] for i in ids]
        us = [lax.dot_general(lhs[i], s0[i].astype(BF16), (((1,), (1,)), ((), ())),
                              preferred_element_type=F32) for i in ids]
        uv_t = []
        for i in ids:
            y_s[sls[i], :] = us[i][chunk:2 * chunk] + q2[i]
            u = us[i][0:chunk] + x2[i]
            uv_t.append(jnp.transpose(jnp.concatenate([u, v_c[i]], axis=0)).astype(BF16))
        upd = [jnp.dot(uv_t[i], rhs[i], preferred_element_type=F32) for i in ids]
        for i in ids:
            state[i] = s0[i] * decay[i] + jnp.where(head_mask, upd[i], 0.0)

    def chunk_step(c, carry):
        group_local(c)
        return carry

    lax.fori_loop(0, seq_rows // chunk, chunk_step, 0)

    y = y_s[...]
    inv_n = 1.0 / HEAD_DIM
    mean = head_sum(y) * inv_n
    dev = y - mean
    var = head_sum(dev * dev) * inv_n
    yn = dev * lax.rsqrt(var + RWKV_LN_EPS) * lnw_ref[...] + lnb_ref[...]
    out = (yn + bonus) * gate
    for b in range(group):
        o_ref[b] = out[b * seq_rows:(b + 1) * seq_rows, :]


def _rwkv_call(pb, mu, w0, w_up, a0, a_up, g_up, k_k, k_a, r_k, ln_w, ln_b):
    b, s, _ = pb.shape
    g_w = G_WIDTH
    row = lambda t: t.reshape(1, g_w)
    o_a = RWKV_W_RANK
    o_g = o_a + RWKV_A_RANK
    wup = jnp.zeros((g_w, g_w), F32).at[0:o_a].set(w_up).astype(BF16)
    aup = jnp.zeros((g_w, g_w), F32).at[o_a:o_g].set(a_up).astype(BF16)
    gup = jnp.zeros((g_w, g_w), F32).at[o_g:o_g + RWKV_G_RANK].set(g_up).astype(BF16)
    mu_p = _pad_cols(mu.reshape(1, -1), PB_COLS)
    vec = _const_spec((1, g_w))
    mat = _const_spec((g_w, g_w))
    group = min(RWKV_GROUP, b)
    seq_rows = MIX_ROWS // group
    seq = pltpu.VMEM((MIX_ROWS, g_w), F32)
    return pl.pallas_call(
        _rwkv_kernel,
        grid=(b // group, s // seq_rows),
        in_specs=[pl.BlockSpec((group, seq_rows, PB_COLS), lambda bi, si: (bi, si, 0)),
                  _const_spec((1, PB_COLS)), vec, mat, vec, mat, mat, vec, vec, vec, vec, vec],
        out_specs=pl.BlockSpec((group, seq_rows, g_w), lambda bi, si: (bi, si, 0)),
        out_shape=jax.ShapeDtypeStruct((b, s, g_w), F32),
        scratch_shapes=[pltpu.VMEM((group, seq_rows + 8, PB_COLS), F32),
                        pltpu.VMEM((group, g_w, g_w), F32),
                        seq, seq, seq, seq, seq, seq, seq],
        compiler_params=_params(2),
    )(pb, mu_p, row(w0), wup, row(a0), aup, gup, row(k_k), row(k_a), row(r_k), row(ln_w), row(ln_b))


def _gelu_tanh(x):
    return 0.5 * x * (1.0 + jnp.tanh(0.7978845608028654 * (x + 0.044715 * x * x * x)))


def _pos_features(pos, width):
    lane = lax.broadcasted_iota(jnp.int32, (pos.shape[0], width), 1)
    feat = jnp.where(lane == 0, pos // ALIBI_SPLIT,
                     jnp.where(lane == 1, pos % ALIBI_SPLIT, jnp.where(lane == 2, 1, 0)))
    return feat.astype(F32).astype(BF16)


def _nsa_compress_kernel(x_ref, pos_ref, w_first_ref, w_second_ref, kw2_ref, vw2t_ref, kn_ref,
                         kcmp_ref, vcmpt_ref):
    n_rows = x_ref.shape[1]
    hid = kw2_ref.shape[0]
    x = x_ref[0]
    first = _dot(x + pos_ref[0:1, :], w_first_ref[...])
    second = _dot(x + pos_ref[1:2, :], w_second_ref[...])
    hidden = _gelu_tanh(first + pltpu.roll(second, n_rows - 1, 0))
    k_cmp = _dot(hidden[:, 0:hid], kw2_ref[...])
    kcmp_ref[0, :, 0:HEAD_DIM] = _rms_rows(k_cmp, kn_ref[...]).astype(BF16)
    last_token = (lax.broadcasted_iota(jnp.int32, (n_rows, 1), 0) * NSA_CMP_STRIDE
                  + (NSA_CMP_LEN - 1))
    kcmp_ref[0, :, HEAD_DIM:2 * HEAD_DIM] = _pos_features(last_token, HEAD_DIM)
    vcmpt_ref[0] = _dot_nt(vw2t_ref[...], hidden[:, hid:2 * hid]).astype(BF16)


def _interleave_kv(k_part, v_part):
    zeros = jnp.zeros_like(k_part)
    top = jnp.concatenate([k_part, zeros], axis=2)
    bottom = jnp.concatenate([zeros, v_part], axis=2)
    return jnp.concatenate([top, bottom], axis=1).reshape(NSA_CMP_STRIDE * LANES, -1)


def _nsa_compress_call(kcvc, pos, kw1, kw2, vw1, vw2, k_norm0):
    b, n_chunks, width = kcvc.shape
    hid = kw1.shape[1]
    split = lambda w1, half: w1.reshape(2, NSA_CMP_STRIDE, HEAD_DIM, hid)[half]
    w_first = _interleave_kv(split(kw1, 0), split(vw1, 0)).astype(BF16)
    w_second = _interleave_kv(split(kw1, 1), split(vw1, 1)).astype(BF16)
    pos_halves = pos.reshape(2, NSA_CMP_STRIDE, HEAD_DIM)
    pos2 = jnp.concatenate([pos_halves, pos_halves], axis=2).reshape(2, width)
    pos2 = jnp.pad(pos2, ((0, 6), (0, 0)))
    return pl.pallas_call(
        _nsa_compress_kernel,
        grid=(b,),
        in_specs=[pl.BlockSpec((1, n_chunks, width), lambda bi: (bi, 0, 0)), _const_spec((8, width)),
                  _const_spec((width, 2 * hid)), _const_spec((width, 2 * hid)),
                  _const_spec((hid, HEAD_DIM)), _const_spec((HEAD_DIM, hid)),
                  _const_spec((1, HEAD_DIM))],
        out_specs=[pl.BlockSpec((1, n_chunks, 2 * HEAD_DIM), lambda bi: (bi, 0, 0)),
                   pl.BlockSpec((1, HEAD_DIM, n_chunks), lambda bi: (bi, 0, 0))],
        out_shape=[jax.ShapeDtypeStruct((b, n_chunks, 2 * HEAD_DIM), BF16),
                   jax.ShapeDtypeStruct((b, HEAD_DIM, n_chunks), BF16)],
        compiler_params=_params(1),
    )(kcvc, pos2, w_first, w_second, kw2.astype(BF16), vw2.T.astype(BF16),
      k_norm0.reshape(1, HEAD_DIM))


def _col_reduce(x, pair_op, reduce_fn, slab=32):
    parts = [x[i:i + slab] for i in range(0, x.shape[0], slab)]
    while len(parts) > 1:
        parts = [pair_op(parts[i], parts[i + 1]) for i in range(0, len(parts), 2)]
    return reduce_fn(parts[0], axis=0, keepdims=True)


def _nsa_attn_kernel(q_ref, gate_ref, qn_ref, kc_ref, vct_ref, kk_ref, vt_ref, o_ref, s_ref):
    qi = pl.program_id(1)
    tq = ATT_ROWS
    hw = N_HEADS * tq
    q0 = qi * tq
    n_cmp_rows = kc_ref.shape[1]
    n_sel = kk_ref.shape[1] // NSA_SEL_BLOCK

    head_ones = _head_ones().astype(BF16)
    q = q_ref[0]
    ms = _dot2_exact_rhs(q * q, head_ones) * (1.0 / HEAD_DIM)
    qn = q * lax.rsqrt(ms + RMS_EPS) * qn_ref[...] * (HEAD_DIM ** -0.5)
    qt = jnp.transpose(qn)
    q4t = jnp.concatenate([qt[h * HEAD_DIM:(h + 1) * HEAD_DIM, :] for h in range(N_HEADS)],
                          axis=1).astype(BF16)
    lane = lax.broadcasted_iota(jnp.int32, (1, hw), 1)
    slope = jnp.exp2(-2.0 * (lane // tq + 1).astype(F32))
    t_lane = lane % tq

    f_row = lax.broadcasted_iota(jnp.int32, (HEAD_DIM, hw), 0)
    q_feat = jnp.where(f_row == 0, ALIBI_SPLIT * slope,
                       jnp.where(f_row == 1, slope,
                                 jnp.where(f_row == 2, -slope * q0.astype(F32), 0.0))).astype(BF16)
    zero_half = jnp.zeros_like(q4t)
    q_cmp = jnp.concatenate([q4t, q_feat], axis=0)
    q_sel = jnp.concatenate([q4t, zero_half, q_feat, zero_half], axis=0)
    q_win = jnp.concatenate([zero_half, q4t, q_feat, zero_half], axis=0)

    n_idx = lax.broadcasted_iota(jnp.int32, (n_cmp_rows, hw), 0)
    valid_c = (q0 + t_lane) >= (n_idx * NSA_CMP_STRIDE + (NSA_CMP_LEN - 1))
    s_c = (jnp.dot(kc_ref[0], q_cmp, preferred_element_type=F32)
           + jnp.where(valid_c, 0.0, NSA_NEG))
    m_c = jnp.maximum(jnp.max(s_c, axis=0, keepdims=True), SCORE_FLOOR)
    p_c = jnp.exp(s_c - m_c)
    l_c = jnp.sum(p_c, axis=0, keepdims=True)
    p_c = p_c * (1.0 / jnp.maximum(l_c, 1e-30))
    o_cmp = jnp.dot(vct_ref[0], p_c.astype(BF16), preferred_element_type=F32)

    p_heads = p_c[:, 0:tq]
    for h in range(1, N_HEADS):
        p_heads = p_heads + p_c[:, h * tq:(h + 1) * tq]
    per_sel = NSA_SEL_BLOCK // NSA_CMP_STRIDE
    j_ov = lax.broadcasted_iota(jnp.int32, (n_sel, n_cmp_rows), 0)
    n_ov = lax.broadcasted_iota(jnp.int32, (n_sel, n_cmp_rows), 1)
    overlap_t = ((n_ov >= per_sel * j_ov - (NSA_CMP_LEN // NSA_CMP_STRIDE - 1))
                 & (n_ov <= per_sel * j_ov + per_sel - 1)).astype(BF16)
    p_hi = p_heads.astype(BF16)
    p_rest = p_heads - p_hi.astype(F32)
    p_mid = p_rest.astype(BF16)
    p_lo = (p_rest - p_mid.astype(F32)).astype(BF16)
    imp = (jnp.dot(overlap_t, p_hi, preferred_element_type=F32)
           + jnp.dot(overlap_t, p_mid, preferred_element_type=F32)
           + jnp.dot(overlap_t, p_lo, preferred_element_type=F32))
    j_idx = lax.broadcasted_iota(jnp.int32, (n_sel, tq), 0)
    cur = (q0 + lax.broadcasted_iota(jnp.int32, (1, tq), 1)) // NSA_SEL_BLOCK
    forced = (j_idx == 0) | (j_idx == cur) | (j_idx == cur - 1)
    imp = jnp.where(j_idx <= cur, imp + jnp.where(forced, NSA_FORCE_BONUS, 0.0), -1.0)
    rank = jnp.zeros((n_sel, tq), F32)
    for jp in range(n_sel):
        other = imp[jp:jp + 1, :]
        ahead = (other > imp) | ((other == imp) & (j_idx > jp))
        rank = rank + jnp.where(ahead, 1.0, 0.0)
    chosen = jnp.where(rank < float(min(NSA_TOP_N, n_sel)), 0.0, NSA_NEG)

    key_i = lax.broadcasted_iota(jnp.int32, (tq, tq), 0)
    query_i = lax.broadcasted_iota(jnp.int32, (tq, tq), 1)
    causal = jnp.where(query_i >= key_i, 0.0, NSA_NEG)
    window_start = jnp.where(key_i > query_i, 0.0, NSA_NEG)
    tiles_per_step = KEY_STEP // tq
    blocks_per_tile = tq // NSA_SEL_BLOCK
    win_tiles = NSA_WINDOW // tq
    last_step = qi // tiles_per_step

    def branch(n_steps, step_of, q_pad, v_rows, bias_fn):
        def prefetch(i, slot):
            p = step_of(i)
            k0 = pl.multiple_of(p * KEY_STEP, KEY_STEP)
            raw = jnp.dot(kk_ref[0, pl.ds(k0, KEY_STEP), :], q_pad, preferred_element_type=F32)
            s = raw + jnp.concatenate([bias_fn(p)] * N_HEADS, axis=1)
            s_ref[slot] = s
            return _col_reduce(s, jnp.maximum, jnp.max)

        def absorb(i, slot, s_max, m, l, acc):
            m_new = jnp.maximum(m, s_max)
            alpha = jnp.exp(m - m_new)
            prob = jnp.exp(s_ref[slot] - m_new)
            l = alpha * l + _col_reduce(prob, jnp.add, jnp.sum)
            v_t = vt_ref[0, step_of(i)][v_rows[0]:v_rows[1], :]
            pv = jnp.dot(v_t, prob.astype(BF16), preferred_element_type=F32)
            return m_new, l, alpha * acc + pv

        def pair(i2, carry):
            max0, m, l, acc = carry
            i = 2 * i2
            max1 = prefetch(i + 1, 1)
            m, l, acc = absorb(i, 0, max0, m, l, acc)
            max0 = prefetch(i + 2, 0)
            m, l, acc = absorb(i + 1, 1, max1, m, l, acc)
            return max0, m, l, acc

        init = (prefetch(0, 0), jnp.full((1, hw), SCORE_FLOOR, F32), jnp.zeros((1, hw), F32),
                jnp.zeros((HEAD_DIM, hw), F32))
        n_pairs = (n_steps - 1) // 2
        max0, m, l, acc = lax.fori_loop(0, n_pairs, pair, init)
        i = 2 * n_pairs

        def two_left(m, l, acc):
            max1 = prefetch(i + 1, 1)
            m, l, acc = absorb(i, 0, max0, m, l, acc)
            return absorb(i + 1, 1, max1, m, l, acc)

        def one_left(m, l, acc):
            return absorb(i, 0, max0, m, l, acc)

        _, l, acc = lax.cond(n_steps - i == 2, two_left, one_left, m, l, acc)
        return acc * (1.0 / l)

    def sel_tile_bias(kt):
        rows = [jnp.max(jnp.where(j_idx == blocks_per_tile * kt + i, chosen, NSA_NEG),
                        axis=0, keepdims=True) for i in range(blocks_per_tile)]
        bias = rows[-1]
        for i in range(blocks_per_tile - 2, -1, -1):
            bias = jnp.where(key_i < (i + 1) * NSA_SEL_BLOCK, rows[i], bias)
        return bias

    def sel_step_bias(p):
        tiles = []
        for i in range(tiles_per_step):
            kt = p * tiles_per_step + i
            tiles.append(sel_tile_bias(kt) + jnp.where(kt == qi, causal, 0.0)
                         + jnp.where(kt > qi, NSA_NEG, 0.0))
        return jnp.concatenate(tiles, axis=0)

    def win_step_bias(p):
        tiles = []
        for i in range(tiles_per_step):
            behind = qi - (p * tiles_per_step + i)
            bias = jnp.where(behind == 0, causal, jnp.where(behind == win_tiles, window_start, 0.0))
            tiles.append(bias + jnp.where((behind >= 0) & (behind <= win_tiles), 0.0, NSA_NEG))
        return jnp.concatenate(tiles, axis=0)

    first_win_step = jnp.maximum(qi - win_tiles, 0) // tiles_per_step
    o_win = branch(last_step - first_win_step + 1, lambda i: first_win_step + i,
                   q_win, (HEAD_DIM, 2 * HEAD_DIM), win_step_bias)

    blocks_per_step = KEY_STEP // NSA_SEL_BLOCK
    block_id = lax.broadcasted_iota(jnp.int32, (n_sel, 1), 0)
    block_used = jnp.max(chosen, axis=1, keepdims=True) > SCORE_FLOOR
    earliest = jnp.min(jnp.where(block_used & (block_id >= blocks_per_step), block_id, n_sel),
                       axis=0, keepdims=True)[0, 0]
    first_sel_step = jnp.clip(earliest // blocks_per_step, 1, jnp.maximum(last_step, 1))
    n_sel_steps = jnp.where(last_step >= 1, last_step - first_sel_step + 2, 1)
    o_sel = branch(n_sel_steps, lambda i: jnp.where(i == 0, 0, first_sel_step + i - 1),
                   q_sel, (0, HEAD_DIM), sel_step_bias)

    g_t = jnp.transpose(_sigmoid(gate_ref[0]))

    def gate_row(c):
        return jnp.concatenate([g_t[3 * h + c:3 * h + c + 1, :] for h in range(N_HEADS)], axis=1)

    o = gate_row(0) * o_cmp + gate_row(1) * o_sel + gate_row(2) * o_win
    o_hd = jnp.concatenate([o[:, h * tq:(h + 1) * tq] for h in range(N_HEADS)], axis=0)
    o_ref[0] = jnp.transpose(o_hd)


def _nsa_attn_call(q, gates, q_norm, kcmp, vcmpt, kk, vt):
    b, s, _ = q.shape
    n_cmp_rows = kcmp.shape[1]
    qn = jnp.tile(q_norm, N_HEADS).reshape(1, G_WIDTH)
    return pl.pallas_call(
        _nsa_attn_kernel,
        grid=(b, s // ATT_ROWS),
        in_specs=[pl.BlockSpec((1, ATT_ROWS, G_WIDTH), lambda bi, qi: (bi, qi, 0)),
                  pl.BlockSpec((1, ATT_ROWS, LANES), lambda bi, qi: (bi, qi, 0)),
                  _const_spec((1, G_WIDTH)),
                  pl.BlockSpec((1, n_cmp_rows, 2 * HEAD_DIM), lambda bi, qi: (bi, 0, 0)),
                  pl.BlockSpec((1, HEAD_DIM, n_cmp_rows), lambda bi, qi: (bi, 0, 0)),
                  pl.BlockSpec((1, s, 2 * LANES), lambda bi, qi: (bi, 0, 0)),
                  pl.BlockSpec((1, s // KEY_STEP, LANES, KEY_STEP), lambda bi, qi: (bi, 0, 0, 0))],
        out_specs=pl.BlockSpec((1, ATT_ROWS, G_WIDTH), lambda bi, qi: (bi, qi, 0)),
        out_shape=jax.ShapeDtypeStruct((b, s, G_WIDTH), F32),
        scratch_shapes=[pltpu.VMEM((2, KEY_STEP, N_HEADS * ATT_ROWS), F32)],
        compiler_params=_params(2),
    )(q, gates, qn, kcmp, vcmpt, kk, vt)


def kernel(x, ffn1_norm, ffn1_w_gate, ffn1_w_up, ffn1_w_down, mix_norm, w_in, pool_w, pool_scale, rwkv_mu, rwkv_w0, rwkv_w_up, rwkv_a0, rwkv_a_up, rwkv_g_up, rwkv_k_k, rwkv_k_a, rwkv_r_k, rwkv_ln_w, rwkv_ln_b, nsa_q_norm, nsa_k_norm, nsa_cmp_pos, nsa_cmp_k_w1, nsa_cmp_k_w2, nsa_cmp_v_w1, nsa_cmp_v_w2, conv_w, w_out, ffn2_norm, ffn2_w_gate, ffn2_w_up, ffn2_w_down):
    b, s, d = x.shape
    n_tok = b * s
    x2d = x.reshape(n_tok, d)
    for l in range(ffn1_norm.shape[0]):
        x2d = _ffn_call(x2d, ffn1_norm[l], ffn1_w_gate[l], ffn1_w_up[l], ffn1_w_down[l])
        ya, pb, q, gates, kcvc, kk, vt, yd = _mixin_call(
            x2d.reshape(b, s, d), mix_norm[l], w_in[l], pool_w[l], pool_scale[l], conv_w[l],
            nsa_k_norm[l][1], nsa_k_norm[l][2])
        yb = _rwkv_call(pb, rwkv_mu[l], rwkv_w0[l], rwkv_w_up[l], rwkv_a0[l], rwkv_a_up[l],
                        rwkv_g_up[l], rwkv_k_k[l], rwkv_k_a[l], rwkv_r_k[l], rwkv_ln_w[l],
                        rwkv_ln_b[l])
        kcmp, vcmpt = _nsa_compress_call(kcvc, nsa_cmp_pos[l], nsa_cmp_k_w1[l], nsa_cmp_k_w2[l],
                                         nsa_cmp_v_w1[l], nsa_cmp_v_w2[l], nsa_k_norm[l][0])
        yc = _nsa_attn_call(q, gates, nsa_q_norm[l], kcmp, vcmpt, kk, vt)
        mix = tuple(t.reshape(n_tok, G_WIDTH) for t in (ya, yb, yc, yd))
        x2d = _ffn_call(x2d, ffn2_norm[l], ffn2_w_gate[l], ffn2_w_up[l], ffn2_w_down[l],
                        mix=mix, w_out=w_out[l])
    return x2d.reshape(b, s, d)
```

```python
import functools

import jax
import jax.numpy as jnp
from jax import lax
from jax.experimental import pallas as pl
from jax.experimental.pallas import tpu as pltpu

F32 = jnp.float32
BF16 = jnp.bfloat16

N_MIXERS = 4
HEAD_DIM = 64
N_HEADS = 4
G_WIDTH = N_HEADS * HEAD_DIM
RMS_EPS = 1e-6
POOL_WINDOWS = (2, 4, 8, 16)
POOL_HALO = 16
CONV_HALO = 8
RWKV_W_RANK, RWKV_A_RANK, RWKV_G_RANK = 64, 32, 64
RWKV_LN_EPS = 64e-5
RWKV_CHUNK = 64
RWKV_GROUP = 8
NSA_CMP_LEN = 32
NSA_CMP_STRIDE = 16
NSA_SEL_BLOCK = 64
NSA_TOP_N = 16
NSA_WINDOW = 512
NSA_FORCE_BONUS = 1e4
NSA_NEG = -1e9
SCORE_FLOOR = 0.5 * NSA_NEG
ALIBI_SPLIT = 64
ALIBI_PARTS = 3
LOG2E = 1.4426950408889634

PB_COLS = 1024
PC_COLS = 768
PD_COLS = 768
P_PAD = G_WIDTH + PB_COLS + PC_COLS + PD_COLS

V7X_VMEM_BYTES = 64 * 1024 * 1024
VMEM_LIMIT = V7X_VMEM_BYTES - 8 * 1024 * 1024
LANES = 128

FFN_ROWS = 512
FFN_COLS = 256
MIX_ROWS = 512
ATT_ROWS = 256
KEY_STEP = 256


def _params(n_axes):
    return pltpu.CompilerParams(dimension_semantics=("arbitrary",) * n_axes,
                                vmem_limit_bytes=VMEM_LIMIT)


def _const_spec(shape):
    nd = len(shape)
    return pl.BlockSpec(shape, lambda *_: (0,) * nd, pipeline_mode=pl.Buffered(1))


def _dot(a, b):
    return jnp.dot(a.astype(BF16), b.astype(BF16), preferred_element_type=F32)


def _dot_nt(a, b):
    return lax.dot_general(a.astype(BF16), b.astype(BF16), (((1,), (1,)), ((), ())),
                           preferred_element_type=F32)


def _split(x):
    hi = x.astype(BF16)
    lo = (x - hi.astype(F32)).astype(BF16)
    return hi, lo


def _dot2_exact_rhs(a, b_bf16):
    ah, al = _split(a)
    return (jnp.dot(ah, b_bf16, preferred_element_type=F32)
            + jnp.dot(al, b_bf16, preferred_element_type=F32))


def _dot2_exact_lhs(a_bf16, b):
    bh, bl = _split(b)
    return (jnp.dot(a_bf16, bh, preferred_element_type=F32)
            + jnp.dot(a_bf16, bl, preferred_element_type=F32))


def _rms_rows(x, g):
    return x * lax.rsqrt(jnp.mean(x * x, axis=-1, keepdims=True) + RMS_EPS) * g


def _sigmoid(x):
    return 1.0 / (1.0 + jnp.exp(-x))


def _head_ones():
    r = lax.broadcasted_iota(jnp.int32, (G_WIDTH, G_WIDTH), 0) // HEAD_DIM
    c = lax.broadcasted_iota(jnp.int32, (G_WIDTH, G_WIDTH), 1) // HEAD_DIM
    return r == c


def _ffn_body(x_in, g_ref, wg_ref, wu_ref, wd_ref, o_ref, acc_ref):
    h = _rms_rows(x_in, g_ref[...]).astype(BF16)
    for c in range(wg_ref.shape[1] // FFN_COLS):
        cols = slice(c * FFN_COLS, (c + 1) * FFN_COLS)
        gate = jnp.dot(h, wg_ref[:, cols], preferred_element_type=F32)
        up = jnp.dot(h, wu_ref[:, cols], preferred_element_type=F32)
        act = (gate * _sigmoid(gate) * up).astype(BF16)
        down = jnp.dot(act, wd_ref[cols, :], preferred_element_type=F32)
        if c == 0:
            acc_ref[...] = down
        else:
            acc_ref[...] += down
    o_ref[...] = x_in + 0.5 * acc_ref[...]


def _ffn_kernel(x_ref, g_ref, wg_ref, wu_ref, wd_ref, o_ref, acc_ref):
    _ffn_body(x_ref[...], g_ref, wg_ref, wu_ref, wd_ref, o_ref, acc_ref)


def _out_ffn_kernel(x_ref, ya_ref, yb_ref, yc_ref, yd_ref, wo_ref,
                    g_ref, wg_ref, wu_ref, wd_ref, o_ref, acc_ref):
    x1 = x_ref[...]
    for i, y_ref in enumerate((ya_ref, yb_ref, yc_ref, yd_ref)):
        x1 = x1 + _dot(y_ref[...], wo_ref[i])
    _ffn_body(x1, g_ref, wg_ref, wu_ref, wd_ref, o_ref, acc_ref)


def _ffn_call(x2d, g, w_gate, w_up, w_down, mix=None, w_out=None):
    n_tok, d = x2d.shape
    wg, wu, wd = w_gate.astype(BF16), w_up.astype(BF16), w_down.astype(BF16)
    row_spec = pl.BlockSpec((FFN_ROWS, d), lambda i: (i, 0))
    w_specs = [_const_spec((1, d)), _const_spec(wg.shape), _const_spec(wu.shape),
               _const_spec(wd.shape)]
    w_args = [g.reshape(1, d), wg, wu, wd]
    if mix is None:
        kern, in_specs, args = _ffn_kernel, [row_spec] + w_specs, [x2d] + w_args
    else:
        y_spec = pl.BlockSpec((FFN_ROWS, G_WIDTH), lambda i: (i, 0))
        wo = w_out.astype(BF16).reshape(N_MIXERS, G_WIDTH, d)
        kern = _out_ffn_kernel
        in_specs = [row_spec] + [y_spec] * N_MIXERS + [_const_spec(wo.shape)] + w_specs
        args = [x2d] + list(mix) + [wo] + w_args
    return pl.pallas_call(
        kern,
        grid=(n_tok // FFN_ROWS,),
        in_specs=in_specs,
        out_specs=row_spec,
        out_shape=jax.ShapeDtypeStruct((n_tok, d), F32),
        scratch_shapes=[pltpu.VMEM((FFN_ROWS, d), F32)],
        compiler_params=_params(1),
    )(*args)


def _mixin_kernel(x_ref, g_ref, w_ref, poolw_ref, pools_ref, convw_ref, kn_ref,
                  ya_ref, pb_ref, q_ref, gate_ref, kcvc_ref, kk_ref, vt_ref, yd_ref,
                  pa_ext, z_ext, kcvc_s):
    si = pl.program_id(1)
    rows = x_ref.shape[1]

    @pl.when(si == 0)
    def _():
        pa_ext[0:POOL_HALO, :] = jnp.zeros((POOL_HALO, G_WIDTH), F32)
        z_ext[0:CONV_HALO, :] = jnp.zeros((CONV_HALO, G_WIDTH), F32)

    c0, c1, c2 = G_WIDTH, G_WIDTH + PB_COLS, G_WIDTH + PB_COLS + PC_COLS
    half = rows // 2
    u_parts, pd_parts, pc_parts = [], [], []
    for r0 in (0, half):
        h = _rms_rows(x_ref[0, r0:r0 + half, :], g_ref[...]).astype(BF16)
        u_parts.append(jnp.dot(h, w_ref[:, 0:c0], preferred_element_type=F32))
        pd_parts.append(jnp.dot(h, w_ref[:, c2:c2 + PD_COLS], preferred_element_type=F32))
        pb_ref[0, r0:r0 + half, :] = jnp.dot(h, w_ref[:, c0:c1], preferred_element_type=F32)
        pc_parts.append(jnp.dot(h, w_ref[:, c1:c2], preferred_element_type=F32))
    u = jnp.concatenate(u_parts, axis=0)
    pd = jnp.concatenate(pd_parts, axis=0)
    pc = jnp.concatenate(pc_parts, axis=0)

    q_ref[0] = pc[:, 0:G_WIDTH]
    gate_ref[0] = pc[:, 5 * LANES:6 * LANES]
    kcvc_s[...] = pc[:, 2 * LANES:3 * LANES]
    for t in range(NSA_CMP_STRIDE):
        kcvc_ref[0, :, t * LANES:(t + 1) * LANES] = kcvc_s[pl.ds(t, rows // NSA_CMP_STRIDE,
                                                                stride=NSA_CMP_STRIDE), :]
    kx = pc[:, 3 * LANES:4 * LANES]
    first = lax.broadcasted_iota(jnp.int32, kx.shape, 1) < HEAD_DIM
    sq = kx * kx
    ms_first = jnp.sum(jnp.where(first, sq, 0.0), axis=-1, keepdims=True) * (1.0 / HEAD_DIM)
    ms_second = jnp.sum(jnp.where(first, 0.0, sq), axis=-1, keepdims=True) * (1.0 / HEAD_DIM)
    inv = jnp.where(first, lax.rsqrt(ms_first + RMS_EPS), lax.rsqrt(ms_second + RMS_EPS))
    kk_ref[0, :, 0:LANES] = (kx * inv * kn_ref[...]).astype(BF16)
    key_pos = si * rows + lax.broadcasted_iota(jnp.int32, (rows, 1), 0)
    kk_ref[0, :, LANES:2 * LANES] = _pos_features(key_pos, LANES)
    vx = pc[:, 4 * LANES:5 * LANES]
    for j in range(vt_ref.shape[1]):
        vt_ref[0, j] = jnp.transpose(vx[j * KEY_STEP:(j + 1) * KEY_STEP, :]).astype(BF16)

    pa_ext[POOL_HALO:, :] = u
    lane_group = lax.broadcasted_iota(jnp.int32, (rows, G_WIDTH), 1) // (G_WIDTH // len(POOL_WINDOWS))
    pos = si * rows + lax.broadcasted_iota(jnp.int32, (rows, G_WIDTH), 0)
    total = u
    for k in range(1, max(POOL_WINDOWS)):
        first_group = sum(1 for w in POOL_WINDOWS if w <= k)
        shifted = pa_ext[POOL_HALO - k:POOL_HALO - k + rows, :]
        total = total + jnp.where(lane_group >= first_group, shifted, 0.0)
    window = jnp.left_shift(2, lane_group)
    cnt = jnp.minimum(pos + 1, window).astype(F32)
    pooled = total / cnt
    ya = _dot(pooled - u, poolw_ref[...]) * pools_ref[...]
    ya_ref[0] = ya
    pa_ext[0:POOL_HALO, :] = pa_ext[rows:rows + POOL_HALO, :]

    cu = pd[:, 0:G_WIDTH]
    cb = pd[:, G_WIDTH:2 * G_WIDTH]
    cc = pd[:, 2 * G_WIDTH:3 * G_WIDTH]
    z = cc * cu
    z_ext[CONV_HALO:, :] = z
    y = (convw_ref[0:1, :] * z_ext[CONV_HALO - 2:CONV_HALO - 2 + rows, :]
         + convw_ref[1:2, :] * z_ext[CONV_HALO - 1:CONV_HALO - 1 + rows, :]
         + convw_ref[2:3, :] * z)
    yd_ref[0] = cb * y
    z_ext[0:CONV_HALO, :] = z_ext[rows:rows + CONV_HALO, :]


def _pad_cols(w, width):
    return jnp.pad(w, ((0, 0), (0, width - w.shape[1])))


def _w_in_padded(w_in):
    g = G_WIDTH
    o = 0
    w_a = w_in[:, o:o + g]; o += g
    rwkv_cols = 3 * g + RWKV_W_RANK + RWKV_A_RANK + RWKV_G_RANK
    w_b = _pad_cols(w_in[:, o:o + rwkv_cols], PB_COLS); o += rwkv_cols
    q = w_in[:, o:o + g]; o += g
    kc, vc, ksl, vsl, kwn, vwn = [w_in[:, o + i * HEAD_DIM:o + (i + 1) * HEAD_DIM] for i in range(6)]
    o += 6 * HEAD_DIM
    gates = w_in[:, o:o + 3 * N_HEADS]; o += 3 * N_HEADS
    w_c = jnp.concatenate([q, kc, vc, ksl, kwn, vsl, vwn, _pad_cols(gates, LANES)], axis=1)
    w_d = w_in[:, o:o + PD_COLS]
    return jnp.concatenate([w_a, w_b, w_c, w_d], axis=1).astype(BF16)


def _mixin_call(x, g, w_in, pool_w, pool_scale, conv_w, k_norm_sel, k_norm_win):
    b, s, d = x.shape
    w = _w_in_padded(w_in)
    kn = jnp.concatenate([k_norm_sel, k_norm_win]).reshape(1, LANES)
    cmp_rows = MIX_ROWS // NSA_CMP_STRIDE
    key_steps = MIX_ROWS // KEY_STEP
    n_groups = len(POOL_WINDOWS)
    pool_ch = G_WIDTH // n_groups
    poolw = jnp.zeros((G_WIDTH, G_WIDTH), F32)
    for gi in range(n_groups):
        poolw = poolw.at[gi * pool_ch:(gi + 1) * pool_ch, gi * pool_ch:(gi + 1) * pool_ch].set(pool_w[gi])
    convw = jnp.pad(conv_w, ((0, 8 - conv_w.shape[0]), (0, 0)))

    def out_spec(c):
        return pl.BlockSpec((1, MIX_ROWS, c), lambda bi, si: (bi, si, 0))

    return pl.pallas_call(
        _mixin_kernel,
        grid=(b, s // MIX_ROWS),
        in_specs=[pl.BlockSpec((1, MIX_ROWS, d), lambda bi, si: (bi, si, 0)),
                  _const_spec((1, d)), _const_spec(w.shape),
                  _const_spec((G_WIDTH, G_WIDTH)), _const_spec((1, G_WIDTH)),
                  _const_spec((8, G_WIDTH)), _const_spec((1, LANES))],
        out_specs=[out_spec(G_WIDTH), out_spec(PB_COLS), out_spec(G_WIDTH), out_spec(LANES),
                   pl.BlockSpec((1, cmp_rows, NSA_CMP_STRIDE * LANES), lambda bi, si: (bi, si, 0)),
                   out_spec(2 * LANES),
                   pl.BlockSpec((1, key_steps, LANES, KEY_STEP), lambda bi, si: (bi, si, 0, 0)),
                   out_spec(G_WIDTH)],
        out_shape=[jax.ShapeDtypeStruct((b, s, G_WIDTH), F32),
                   jax.ShapeDtypeStruct((b, s, PB_COLS), F32),
                   jax.ShapeDtypeStruct((b, s, G_WIDTH), F32),
                   jax.ShapeDtypeStruct((b, s, LANES), F32),
                   jax.ShapeDtypeStruct((b, s // NSA_CMP_STRIDE, NSA_CMP_STRIDE * LANES), F32),
                   jax.ShapeDtypeStruct((b, s, 2 * LANES), BF16),
                   jax.ShapeDtypeStruct((b, s // KEY_STEP, LANES, KEY_STEP), BF16),
                   jax.ShapeDtypeStruct((b, s, G_WIDTH), F32)],
        scratch_shapes=[pltpu.VMEM((MIX_ROWS + POOL_HALO, G_WIDTH), F32),
                        pltpu.VMEM((MIX_ROWS + CONV_HALO, G_WIDTH), F32),
                        pltpu.VMEM((MIX_ROWS, LANES), F32)],
        compiler_params=_params(2),
    )(x, g.reshape(1, d), w, poolw.astype(BF16), pool_scale.reshape(1, G_WIDTH), convw, kn)


def _block_diag(x, mask01):
    return jnp.concatenate([x] * N_HEADS, axis=0) * mask01


def _rwkv_kernel(pb_ref, mu_ref, w0_ref, wup_ref, a0_ref, aup_ref, gup_ref, kk_ref, ka_ref,
                 rk_ref, lnw_ref, lnb_ref, o_ref,
                 ext, state, r_s, k_s, v_s, lw_s, a_s, b_s, y_s):
    si = pl.program_id(1)
    group, seq_rows = pb_ref.shape[0], pb_ref.shape[1]
    g_w = G_WIDTH
    chunk = RWKV_CHUNK

    @pl.when(si == 0)
    def _():
        ext[:, 0:8, :] = jnp.zeros((group, 8, PB_COLS), F32)
        state[...] = jnp.zeros_like(state)

    shifted = []
    for b in range(group):
        p_b = pb_ref[b]
        ext[b, 8:, :] = p_b
        prev = ext[b, 7:7 + seq_rows, :]
        shifted.append(p_b + mu_ref[...] * (prev - p_b))
        ext[b, 0:8, :] = ext[b, seq_rows:seq_rows + 8, :]
    ps = jnp.concatenate(shifted, axis=0)

    head_mask = _head_ones()
    head_ones = head_mask.astype(BF16)

    def head_sum(t):
        return _dot2_exact_rhs(t, head_ones)

    r = ps[:, 0:g_w]
    k = ps[:, g_w:2 * g_w]
    v = ps[:, 2 * g_w:3 * g_w]
    tail = ps[:, 3 * g_w:4 * g_w]
    w = w0_ref[...] + _dot(jnp.tanh(tail), wup_ref[...])
    lw = -jnp.exp(-0.5) * _sigmoid(w)
    a = _sigmoid(a0_ref[...] + _dot(tail, aup_ref[...]))
    gate = _dot(_sigmoid(tail), gup_ref[...])
    kk = k * kk_ref[...]
    kk = kk * lax.rsqrt(jnp.maximum(head_sum(kk * kk), 1e-24))
    k2 = k * (1.0 + (a - 1.0) * ka_ref[...])
    bonus = head_sum(r * k2 * rk_ref[...]) * v
    r_s[...] = r
    k_s[...] = k2
    v_s[...] = v
    lw_s[...] = lw
    a_s[...] = -kk
    b_s[...] = kk * a

    row_i = lax.broadcasted_iota(jnp.int32, (chunk, g_w), 0)
    col_j = lax.broadcasted_iota(jnp.int32, (chunk, g_w), 1) % chunk
    strict_lower = row_i > col_j
    lower = row_i >= col_j
    eye = jnp.where(row_i == col_j, 1.0, 0.0)
    tri = (lax.broadcasted_iota(jnp.int32, (chunk, chunk), 0)
           >= lax.broadcasted_iota(jnp.int32, (chunk, chunk), 1)).astype(BF16)
    n_doublings = chunk.bit_length() - 1

    head_mask16 = head_ones

    def bd(x):
        return _block_diag(x.astype(BF16), head_mask16)

    def mm(a, b16):
        return jnp.dot(a.astype(BF16), b16, preferred_element_type=F32)

    ids = range(group)

    def group_local(c):
        sls = [pl.ds(pl.multiple_of(i * seq_rows + c * chunk, chunk), chunk) for i in ids]
        r_c, k_c, v_c, lw_c, a_c, b_c = ([t[sl, :] for sl in sls]
                                         for t in (r_s, k_s, v_s, lw_s, a_s, b_s))
        cw = [_dot2_exact_lhs(tri, lw_c[i]) for i in ids]
        cw_last = [cw[i][chunk - 1:chunk, :] for i in ids]
        a_t = [a_c[i] * jnp.exp(cw[i] - lw_c[i]) for i in ids]
        r_t = [r_c[i] * jnp.exp(cw[i]) for i in ids]
        e_inv = [jnp.exp(-cw[i]) for i in ids]
        bd_v = [bd(v_c[i]) for i in ids]
        pair = [lax.dot_general(
            jnp.concatenate([a_t[i], r_t[i]], axis=0).astype(BF16),
            jnp.concatenate([bd(b_c[i] * e_inv[i]), bd(k_c[i] * e_inv[i])], axis=0),
            (((1,), (1,)), ((), ())), preferred_element_type=F32) for i in ids]
        l_ak = [jnp.where(strict_lower, pair[i][0:chunk, g_w:2 * g_w], 0.0) for i in ids]
        m_rb = [jnp.where(lower, pair[i][chunk:2 * chunk, 0:g_w], 0.0) for i in ids]
        m_rk = [jnp.where(lower, pair[i][chunk:2 * chunk, g_w:2 * g_w], 0.0) for i in ids]

        with_v = [mm(jnp.concatenate([l_ak[i], m_rk[i]], axis=0), bd_v[i]) for i in ids]
        z0 = [with_v[i][0:chunk] for i in ids]
        l_pow = [jnp.where(strict_lower, pair[i][0:chunk, 0:g_w], 0.0) for i in ids]
        t_inv = [eye + l_pow[i] for i in ids]
        l_pow = [mm(l_pow[i], bd(l_pow[i])) for i in ids]
        for it in range(1, n_doublings):
            last = it == n_doublings - 1
            res = [mm(l_pow[i], jnp.concatenate(
                [bd(t_inv[i])] + ([] if last else [bd(l_pow[i])]), axis=1)) for i in ids]
            t_inv = [t_inv[i] + res[i][:, 0:g_w] for i in ids]
            if not last:
                l_pow = [res[i][:, g_w:2 * g_w] for i in ids]
        sol = [mm(t_inv[i], jnp.concatenate([bd(a_t[i]), bd(z0[i])], axis=1)) for i in ids]
        x1 = [sol[i][:, 0:g_w] for i in ids]
        x2 = [sol[i][:, g_w:2 * g_w] for i in ids]

        q = [mm(m_rb[i], jnp.concatenate([bd(x1[i]), bd(x2[i])], axis=1)) for i in ids]
        q2 = [q[i][:, g_w:2 * g_w] + with_v[i][chunk:2 * chunk] for i in ids]
        lhs = [jnp.concatenate([x1[i], r_t[i] + q[i][:, 0:g_w]], axis=0).astype(BF16) for i in ids]
        rhs = []
        for i in ids:
            e_fut = jnp.exp(cw_last[i] - cw[i])
            rhs.append(jnp.concatenate([b_c[i] * e_fut, k_c[i] * e_fut], axis=0).astype(BF16))
        decay = [jnp.exp(cw_last[i]) for i in ids]

        s0 = [state[i] for i in ids]
        us = [lax.dot_general(lhs[i], s0[i].astype(BF16), (((1,), (1,)), ((), ())),
                              preferred_element_type=F32) for i in ids]
        uv_t = []
        for i in ids:
            y_s[sls[i], :] = us[i][chunk:2 * chunk] + q2[i]
            u = us[i][0:chunk] + x2[i]
            uv_t.append(jnp.transpose(jnp.concatenate([u, v_c[i]], axis=0)).astype(BF16))
        upd = [jnp.dot(uv_t[i], rhs[i], preferred_element_type=F32) for i in ids]
        for i in ids:
            state[i] = s0[i] * decay[i] + jnp.where(head_mask, upd[i], 0.0)

    def chunk_step(c, carry):
        group_local(c)
        return carry

    lax.fori_loop(0, seq_rows // chunk, chunk_step, 0)

    y = y_s[...]
    inv_n = 1.0 / HEAD_DIM
    mean = head_sum(y) * inv_n
    dev = y - mean
    var = head_sum(dev * dev) * inv_n
    yn = dev * lax.rsqrt(var + RWKV_LN_EPS) * lnw_ref[...] + lnb_ref[...]
    out = (yn + bonus) * gate
    for b in range(group):
        o_ref[b] = out[b * seq_rows:(b + 1) * seq_rows, :]


def _rwkv_call(pb, mu, w0, w_up, a0, a_up, g_up, k_k, k_a, r_k, ln_w, ln_b):
    b, s, _ = pb.shape
    g_w = G_WIDTH
    row = lambda t: t.reshape(1, g_w)
    o_a = RWKV_W_RANK
    o_g = o_a + RWKV_A_RANK
    wup = jnp.zeros((g_w, g_w), F32).at[0:o_a].set(w_up).astype(BF16)
    aup = jnp.zeros((g_w, g_w), F32).at[o_a:o_g].set(a_up).astype(BF16)
    gup = jnp.zeros((g_w, g_w), F32).at[o_g:o_g + RWKV_G_RANK].set(g_up).astype(BF16)
    mu_p = _pad_cols(mu.reshape(1, -1), PB_COLS)
    vec = _const_spec((1, g_w))
    mat = _const_spec((g_w, g_w))
    group = min(RWKV_GROUP, b)
    seq_rows = MIX_ROWS // group
    seq = pltpu.VMEM((MIX_ROWS, g_w), F32)
    return pl.pallas_call(
        _rwkv_kernel,
        grid=(b // group, s // seq_rows),
        in_specs=[pl.BlockSpec((group, seq_rows, PB_COLS), lambda bi, si: (bi, si, 0)),
                  _const_spec((1, PB_COLS)), vec, mat, vec, mat, mat, vec, vec, vec, vec, vec],
        out_specs=pl.BlockSpec((group, seq_rows, g_w), lambda bi, si: (bi, si, 0)),
        out_shape=jax.ShapeDtypeStruct((b, s, g_w), F32),
        scratch_shapes=[pltpu.VMEM((group, seq_rows + 8, PB_COLS), F32),
                        pltpu.VMEM((group, g_w, g_w), F32),
                        seq, seq, seq, seq, seq, seq, seq],
        compiler_params=_params(2),
    )(pb, mu_p, row(w0), wup, row(a0), aup, gup, row(k_k), row(k_a), row(r_k), row(ln_w), row(ln_b))


def _gelu_tanh(x):
    return 0.5 * x * (1.0 + jnp.tanh(0.7978845608028654 * (x + 0.044715 * x * x * x)))


def _pos_features(pos, width):
    lane = lax.broadcasted_iota(jnp.int32, (pos.shape[0], width), 1) // ALIBI_PARTS
    feat = jnp.where(lane == 0, pos // ALIBI_SPLIT,
                     jnp.where(lane == 1, pos % ALIBI_SPLIT, jnp.where(lane == 2, 1, 0)))
    return feat.astype(F32).astype(BF16)


def _bf16_parts(v):
    parts = []
    for _ in range(ALIBI_PARTS):
        head = v.astype(BF16).astype(F32)
        parts.append(head)
        v = v - head
    return parts


def _nsa_compress_kernel(x_ref, pos_ref, w_first_ref, w_second_ref, kw2_ref, vw2t_ref, kn_ref,
                         kcmp_ref, vcmpt_ref):
    n_rows = x_ref.shape[1]
    hid = kw2_ref.shape[0]
    x = x_ref[0]
    first = _dot(x + pos_ref[0:1, :], w_first_ref[...])
    second = _dot(x + pos_ref[1:2, :], w_second_ref[...])
    hidden = _gelu_tanh(first + pltpu.roll(second, n_rows - 1, 0))
    k_cmp = _dot(hidden[:, 0:hid], kw2_ref[...])
    kcmp_ref[0, :, 0:HEAD_DIM] = _rms_rows(k_cmp, kn_ref[...]).astype(BF16)
    last_token = (lax.broadcasted_iota(jnp.int32, (n_rows, 1), 0) * NSA_CMP_STRIDE
                  + (NSA_CMP_LEN - 1))
    kcmp_ref[0, :, HEAD_DIM:2 * HEAD_DIM] = _pos_features(last_token, HEAD_DIM)
    vcmpt_ref[0] = _dot_nt(vw2t_ref[...], hidden[:, hid:2 * hid]).astype(BF16)


def _interleave_kv(k_part, v_part):
    zeros = jnp.zeros_like(k_part)
    top = jnp.concatenate([k_part, zeros], axis=2)
    bottom = jnp.concatenate([zeros, v_part], axis=2)
    return jnp.concatenate([top, bottom], axis=1).reshape(NSA_CMP_STRIDE * LANES, -1)


def _nsa_compress_call(kcvc, pos, kw1, kw2, vw1, vw2, k_norm0):
    b, n_chunks, width = kcvc.shape
    hid = kw1.shape[1]
    split = lambda w1, half: w1.reshape(2, NSA_CMP_STRIDE, HEAD_DIM, hid)[half]
    w_first = _interleave_kv(split(kw1, 0), split(vw1, 0)).astype(BF16)
    w_second = _interleave_kv(split(kw1, 1), split(vw1, 1)).astype(BF16)
    pos_halves = pos.reshape(2, NSA_CMP_STRIDE, HEAD_DIM)
    pos2 = jnp.concatenate([pos_halves, pos_halves], axis=2).reshape(2, width)
    pos2 = jnp.pad(pos2, ((0, 6), (0, 0)))
    return pl.pallas_call(
        _nsa_compress_kernel,
        grid=(b,),
        in_specs=[pl.BlockSpec((1, n_chunks, width), lambda bi: (bi, 0, 0)), _const_spec((8, width)),
                  _const_spec((width, 2 * hid)), _const_spec((width, 2 * hid)),
                  _const_spec((hid, HEAD_DIM)), _const_spec((HEAD_DIM, hid)),
                  _const_spec((1, HEAD_DIM))],
        out_specs=[pl.BlockSpec((1, n_chunks, 2 * HEAD_DIM), lambda bi: (bi, 0, 0)),
                   pl.BlockSpec((1, HEAD_DIM, n_chunks), lambda bi: (bi, 0, 0))],
        out_shape=[jax.ShapeDtypeStruct((b, n_chunks, 2 * HEAD_DIM), BF16),
                   jax.ShapeDtypeStruct((b, HEAD_DIM, n_chunks), BF16)],
        compiler_params=_params(1),
    )(kcvc, pos2, w_first, w_second, kw2.astype(BF16), vw2.T.astype(BF16),
      k_norm0.reshape(1, HEAD_DIM))


def _col_reduce(x, pair_op, reduce_fn, slab=32):
    parts = [x[i:i + slab] for i in range(0, x.shape[0], slab)]
    while len(parts) > 1:
        parts = [pair_op(parts[i], parts[i + 1]) for i in range(0, len(parts), 2)]
    return reduce_fn(parts[0], axis=0, keepdims=True)


def _nsa_attn_kernel(q_ref, gate_ref, qn_ref, kc_ref, vct_ref, kk_ref, vt_ref, o_ref, s_ref):
    qi = pl.program_id(1)
    tq = ATT_ROWS
    hw = N_HEADS * tq
    q0 = qi * tq
    n_cmp_rows = kc_ref.shape[1]
    n_sel = kk_ref.shape[1] // NSA_SEL_BLOCK

    head_ones = _head_ones().astype(BF16)
    q = q_ref[0]
    ms = _dot2_exact_rhs(q * q, head_ones) * (1.0 / HEAD_DIM)
    qn = q * lax.rsqrt(ms + RMS_EPS) * qn_ref[...] * (LOG2E * HEAD_DIM ** -0.5)
    qt = jnp.transpose(qn)
    q4t = jnp.concatenate([qt[h * HEAD_DIM:(h + 1) * HEAD_DIM, :] for h in range(N_HEADS)],
                          axis=1).astype(BF16)
    lane = lax.broadcasted_iota(jnp.int32, (1, hw), 1)
    slope = jnp.exp2(-2.0 * (lane // tq + 1).astype(F32))
    t_lane = lane % tq

    coeffs = (ALIBI_SPLIT * LOG2E * slope, LOG2E * slope, -LOG2E * slope * q0.astype(F32))
    f_row = lax.broadcasted_iota(jnp.int32, (HEAD_DIM, hw), 0)
    q_feat = jnp.zeros((HEAD_DIM, hw), F32)
    for ci, coeff in enumerate(coeffs):
        for pi, part in enumerate(_bf16_parts(coeff)):
            q_feat = jnp.where(f_row == ci * ALIBI_PARTS + pi, part, q_feat)
    q_feat = q_feat.astype(BF16)
    zero_half = jnp.zeros_like(q4t)
    q_cmp = jnp.concatenate([q4t, q_feat], axis=0)
    q_sel = jnp.concatenate([q4t, zero_half, q_feat, zero_half], axis=0)
    q_win = jnp.concatenate([zero_half, q4t, q_feat, zero_half], axis=0)

    n_idx = lax.broadcasted_iota(jnp.int32, (n_cmp_rows, hw), 0)
    valid_c = (q0 + t_lane) >= (n_idx * NSA_CMP_STRIDE + (NSA_CMP_LEN - 1))
    s_c = (jnp.dot(kc_ref[0], q_cmp, preferred_element_type=F32)
           + jnp.where(valid_c, 0.0, NSA_NEG))
    m_c = jnp.maximum(jnp.max(s_c, axis=0, keepdims=True), SCORE_FLOOR)
    p_c = jnp.exp2(s_c - m_c)
    l_c = jnp.sum(p_c, axis=0, keepdims=True)
    p_c = p_c * (1.0 / jnp.maximum(l_c, 1e-30))
    o_cmp = jnp.dot(vct_ref[0], p_c.astype(BF16), preferred_element_type=F32)

    p_heads = p_c[:, 0:tq]
    for h in range(1, N_HEADS):
        p_heads = p_heads + p_c[:, h * tq:(h + 1) * tq]
    per_sel = NSA_SEL_BLOCK // NSA_CMP_STRIDE
    j_ov = lax.broadcasted_iota(jnp.int32, (n_sel, n_cmp_rows), 0)
    n_ov = lax.broadcasted_iota(jnp.int32, (n_sel, n_cmp_rows), 1)
    overlap_t = ((n_ov >= per_sel * j_ov - (NSA_CMP_LEN // NSA_CMP_STRIDE - 1))
                 & (n_ov <= per_sel * j_ov + per_sel - 1)).astype(BF16)
    p_hi = p_heads.astype(BF16)
    p_rest = p_heads - p_hi.astype(F32)
    p_mid = p_rest.astype(BF16)
    p_lo = (p_rest - p_mid.astype(F32)).astype(BF16)
    imp = (jnp.dot(overlap_t, p_hi, preferred_element_type=F32)
           + jnp.dot(overlap_t, p_mid, preferred_element_type=F32)
           + jnp.dot(overlap_t, p_lo, preferred_element_type=F32))
    j_idx = lax.broadcasted_iota(jnp.int32, (n_sel, tq), 0)
    cur = (q0 + lax.broadcasted_iota(jnp.int32, (1, tq), 1)) // NSA_SEL_BLOCK
    forced = (j_idx == 0) | (j_idx == cur) | (j_idx == cur - 1)
    imp = jnp.where(j_idx <= cur, imp + jnp.where(forced, NSA_FORCE_BONUS, 0.0), -1.0)
    rank = jnp.zeros((n_sel, tq), F32)
    for jp in range(n_sel):
        other = imp[jp:jp + 1, :]
        ahead = (other > imp) | ((other == imp) & (j_idx > jp))
        rank = rank + jnp.where(ahead, 1.0, 0.0)
    chosen = jnp.where(rank < float(min(NSA_TOP_N, n_sel)), 0.0, NSA_NEG)

    key_i = lax.broadcasted_iota(jnp.int32, (tq, tq), 0)
    query_i = lax.broadcasted_iota(jnp.int32, (tq, tq), 1)
    causal = jnp.where(query_i >= key_i, 0.0, NSA_NEG)
    window_start = jnp.where(key_i > query_i, 0.0, NSA_NEG)
    tiles_per_step = KEY_STEP // tq
    blocks_per_tile = tq // NSA_SEL_BLOCK
    win_tiles = NSA_WINDOW // tq
    last_step = qi // tiles_per_step
    ones_rows = jnp.ones((16, KEY_STEP), BF16)

    def branch(n_steps, step_of, q_pad, v_rows, bias_fn):
        def prefetch(i, slot):
            p = step_of(i)
            k0 = pl.multiple_of(p * KEY_STEP, KEY_STEP)
            raw = jnp.dot(kk_ref[0, pl.ds(k0, KEY_STEP), :], q_pad, preferred_element_type=F32)
            s = raw + jnp.concatenate([bias_fn(p)] * N_HEADS, axis=1)
            s_ref[slot] = s
            return _col_reduce(s, jnp.maximum, jnp.max)

        def absorb(i, slot, s_max, m, l, acc):
            m_new = jnp.maximum(m, s_max)
            alpha = jnp.exp2(m - m_new)
            prob = jnp.exp2(s_ref[slot] - m_new).astype(BF16)
            v_t = jnp.concatenate([vt_ref[0, step_of(i)][v_rows[0]:v_rows[1], :], ones_rows], axis=0)
            pv = jnp.dot(v_t, prob, preferred_element_type=F32)
            l = alpha * l + pv[HEAD_DIM:HEAD_DIM + 1, :]
            return m_new, l, alpha * acc + pv[0:HEAD_DIM, :]

        def pair(i2, carry):
            max0, m, l, acc = carry
            i = 2 * i2
            max1 = prefetch(i + 1, 1)
            m, l, acc = absorb(i, 0, max0, m, l, acc)
            max0 = prefetch(i + 2, 0)
            m, l, acc = absorb(i + 1, 1, max1, m, l, acc)
            return max0, m, l, acc

        init = (prefetch(0, 0), jnp.full((1, hw), SCORE_FLOOR, F32), jnp.zeros((1, hw), F32),
                jnp.zeros((HEAD_DIM, hw), F32))
        n_pairs = (n_steps - 1) // 2
        max0, m, l, acc = lax.fori_loop(0, n_pairs, pair, init)
        i = 2 * n_pairs

        def two_left(m, l, acc):
            max1 = prefetch(i + 1, 1)
            m, l, acc = absorb(i, 0, max0, m, l, acc)
            return absorb(i + 1, 1, max1, m, l, acc)

        def one_left(m, l, acc):
            return absorb(i, 0, max0, m, l, acc)

        _, l, acc = lax.cond(n_steps - i == 2, two_left, one_left, m, l, acc)
        return acc * (1.0 / l)

    def sel_tile_bias(kt):
        rows = [jnp.max(jnp.where(j_idx == blocks_per_tile * kt + i, chosen, NSA_NEG),
                        axis=0, keepdims=True) for i in range(blocks_per_tile)]
        bias = rows[-1]
        for i in range(blocks_per_tile - 2, -1, -1):
            bias = jnp.where(key_i < (i + 1) * NSA_SEL_BLOCK, rows[i], bias)
        return bias

    def sel_step_bias(p):
        tiles = []
        for i in range(tiles_per_step):
            kt = p * tiles_per_step + i
            tiles.append(sel_tile_bias(kt) + jnp.where(kt == qi, causal, 0.0)
                         + jnp.where(kt > qi, NSA_NEG, 0.0))
        return jnp.concatenate(tiles, axis=0)

    def win_step_bias(p):
        tiles = []
        for i in range(tiles_per_step):
            behind = qi - (p * tiles_per_step + i)
            bias = jnp.where(behind == 0, causal, jnp.where(behind == win_tiles, window_start, 0.0))
            tiles.append(bias + jnp.where((behind >= 0) & (behind <= win_tiles), 0.0, NSA_NEG))
        return jnp.concatenate(tiles, axis=0)

    first_win_step = jnp.maximum(qi - win_tiles, 0) // tiles_per_step
    o_win = branch(last_step - first_win_step + 1, lambda i: first_win_step + i,
                   q_win, (HEAD_DIM, 2 * HEAD_DIM), win_step_bias)

    blocks_per_step = KEY_STEP // NSA_SEL_BLOCK
    block_id = lax.broadcasted_iota(jnp.int32, (n_sel, 1), 0)
    block_used = jnp.max(chosen, axis=1, keepdims=True) > SCORE_FLOOR
    earliest = jnp.min(jnp.where(block_used & (block_id >= blocks_per_step), block_id, n_sel),
                       axis=0, keepdims=True)[0, 0]
    first_sel_step = jnp.clip(earliest // blocks_per_step, 1, jnp.maximum(last_step, 1))
    n_sel_steps = jnp.where(last_step >= 1, last_step - first_sel_step + 2, 1)
    o_sel = branch(n_sel_steps, lambda i: jnp.where(i == 0, 0, first_sel_step + i - 1),
                   q_sel, (0, HEAD_DIM), sel_step_bias)

    g_t = jnp.transpose(_sigmoid(gate_ref[0]))

    def gate_row(c):
        return jnp.concatenate([g_t[3 * h + c:3 * h + c + 1, :] for h in range(N_HEADS)], axis=1)

    o = gate_row(0) * o_cmp + gate_row(1) * o_sel + gate_row(2) * o_win
    o_hd = jnp.concatenate([o[:, h * tq:(h + 1) * tq] for h in range(N_HEADS)], axis=0)
    o_ref[0] = jnp.transpose(o_hd)


def _nsa_attn_call(q, gates, q_norm, kcmp, vcmpt, kk, vt):
    b, s, _ = q.shape
    n_cmp_rows = kcmp.shape[1]
    qn = jnp.tile(q_norm, N_HEADS).reshape(1, G_WIDTH)
    return pl.pallas_call(
        _nsa_attn_kernel,
        grid=(b, s // ATT_ROWS),
        in_specs=[pl.BlockSpec((1, ATT_ROWS, G_WIDTH), lambda bi, qi: (bi, qi, 0)),
                  pl.BlockSpec((1, ATT_ROWS, LANES), lambda bi, qi: (bi, qi, 0)),
                  _const_spec((1, G_WIDTH)),
                  pl.BlockSpec((1, n_cmp_rows, 2 * HEAD_DIM), lambda bi, qi: (bi, 0, 0)),
                  pl.BlockSpec((1, HEAD_DIM, n_cmp_rows), lambda bi, qi: (bi, 0, 0)),
                  pl.BlockSpec((1, s, 2 * LANES), lambda bi, qi: (bi, 0, 0)),
                  pl.BlockSpec((1, s // KEY_STEP, LANES, KEY_STEP), lambda bi, qi: (bi, 0, 0, 0))],
        out_specs=pl.BlockSpec((1, ATT_ROWS, G_WIDTH), lambda bi, qi: (bi, qi, 0)),
        out_shape=jax.ShapeDtypeStruct((b, s, G_WIDTH), F32),
        scratch_shapes=[pltpu.VMEM((2, KEY_STEP, N_HEADS * ATT_ROWS), F32)],
        compiler_params=_params(2),
    )(q, gates, qn, kcmp, vcmpt, kk, vt)


def kernel(x, ffn1_norm, ffn1_w_gate, ffn1_w_up, ffn1_w_down, mix_norm, w_in, pool_w, pool_scale, rwkv_mu, rwkv_w0, rwkv_w_up, rwkv_a0, rwkv_a_up, rwkv_g_up, rwkv_k_k, rwkv_k_a, rwkv_r_k, rwkv_ln_w, rwkv_ln_b, nsa_q_norm, nsa_k_norm, nsa_cmp_pos, nsa_cmp_k_w1, nsa_cmp_k_w2, nsa_cmp_v_w1, nsa_cmp_v_w2, conv_w, w_out, ffn2_norm, ffn2_w_gate, ffn2_w_up, ffn2_w_down):
    b, s, d = x.shape
    n_tok = b * s
    x2d = x.reshape(n_tok, d)
    for l in range(ffn1_norm.shape[0]):
        x2d = _ffn_call(x2d, ffn1_norm[l], ffn1_w_gate[l], ffn1_w_up[l], ffn1_w_down[l])
        ya, pb, q, gates, kcvc, kk, vt, yd = _mixin_call(
            x2d.reshape(b, s, d), mix_norm[l], w_in[l], pool_w[l], pool_scale[l], conv_w[l],
            nsa_k_norm[l][1], nsa_k_norm[l][2])
        yb = _rwkv_call(pb, rwkv_mu[l], rwkv_w0[l], rwkv_w_up[l], rwkv_a0[l], rwkv_a_up[l],
                        rwkv_g_up[l], rwkv_k_k[l], rwkv_k_a[l], rwkv_r_k[l], rwkv_ln_w[l],
                        rwkv_ln_b[l])
        kcmp, vcmpt = _nsa_compress_call(kcvc, nsa_cmp_pos[l], nsa_cmp_k_w1[l], nsa_cmp_k_w2[l],
                                         nsa_cmp_v_w1[l], nsa_cmp_v_w2[l], nsa_k_norm[l][0])
        yc = _nsa_attn_call(q, gates, nsa_q_norm[l], kcmp, vcmpt, kk, vt)
        mix = tuple(t.reshape(n_tok, G_WIDTH) for t in (ya, yb, yc, yd))
        x2d = _ffn_call(x2d, ffn2_norm[l], ffn2_w_gate[l], ffn2_w_up[l], ffn2_w_down[l],
                        mix=mix, w_out=w_out[l])
    return x2d.reshape(b, s, d)
```

```python
import functools

import jax
import jax.numpy as jnp
from jax import lax
from jax.experimental import pallas as pl
from jax.experimental.pallas import tpu as pltpu

F32 = jnp.float32
BF16 = jnp.bfloat16

N_MIXERS = 4
HEAD_DIM = 64
N_HEADS = 4
G_WIDTH = N_HEADS * HEAD_DIM
RMS_EPS = 1e-6
POOL_WINDOWS = (2, 4, 8, 16)
POOL_HALO = 16
CONV_HALO = 8
RWKV_W_RANK, RWKV_A_RANK, RWKV_G_RANK = 64, 32, 64
RWKV_LN_EPS = 64e-5
RWKV_CHUNK = 64
RWKV_GROUP = 8
NSA_CMP_LEN = 32
NSA_CMP_STRIDE = 16
NSA_SEL_BLOCK = 64
NSA_TOP_N = 16
NSA_WINDOW = 512
NSA_FORCE_BONUS = 1e4
NSA_NEG = -1e9
SCORE_FLOOR = 0.5 * NSA_NEG
ALIBI_SPLIT = 64
ALIBI_PARTS = 3
LOG2E = 1.4426950408889634

PB_COLS = 1024
PC_COLS = 768
PD_COLS = 768
P_PAD = G_WIDTH + PB_COLS + PC_COLS + PD_COLS

V7X_VMEM_BYTES = 64 * 1024 * 1024
VMEM_LIMIT = V7X_VMEM_BYTES - 8 * 1024 * 1024
LANES = 128
SUBLANES = 8

FFN_ROWS = 512
FFN_COLS = 256
MIX_ROWS = 512
ATT_ROWS = 256
KEY_STEP = 256


def _params(n_axes):
    return pltpu.CompilerParams(dimension_semantics=("arbitrary",) * n_axes,
                                vmem_limit_bytes=VMEM_LIMIT)


def _const_spec(shape):
    nd = len(shape)
    return pl.BlockSpec(shape, lambda *_: (0,) * nd, pipeline_mode=pl.Buffered(1))


def _dot(a, b):
    return jnp.dot(a.astype(BF16), b.astype(BF16), preferred_element_type=F32)


def _dot_nt(a, b):
    return lax.dot_general(a.astype(BF16), b.astype(BF16), (((1,), (1,)), ((), ())),
                           preferred_element_type=F32)


def _split(x):
    hi = x.astype(BF16)
    lo = (x - hi.astype(F32)).astype(BF16)
    return hi, lo


def _dot2_exact_rhs(a, b_bf16):
    ah, al = _split(a)
    return (jnp.dot(ah, b_bf16, preferred_element_type=F32)
            + jnp.dot(al, b_bf16, preferred_element_type=F32))


def _dot2_exact_lhs(a_bf16, b):
    bh, bl = _split(b)
    return (jnp.dot(a_bf16, bh, preferred_element_type=F32)
            + jnp.dot(a_bf16, bl, preferred_element_type=F32))


def _rms_rows(x, g):
    return x * lax.rsqrt(jnp.mean(x * x, axis=-1, keepdims=True) + RMS_EPS) * g


def _sigmoid(x):
    return 1.0 / (1.0 + jnp.exp(-x))


def _head_ones():
    r = lax.broadcasted_iota(jnp.int32, (G_WIDTH, G_WIDTH), 0) // HEAD_DIM
    c = lax.broadcasted_iota(jnp.int32, (G_WIDTH, G_WIDTH), 1) // HEAD_DIM
    return r == c


def _ffn_body(x_in, g_ref, wg_ref, wu_ref, wd_ref, o_ref, acc_ref):
    h = _rms_rows(x_in, g_ref[...]).astype(BF16)
    for c in range(wg_ref.shape[1] // FFN_COLS):
        cols = slice(c * FFN_COLS, (c + 1) * FFN_COLS)
        gate = jnp.dot(h, wg_ref[:, cols], preferred_element_type=F32)
        up = jnp.dot(h, wu_ref[:, cols], preferred_element_type=F32)
        act = (gate * _sigmoid(gate) * up).astype(BF16)
        down = jnp.dot(act, wd_ref[cols, :], preferred_element_type=F32)
        if c == 0:
            acc_ref[...] = down
        else:
            acc_ref[...] += down
    o_ref[...] = x_in + 0.5 * acc_ref[...]


def _ffn_kernel(x_ref, g_ref, wg_ref, wu_ref, wd_ref, o_ref, acc_ref):
    _ffn_body(x_ref[...], g_ref, wg_ref, wu_ref, wd_ref, o_ref, acc_ref)


def _out_ffn_kernel(x_ref, ya_ref, yb_ref, yc_ref, yd_ref, wo_ref,
                    g_ref, wg_ref, wu_ref, wd_ref, o_ref, acc_ref):
    x1 = x_ref[...]
    for i, y_ref in enumerate((ya_ref, yb_ref, yc_ref, yd_ref)):
        x1 = x1 + _dot(y_ref[...], wo_ref[i])
    _ffn_body(x1, g_ref, wg_ref, wu_ref, wd_ref, o_ref, acc_ref)


def _layer_spec(stacked, layer):
    shape = stacked.shape[1:]
    nd = len(shape)
    return pl.BlockSpec((None,) + shape, lambda *_: (layer,) + (0,) * nd,
                        pipeline_mode=pl.Buffered(1))


def _ffn_call(x2d, g, wg, wu, wd, layer, mix=None, wo=None):
    n_tok, d = x2d.shape
    row_spec = pl.BlockSpec((FFN_ROWS, d), lambda i: (i, 0))
    w_specs = [_const_spec((1, d)), _layer_spec(wg, layer), _layer_spec(wu, layer),
               _layer_spec(wd, layer)]
    w_args = [g.reshape(1, d), wg, wu, wd]
    if mix is None:
        kern, in_specs, args = _ffn_kernel, [row_spec] + w_specs, [x2d] + w_args
    else:
        y_spec = pl.BlockSpec((FFN_ROWS, G_WIDTH), lambda i: (i, 0))
        kern = _out_ffn_kernel
        in_specs = [row_spec] + [y_spec] * N_MIXERS + [_layer_spec(wo, layer)] + w_specs
        args = [x2d] + list(mix) + [wo] + w_args
    return pl.pallas_call(
        kern,
        grid=(n_tok // FFN_ROWS,),
        in_specs=in_specs,
        out_specs=row_spec,
        out_shape=jax.ShapeDtypeStruct((n_tok, d), F32),
        scratch_shapes=[pltpu.VMEM((FFN_ROWS, d), F32)],
        compiler_params=_params(1),
    )(*args)


def _mixin_kernel(x_ref, g_ref, w_ref, poolw_ref, pools_ref, convw_ref, kn_ref,
                  ya_ref, pb_ref, q_ref, gate_ref, kcvc_ref, kk_ref, vt_ref, yd_ref,
                  pa_ext, z_ext, kcvc_s):
    si = pl.program_id(1)
    rows = x_ref.shape[1]

    @pl.when(si == 0)
    def _():
        pa_ext[0:POOL_HALO, :] = jnp.zeros((POOL_HALO, G_WIDTH), F32)
        z_ext[0:CONV_HALO, :] = jnp.zeros((CONV_HALO, G_WIDTH), F32)

    c0, c1, c2 = G_WIDTH, G_WIDTH + PB_COLS, G_WIDTH + PB_COLS + PC_COLS
    half = rows // 2
    u_parts, pd_parts, pc_parts = [], [], []
    for r0 in (0, half):
        h = _rms_rows(x_ref[0, r0:r0 + half, :], g_ref[...]).astype(BF16)
        u_parts.append(jnp.dot(h, w_ref[:, 0:c0], preferred_element_type=F32))
        pd_parts.append(jnp.dot(h, w_ref[:, c2:c2 + PD_COLS], preferred_element_type=F32))
        pb_ref[0, r0:r0 + half, :] = jnp.dot(h, w_ref[:, c0:c1], preferred_element_type=F32)
        pc_parts.append(jnp.dot(h, w_ref[:, c1:c2], preferred_element_type=F32))
    u = jnp.concatenate(u_parts, axis=0)
    pd = jnp.concatenate(pd_parts, axis=0)
    pc = jnp.concatenate(pc_parts, axis=0)

    q_ref[0] = pc[:, 0:G_WIDTH]
    gate_ref[0] = pc[:, 5 * LANES:6 * LANES]
    kcvc_s[...] = pc[:, 2 * LANES:3 * LANES]
    for t in range(NSA_CMP_STRIDE):
        kcvc_ref[0, :, t * LANES:(t + 1) * LANES] = kcvc_s[pl.ds(t, rows // NSA_CMP_STRIDE,
                                                                stride=NSA_CMP_STRIDE), :]
    sel_kv = pc[:, 3 * LANES:4 * LANES]
    win_kv = pc[:, 4 * LANES:5 * LANES]
    first = lax.broadcasted_iota(jnp.int32, sel_kv.shape, 1) < HEAD_DIM
    kx = jnp.where(first, sel_kv, pltpu.roll(win_kv, HEAD_DIM, 1))
    vx = jnp.where(first, pltpu.roll(sel_kv, HEAD_DIM, 1), win_kv)
    sq = kx * kx
    ms_first = jnp.sum(jnp.where(first, sq, 0.0), axis=-1, keepdims=True) * (1.0 / HEAD_DIM)
    ms_second = jnp.sum(jnp.where(first, 0.0, sq), axis=-1, keepdims=True) * (1.0 / HEAD_DIM)
    inv = jnp.where(first, lax.rsqrt(ms_first + RMS_EPS), lax.rsqrt(ms_second + RMS_EPS))
    kk_ref[0, :, 0:LANES] = (kx * inv * kn_ref[...]).astype(BF16)
    key_pos = si * rows + lax.broadcasted_iota(jnp.int32, (rows, 1), 0)
    kk_ref[0, :, LANES:2 * LANES] = _pos_features(key_pos, LANES)
    for j in range(vt_ref.shape[1]):
        vt_ref[0, j] = jnp.transpose(vx[j * KEY_STEP:(j + 1) * KEY_STEP, :]).astype(BF16)

    pa_ext[POOL_HALO:, :] = u
    lane_group = lax.broadcasted_iota(jnp.int32, (rows, G_WIDTH), 1) // (G_WIDTH // len(POOL_WINDOWS))
    pos = si * rows + lax.broadcasted_iota(jnp.int32, (rows, G_WIDTH), 0)
    total = u
    for k in range(1, max(POOL_WINDOWS)):
        first_group = sum(1 for w in POOL_WINDOWS if w <= k)
        shifted = pa_ext[POOL_HALO - k:POOL_HALO - k + rows, :]
        total = total + jnp.where(lane_group >= first_group, shifted, 0.0)
    window = jnp.left_shift(2, lane_group)
    cnt = jnp.minimum(pos + 1, window).astype(F32)
    pooled = total / cnt
    ya = _dot(pooled - u, poolw_ref[...]) * pools_ref[...]
    ya_ref[0] = ya
    pa_ext[0:POOL_HALO, :] = pa_ext[rows:rows + POOL_HALO, :]

    cu = pd[:, 0:G_WIDTH]
    cb = pd[:, G_WIDTH:2 * G_WIDTH]
    cc = pd[:, 2 * G_WIDTH:3 * G_WIDTH]
    z = cc * cu
    z_ext[CONV_HALO:, :] = z
    y = (convw_ref[0:1, :] * z_ext[CONV_HALO - 2:CONV_HALO - 2 + rows, :]
         + convw_ref[1:2, :] * z_ext[CONV_HALO - 1:CONV_HALO - 1 + rows, :]
         + convw_ref[2:3, :] * z)
    yd_ref[0] = cb * y
    z_ext[0:CONV_HALO, :] = z_ext[rows:rows + CONV_HALO, :]


def _pad_cols(w, width):
    return jnp.pad(w, ((0, 0), (0, width - w.shape[1])))


def _w_in_padded(w_in):
    w = w_in.astype(BF16)
    rwkv_end = G_WIDTH + 3 * G_WIDTH + RWKV_W_RANK + RWKV_A_RANK + RWKV_G_RANK
    nsa_end = rwkv_end + G_WIDTH + 6 * HEAD_DIM + 3 * N_HEADS
    zeros = lambda n: jnp.zeros((w.shape[0], n), BF16)
    return jnp.concatenate(
        [w[:, 0:rwkv_end], zeros(G_WIDTH + PB_COLS - rwkv_end),
         w[:, rwkv_end:nsa_end], zeros(PC_COLS - (nsa_end - rwkv_end)),
         w[:, nsa_end:]], axis=1)


def _mixin_call(x, g, w_in, pool_w, pool_scale, conv_w, k_norm_sel, k_norm_win):
    b, s, d = x.shape
    w = _w_in_padded(w_in)
    kn = jnp.concatenate([k_norm_sel, k_norm_win]).reshape(1, LANES)
    cmp_rows = MIX_ROWS // NSA_CMP_STRIDE
    key_steps = MIX_ROWS // KEY_STEP
    n_groups = len(POOL_WINDOWS)
    pool_ch = G_WIDTH // n_groups
    poolw = jnp.zeros((G_WIDTH, G_WIDTH), F32)
    for gi in range(n_groups):
        poolw = poolw.at[gi * pool_ch:(gi + 1) * pool_ch, gi * pool_ch:(gi + 1) * pool_ch].set(pool_w[gi])
    convw = jnp.pad(conv_w, ((0, 8 - conv_w.shape[0]), (0, 0)))

    def out_spec(c):
        return pl.BlockSpec((1, MIX_ROWS, c), lambda bi, si: (bi, si, 0))

    return pl.pallas_call(
        _mixin_kernel,
        grid=(b, s // MIX_ROWS),
        in_specs=[pl.BlockSpec((1, MIX_ROWS, d), lambda bi, si: (bi, si, 0)),
                  _const_spec((1, d)), _const_spec(w.shape),
                  _const_spec((G_WIDTH, G_WIDTH)), _const_spec((1, G_WIDTH)),
                  _const_spec((8, G_WIDTH)), _const_spec((1, LANES))],
        out_specs=[out_spec(G_WIDTH), out_spec(PB_COLS), out_spec(G_WIDTH), out_spec(LANES),
                   pl.BlockSpec((1, cmp_rows, NSA_CMP_STRIDE * LANES), lambda bi, si: (bi, si, 0)),
                   out_spec(2 * LANES),
                   pl.BlockSpec((1, key_steps, LANES, KEY_STEP), lambda bi, si: (bi, si, 0, 0)),
                   out_spec(G_WIDTH)],
        out_shape=[jax.ShapeDtypeStruct((b, s, G_WIDTH), F32),
                   jax.ShapeDtypeStruct((b, s, PB_COLS), F32),
                   jax.ShapeDtypeStruct((b, s, G_WIDTH), F32),
                   jax.ShapeDtypeStruct((b, s, LANES), F32),
                   jax.ShapeDtypeStruct((b, s // NSA_CMP_STRIDE, NSA_CMP_STRIDE * LANES), F32),
                   jax.ShapeDtypeStruct((b, s, 2 * LANES), BF16),
                   jax.ShapeDtypeStruct((b, s // KEY_STEP, LANES, KEY_STEP), BF16),
                   jax.ShapeDtypeStruct((b, s, G_WIDTH), F32)],
        scratch_shapes=[pltpu.VMEM((MIX_ROWS + POOL_HALO, G_WIDTH), F32),
                        pltpu.VMEM((MIX_ROWS + CONV_HALO, G_WIDTH), F32),
                        pltpu.VMEM((MIX_ROWS, LANES), F32)],
        compiler_params=_params(2),
    )(x, g.reshape(1, d), w, poolw.astype(BF16), pool_scale.reshape(1, G_WIDTH), convw, kn)


def _block_diag(x, mask01):
    return jnp.concatenate([x] * N_HEADS, axis=0) * mask01


def _rwkv_kernel(pb_ref, mu_ref, w0_ref, wup_ref, a0_ref, aup_ref, gup_ref, kk_ref, ka_ref,
                 rk_ref, lnw_ref, lnb_ref, o_ref,
                 ext, state, r_s, k_s, v_s, lw_s, a_s, b_s, y_s):
    si = pl.program_id(1)
    group, seq_rows = pb_ref.shape[0], pb_ref.shape[1]
    g_w = G_WIDTH
    chunk = RWKV_CHUNK

    @pl.when(si == 0)
    def _():
        ext[:, 0:8, :] = jnp.zeros((group, 8, PB_COLS), F32)
        state[...] = jnp.zeros_like(state)

    shifted = []
    for b in range(group):
        p_b = pb_ref[b]
        ext[b, 8:, :] = p_b
        prev = ext[b, 7:7 + seq_rows, :]
        shifted.append(p_b + mu_ref[...] * (prev - p_b))
        ext[b, 0:8, :] = ext[b, seq_rows:seq_rows + 8, :]
    ps = jnp.concatenate(shifted, axis=0)

    head_mask = _head_ones()
    head_ones = head_mask.astype(BF16)

    def head_sum(t):
        return _dot2_exact_rhs(t, head_ones)

    r = ps[:, 0:g_w]
    k = ps[:, g_w:2 * g_w]
    v = ps[:, 2 * g_w:3 * g_w]
    tail = ps[:, 3 * g_w:4 * g_w]
    w = w0_ref[...] + _dot(jnp.tanh(tail), wup_ref[...])
    lw = -jnp.exp(-0.5) * _sigmoid(w)
    a = _sigmoid(a0_ref[...] + _dot(tail, aup_ref[...]))
    gate = _dot(_sigmoid(tail), gup_ref[...])
    kk = k * kk_ref[...]
    kk = kk * lax.rsqrt(jnp.maximum(head_sum(kk * kk), 1e-24))
    k2 = k * (1.0 + (a - 1.0) * ka_ref[...])
    bonus = head_sum(r * k2 * rk_ref[...]) * v
    r_s[...] = r
    k_s[...] = k2
    v_s[...] = v
    lw_s[...] = lw
    a_s[...] = -kk
    b_s[...] = kk * a

    row_i = lax.broadcasted_iota(jnp.int32, (chunk, g_w), 0)
    col_j = lax.broadcasted_iota(jnp.int32, (chunk, g_w), 1) % chunk
    strict_lower = row_i > col_j
    lower = row_i >= col_j
    eye = jnp.where(row_i == col_j, 1.0, 0.0)
    tri = (lax.broadcasted_iota(jnp.int32, (chunk, chunk), 0)
           >= lax.broadcasted_iota(jnp.int32, (chunk, chunk), 1)).astype(BF16)
    n_doublings = chunk.bit_length() - 1

    head_mask16 = head_ones

    def bd(x):
        return _block_diag(x.astype(BF16), head_mask16)

    def mm(a, b16):
        return jnp.dot(a.astype(BF16), b16, preferred_element_type=F32)

    ids = range(group)

    def group_local(c):
        sls = [pl.ds(pl.multiple_of(i * seq_rows + c * chunk, chunk), chunk) for i in ids]
        r_c, k_c, v_c, lw_c, a_c, b_c = ([t[sl, :] for sl in sls]
                                         for t in (r_s, k_s, v_s, lw_s, a_s, b_s))
        cw = [_dot2_exact_lhs(tri, lw_c[i]) for i in ids]
        cw_last = [cw[i][chunk - 1:chunk, :] for i in ids]
        a_t = [a_c[i] * jnp.exp(cw[i] - lw_c[i]) for i in ids]
        r_t = [r_c[i] * jnp.exp(cw[i]) for i in ids]
        e_inv = [jnp.exp(-cw[i]) for i in ids]
        bd_v = [bd(v_c[i]) for i in ids]
        pair = [lax.dot_general(
            jnp.concatenate([a_t[i], r_t[i]], axis=0).astype(BF16),
            jnp.concatenate([bd(b_c[i] * e_inv[i]), bd(k_c[i] * e_inv[i])], axis=0),
            (((1,), (1,)), ((), ())), preferred_element_type=F32) for i in ids]
        l_ak = [jnp.where(strict_lower, pair[i][0:chunk, g_w:2 * g_w], 0.0) for i in ids]
        m_rb = [jnp.where(lower, pair[i][chunk:2 * chunk, 0:g_w], 0.0) for i in ids]
        m_rk = [jnp.where(lower, pair[i][chunk:2 * chunk, g_w:2 * g_w], 0.0) for i in ids]

        with_v = [mm(jnp.concatenate([l_ak[i], m_rk[i]], axis=0), bd_v[i]) for i in ids]
        z0 = [with_v[i][0:chunk] for i in ids]
        l_pow = [jnp.where(strict_lower, pair[i][0:chunk, 0:g_w], 0.0) for i in ids]
        t_inv = [eye + l_pow[i] for i in ids]
        l_pow = [mm(l_pow[i], bd(l_pow[i])) for i in ids]
        for it in range(1, n_doublings):
            last = it == n_doublings - 1
            res = [mm(l_pow[i], jnp.concatenate(
                [bd(t_inv[i])] + ([] if last else [bd(l_pow[i])]), axis=1)) for i in ids]
            t_inv = [t_inv[i] + res[i][:, 0:g_w] for i in ids]
            if not last:
                l_pow = [res[i][:, g_w:2 * g_w] for i in ids]
        sol = [mm(t_inv[i], jnp.concatenate([bd(a_t[i]), bd(z0[i])], axis=1)) for i in ids]
        x1 = [sol[i][:, 0:g_w] for i in ids]
        x2 = [sol[i][:, g_w:2 * g_w] for i in ids]

        q = [mm(m_rb[i], jnp.concatenate([bd(x1[i]), bd(x2[i])], axis=1)) for i in ids]
        q2 = [q[i][:, g_w:2 * g_w] + with_v[i][chunk:2 * chunk] for i in ids]
        lhs = [jnp.concatenate([x1[i], r_t[i] + q[i][:, 0:g_w]], axis=0).astype(BF16) for i in ids]
        rhs = []
        for i in ids:
            e_fut = jnp.exp(cw_last[i] - cw[i])
            rhs.append(jnp.concatenate([b_c[i] * e_fut, k_c[i] * e_fut], axis=0).astype(BF16))
        decay = [jnp.exp(cw_last[i]) for i in ids]

        s0 = [state[i] for i in ids]
        us = [lax.dot_general(lhs[i], s0[i].astype(BF16), (((1,), (1,)), ((), ())),
                              preferred_element_type=F32) for i in ids]
        uv_t = []
        for i in ids:
            y_s[sls[i], :] = us[i][chunk:2 * chunk] + q2[i]
            u = us[i][0:chunk] + x2[i]
            uv_t.append(jnp.transpose(jnp.concatenate([u, v_c[i]], axis=0)).astype(BF16))
        upd = [jnp.dot(uv_t[i], rhs[i], preferred_element_type=F32) for i in ids]
        for i in ids:
            state[i] = s0[i] * decay[i] + jnp.where(head_mask, upd[i], 0.0)

    def chunk_step(c, carry):
        group_local(c)
        return carry

    lax.fori_loop(0, seq_rows // chunk, chunk_step, 0)

    y = y_s[...]
    inv_n = 1.0 / HEAD_DIM
    mean = head_sum(y) * inv_n
    dev = y - mean
    var = head_sum(dev * dev) * inv_n
    yn = dev * lax.rsqrt(var + RWKV_LN_EPS) * lnw_ref[...] + lnb_ref[...]
    out = (yn + bonus) * gate
    for b in range(group):
        o_ref[b] = out[b * seq_rows:(b + 1) * seq_rows, :]


def _rwkv_call(pb, mu, w0, w_up, a0, a_up, g_up, k_k, k_a, r_k, ln_w, ln_b):
    b, s, _ = pb.shape
    g_w = G_WIDTH
    row = lambda t: t.reshape(1, g_w)
    o_a = RWKV_W_RANK
    o_g = o_a + RWKV_A_RANK
    wup = jnp.zeros((g_w, g_w), F32).at[0:o_a].set(w_up).astype(BF16)
    aup = jnp.zeros((g_w, g_w), F32).at[o_a:o_g].set(a_up).astype(BF16)
    gup = jnp.zeros((g_w, g_w), F32).at[o_g:o_g + RWKV_G_RANK].set(g_up).astype(BF16)
    mu_p = _pad_cols(mu.reshape(1, -1), PB_COLS)
    vec = _const_spec((1, g_w))
    mat = _const_spec((g_w, g_w))
    group = min(RWKV_GROUP, b)
    seq_rows = MIX_ROWS // group
    seq = pltpu.VMEM((MIX_ROWS, g_w), F32)
    return pl.pallas_call(
        _rwkv_kernel,
        grid=(b // group, s // seq_rows),
        in_specs=[pl.BlockSpec((group, seq_rows, PB_COLS), lambda bi, si: (bi, si, 0)),
                  _const_spec((1, PB_COLS)), vec, mat, vec, mat, mat, vec, vec, vec, vec, vec],
        out_specs=pl.BlockSpec((group, seq_rows, g_w), lambda bi, si: (bi, si, 0)),
        out_shape=jax.ShapeDtypeStruct((b, s, g_w), F32),
        scratch_shapes=[pltpu.VMEM((group, seq_rows + 8, PB_COLS), F32),
                        pltpu.VMEM((group, g_w, g_w), F32),
                        seq, seq, seq, seq, seq, seq, seq],
        compiler_params=_params(2),
    )(pb, mu_p, row(w0), wup, row(a0), aup, gup, row(k_k), row(k_a), row(r_k), row(ln_w), row(ln_b))


def _gelu_tanh(x):
    return 0.5 * x * (1.0 + jnp.tanh(0.7978845608028654 * (x + 0.044715 * x * x * x)))


def _pos_features(pos, width):
    lane = lax.broadcasted_iota(jnp.int32, (pos.shape[0], width), 1) // ALIBI_PARTS
    feat = jnp.where(lane == 0, pos // ALIBI_SPLIT,
                     jnp.where(lane == 1, pos % ALIBI_SPLIT, jnp.where(lane == 2, 1, 0)))
    return feat.astype(F32).astype(BF16)


def _bf16_parts(v):
    parts = []
    for _ in range(ALIBI_PARTS):
        head = v.astype(BF16).astype(F32)
        parts.append(head)
        v = v - head
    return parts


def _nsa_compress_kernel(x_ref, pos_ref, w_first_ref, w_second_ref, kw2_ref, vw2t_ref, kn_ref,
                         kcmp_ref, vcmpt_ref):
    n_rows = x_ref.shape[1]
    hid = kw2_ref.shape[0]
    x = x_ref[0]
    first = _dot(x + pos_ref[0:1, :], w_first_ref[...])
    second = _dot(x + pos_ref[1:2, :], w_second_ref[...])
    hidden = _gelu_tanh(first + pltpu.roll(second, n_rows - 1, 0))
    k_cmp = _dot(hidden[:, 0:hid], kw2_ref[...])
    kcmp_ref[0, :, 0:HEAD_DIM] = _rms_rows(k_cmp, kn_ref[...]).astype(BF16)
    last_token = (lax.broadcasted_iota(jnp.int32, (n_rows, 1), 0) * NSA_CMP_STRIDE
                  + (NSA_CMP_LEN - 1))
    kcmp_ref[0, :, HEAD_DIM:2 * HEAD_DIM] = _pos_features(last_token, HEAD_DIM)
    vcmpt_ref[0] = _dot_nt(vw2t_ref[...], hidden[:, hid:2 * hid]).astype(BF16)


def _interleave_kv(k_part, v_part):
    zeros = jnp.zeros_like(k_part)
    top = jnp.concatenate([k_part, zeros], axis=2)
    bottom = jnp.concatenate([zeros, v_part], axis=2)
    return jnp.concatenate([top, bottom], axis=1).reshape(NSA_CMP_STRIDE * LANES, -1)


def _nsa_compress_call(kcvc, pos, kw1, kw2, vw1, vw2, k_norm0):
    b, n_chunks, width = kcvc.shape
    hid = kw1.shape[1]
    split = lambda w1, half: w1.reshape(2, NSA_CMP_STRIDE, HEAD_DIM, hid)[half]
    w_first = _interleave_kv(split(kw1, 0), split(vw1, 0)).astype(BF16)
    w_second = _interleave_kv(split(kw1, 1), split(vw1, 1)).astype(BF16)
    pos_halves = pos.reshape(2, NSA_CMP_STRIDE, HEAD_DIM)
    pos2 = jnp.concatenate([pos_halves, pos_halves], axis=2).reshape(2, width)
    pos2 = jnp.pad(pos2, ((0, 6), (0, 0)))
    return pl.pallas_call(
        _nsa_compress_kernel,
        grid=(b,),
        in_specs=[pl.BlockSpec((1, n_chunks, width), lambda bi: (bi, 0, 0)), _const_spec((8, width)),
                  _const_spec((width, 2 * hid)), _const_spec((width, 2 * hid)),
                  _const_spec((hid, HEAD_DIM)), _const_spec((HEAD_DIM, hid)),
                  _const_spec((1, HEAD_DIM))],
        out_specs=[pl.BlockSpec((1, n_chunks, 2 * HEAD_DIM), lambda bi: (bi, 0, 0)),
                   pl.BlockSpec((1, HEAD_DIM, n_chunks), lambda bi: (bi, 0, 0))],
        out_shape=[jax.ShapeDtypeStruct((b, n_chunks, 2 * HEAD_DIM), BF16),
                   jax.ShapeDtypeStruct((b, HEAD_DIM, n_chunks), BF16)],
        compiler_params=_params(1),
    )(kcvc, pos2, w_first, w_second, kw2.astype(BF16), vw2.T.astype(BF16),
      k_norm0.reshape(1, HEAD_DIM))


def _col_reduce(x, pair_op, reduce_fn, slab=32):
    parts = [x[i:i + slab] for i in range(0, x.shape[0], slab)]
    while len(parts) > 1:
        parts = [pair_op(parts[i], parts[i + 1]) for i in range(0, len(parts), 2)]
    return reduce_fn(parts[0], axis=0, keepdims=True)


def _nsa_attn_kernel(q_ref, gate_ref, qn_ref, kc_ref, vct_ref, kk_ref, vt_ref, o_ref, s_ref):
    qi = pl.program_id(1)
    tq = ATT_ROWS
    hw = N_HEADS * tq
    q0 = qi * tq
    n_cmp_rows = kc_ref.shape[1]
    n_sel = kk_ref.shape[1] // NSA_SEL_BLOCK

    head_ones = _head_ones().astype(BF16)
    q = q_ref[0]
    ms = _dot2_exact_rhs(q * q, head_ones) * (1.0 / HEAD_DIM)
    qn = q * lax.rsqrt(ms + RMS_EPS) * qn_ref[...] * (LOG2E * HEAD_DIM ** -0.5)
    qt = jnp.transpose(qn)
    q4t = jnp.concatenate([qt[h * HEAD_DIM:(h + 1) * HEAD_DIM, :] for h in range(N_HEADS)],
                          axis=1).astype(BF16)
    lane = lax.broadcasted_iota(jnp.int32, (1, hw), 1)
    slope = jnp.exp2(-2.0 * (lane // tq + 1).astype(F32))
    t_lane = lane % tq

    coeffs = (ALIBI_SPLIT * LOG2E * slope, LOG2E * slope, -LOG2E * slope * q0.astype(F32))
    f_row = lax.broadcasted_iota(jnp.int32, (HEAD_DIM, hw), 0)
    q_feat = jnp.zeros((HEAD_DIM, hw), F32)
    for ci, coeff in enumerate(coeffs):
        for pi, part in enumerate(_bf16_parts(coeff)):
            q_feat = jnp.where(f_row == ci * ALIBI_PARTS + pi, part, q_feat)
    q_feat = q_feat.astype(BF16)
    zero_half = jnp.zeros_like(q4t)
    q_cmp = jnp.concatenate([q4t, q_feat], axis=0)
    q_sel = jnp.concatenate([q4t, zero_half, q_feat, zero_half], axis=0)
    q_win = jnp.concatenate([zero_half, q4t, q_feat, zero_half], axis=0)

    n_idx = lax.broadcasted_iota(jnp.int32, (n_cmp_rows, hw), 0)
    valid_c = (q0 + t_lane) >= (n_idx * NSA_CMP_STRIDE + (NSA_CMP_LEN - 1))
    s_c = (jnp.dot(kc_ref[0], q_cmp, preferred_element_type=F32)
           + jnp.where(valid_c, 0.0, NSA_NEG))
    m_c = jnp.maximum(jnp.max(s_c, axis=0, keepdims=True), SCORE_FLOOR)
    p_c = jnp.exp2(s_c - m_c)
    l_c = jnp.sum(p_c, axis=0, keepdims=True)
    p_c = p_c * (1.0 / jnp.maximum(l_c, 1e-30))
    o_cmp = jnp.dot(vct_ref[0], p_c.astype(BF16), preferred_element_type=F32)

    p_heads = p_c[:, 0:tq]
    for h in range(1, N_HEADS):
        p_heads = p_heads + p_c[:, h * tq:(h + 1) * tq]
    per_sel = NSA_SEL_BLOCK // NSA_CMP_STRIDE
    j_ov = lax.broadcasted_iota(jnp.int32, (n_sel, n_cmp_rows), 0)
    n_ov = lax.broadcasted_iota(jnp.int32, (n_sel, n_cmp_rows), 1)
    overlap_t = ((n_ov >= per_sel * j_ov - (NSA_CMP_LEN // NSA_CMP_STRIDE - 1))
                 & (n_ov <= per_sel * j_ov + per_sel - 1)).astype(BF16)
    p_hi = p_heads.astype(BF16)
    p_rest = p_heads - p_hi.astype(F32)
    p_mid = p_rest.astype(BF16)
    p_lo = (p_rest - p_mid.astype(F32)).astype(BF16)
    imp = (jnp.dot(overlap_t, p_hi, preferred_element_type=F32)
           + jnp.dot(overlap_t, p_mid, preferred_element_type=F32)
           + jnp.dot(overlap_t, p_lo, preferred_element_type=F32))
    j_idx = lax.broadcasted_iota(jnp.int32, (n_sel, tq), 0)
    cur = (q0 + lax.broadcasted_iota(jnp.int32, (1, tq), 1)) // NSA_SEL_BLOCK
    forced = (j_idx == 0) | (j_idx == cur) | (j_idx == cur - 1)
    imp = jnp.where(j_idx <= cur, imp + jnp.where(forced, NSA_FORCE_BONUS, 0.0), -1.0)
    imp_rows = [imp[g:g + SUBLANES, :] for g in range(0, n_sel, SUBLANES)]
    rank_rows = [jnp.zeros((SUBLANES, tq), F32) for _ in imp_rows]
    row_in_group = lax.broadcasted_iota(jnp.int32, (SUBLANES, tq), 0)
    for jp in range(n_sel):
        other = imp[jp:jp + 1, :]
        jp_group, jp_row = divmod(jp, SUBLANES)
        for g, mine in enumerate(imp_rows):
            if g > jp_group:
                ahead = other >= mine
            elif g < jp_group:
                ahead = other > mine
            else:
                ahead = (other > mine) | ((other == mine) & (row_in_group > jp_row))
            rank_rows[g] = rank_rows[g] + jnp.where(ahead, 1.0, 0.0)
    rank = jnp.concatenate(rank_rows, axis=0)
    chosen = jnp.where(rank < float(min(NSA_TOP_N, n_sel)), 0.0, NSA_NEG)

    key_i = lax.broadcasted_iota(jnp.int32, (tq, tq), 0)
    query_i = lax.broadcasted_iota(jnp.int32, (tq, tq), 1)
    causal = jnp.where(query_i >= key_i, 0.0, NSA_NEG)
    window_start = jnp.where(key_i > query_i, 0.0, NSA_NEG)
    tiles_per_step = KEY_STEP // tq
    blocks_per_tile = tq // NSA_SEL_BLOCK
    win_tiles = NSA_WINDOW // tq
    last_step = qi // tiles_per_step
    ones_rows = jnp.ones((16, KEY_STEP), BF16)

    def branch(n_steps, step_of, q_pad, v_rows, bias_fn):
        def prefetch(i, slot):
            p = step_of(i)
            k0 = pl.multiple_of(p * KEY_STEP, KEY_STEP)
            raw = jnp.dot(kk_ref[0, pl.ds(k0, KEY_STEP), :], q_pad, preferred_element_type=F32)
            s = raw + jnp.concatenate([bias_fn(p)] * N_HEADS, axis=1)
            s_ref[slot] = s
            return _col_reduce(s, jnp.maximum, jnp.max)

        def absorb(i, slot, s_max, m, l, acc):
            m_new = jnp.maximum(m, s_max)
            alpha = jnp.exp2(m - m_new)
            prob = jnp.exp2(s_ref[slot] - m_new).astype(BF16)
            v_t = jnp.concatenate([vt_ref[0, step_of(i)][v_rows[0]:v_rows[1], :], ones_rows], axis=0)
            pv = jnp.dot(v_t, prob, preferred_element_type=F32)
            l = alpha * l + pv[HEAD_DIM:HEAD_DIM + 1, :]
            return m_new, l, alpha * acc + pv[0:HEAD_DIM, :]

        def pair(i2, carry):
            max0, m, l, acc = carry
            i = 2 * i2
            max1 = prefetch(i + 1, 1)
            m, l, acc = absorb(i, 0, max0, m, l, acc)
            max0 = prefetch(i + 2, 0)
            m, l, acc = absorb(i + 1, 1, max1, m, l, acc)
            return max0, m, l, acc

        init = (prefetch(0, 0), jnp.full((1, hw), SCORE_FLOOR, F32), jnp.zeros((1, hw), F32),
                jnp.zeros((HEAD_DIM, hw), F32))
        n_pairs = (n_steps - 1) // 2
        max0, m, l, acc = lax.fori_loop(0, n_pairs, pair, init)
        i = 2 * n_pairs

        def two_left(m, l, acc):
            max1 = prefetch(i + 1, 1)
            m, l, acc = absorb(i, 0, max0, m, l, acc)
            return absorb(i + 1, 1, max1, m, l, acc)

        def one_left(m, l, acc):
            return absorb(i, 0, max0, m, l, acc)

        _, l, acc = lax.cond(n_steps - i == 2, two_left, one_left, m, l, acc)
        return acc * (1.0 / l)

    def sel_tile_bias(kt):
        rows = [jnp.max(jnp.where(j_idx == blocks_per_tile * kt + i, chosen, NSA_NEG),
                        axis=0, keepdims=True) for i in range(blocks_per_tile)]
        bias = rows[-1]
        for i in range(blocks_per_tile - 2, -1, -1):
            bias = jnp.where(key_i < (i + 1) * NSA_SEL_BLOCK, rows[i], bias)
        return bias

    def sel_step_bias(p):
        tiles = []
        for i in range(tiles_per_step):
            kt = p * tiles_per_step + i
            tiles.append(sel_tile_bias(kt) + jnp.where(kt == qi, causal, 0.0)
                         + jnp.where(kt > qi, NSA_NEG, 0.0))
        return jnp.concatenate(tiles, axis=0)

    def win_step_bias(p):
        tiles = []
        for i in range(tiles_per_step):
            behind = qi - (p * tiles_per_step + i)
            bias = jnp.where(behind == 0, causal, jnp.where(behind == win_tiles, window_start, 0.0))
            tiles.append(bias + jnp.where((behind >= 0) & (behind <= win_tiles), 0.0, NSA_NEG))
        return jnp.concatenate(tiles, axis=0)

    first_win_step = jnp.maximum(qi - win_tiles, 0) // tiles_per_step
    o_win = branch(last_step - first_win_step + 1, lambda i: first_win_step + i,
                   q_win, (HEAD_DIM, 2 * HEAD_DIM), win_step_bias)

    blocks_per_step = KEY_STEP // NSA_SEL_BLOCK
    block_id = lax.broadcasted_iota(jnp.int32, (n_sel, 1), 0)
    block_used = jnp.max(chosen, axis=1, keepdims=True) > SCORE_FLOOR
    earliest = jnp.min(jnp.where(block_used & (block_id >= blocks_per_step), block_id, n_sel),
                       axis=0, keepdims=True)[0, 0]
    first_sel_step = jnp.clip(earliest // blocks_per_step, 1, jnp.maximum(last_step, 1))
    n_sel_steps = jnp.where(last_step >= 1, last_step - first_sel_step + 2, 1)
    o_sel = branch(n_sel_steps, lambda i: jnp.where(i == 0, 0, first_sel_step + i - 1),
                   q_sel, (0, HEAD_DIM), sel_step_bias)

    g_t = jnp.transpose(_sigmoid(gate_ref[0]))

    def gate_row(c):
        return jnp.concatenate([g_t[3 * h + c:3 * h + c + 1, :] for h in range(N_HEADS)], axis=1)

    o = gate_row(0) * o_cmp + gate_row(1) * o_sel + gate_row(2) * o_win
    o_hd = jnp.concatenate([o[:, h * tq:(h + 1) * tq] for h in range(N_HEADS)], axis=0)
    o_ref[0] = jnp.transpose(o_hd)


def _nsa_attn_call(q, gates, q_norm, kcmp, vcmpt, kk, vt):
    b, s, _ = q.shape
    n_cmp_rows = kcmp.shape[1]
    qn = jnp.tile(q_norm, N_HEADS).reshape(1, G_WIDTH)
    return pl.pallas_call(
        _nsa_attn_kernel,
        grid=(b, s // ATT_ROWS),
        in_specs=[pl.BlockSpec((1, ATT_ROWS, G_WIDTH), lambda bi, qi: (bi, qi, 0)),
                  pl.BlockSpec((1, ATT_ROWS, LANES), lambda bi, qi: (bi, qi, 0)),
                  _const_spec((1, G_WIDTH)),
                  pl.BlockSpec((1, n_cmp_rows, 2 * HEAD_DIM), lambda bi, qi: (bi, 0, 0)),
                  pl.BlockSpec((1, HEAD_DIM, n_cmp_rows), lambda bi, qi: (bi, 0, 0)),
                  pl.BlockSpec((1, s, 2 * LANES), lambda bi, qi: (bi, 0, 0)),
                  pl.BlockSpec((1, s // KEY_STEP, LANES, KEY_STEP), lambda bi, qi: (bi, 0, 0, 0))],
        out_specs=pl.BlockSpec((1, ATT_ROWS, G_WIDTH), lambda bi, qi: (bi, qi, 0)),
        out_shape=jax.ShapeDtypeStruct((b, s, G_WIDTH), F32),
        scratch_shapes=[pltpu.VMEM((2, KEY_STEP, N_HEADS * ATT_ROWS), F32)],
        compiler_params=_params(2),
    )(q, gates, qn, kcmp, vcmpt, kk, vt)


def kernel(x, ffn1_norm, ffn1_w_gate, ffn1_w_up, ffn1_w_down, mix_norm, w_in, pool_w, pool_scale, rwkv_mu, rwkv_w0, rwkv_w_up, rwkv_a0, rwkv_a_up, rwkv_g_up, rwkv_k_k, rwkv_k_a, rwkv_r_k, rwkv_ln_w, rwkv_ln_b, nsa_q_norm, nsa_k_norm, nsa_cmp_pos, nsa_cmp_k_w1, nsa_cmp_k_w2, nsa_cmp_v_w1, nsa_cmp_v_w2, conv_w, w_out, ffn2_norm, ffn2_w_gate, ffn2_w_up, ffn2_w_down):
    b, s, d = x.shape
    n_tok = b * s
    x2d = x.reshape(n_tok, d)
    n_layers = ffn1_norm.shape[0]
    ffn1_w = [w.astype(BF16) for w in (ffn1_w_gate, ffn1_w_up, ffn1_w_down)]
    ffn2_w = [w.astype(BF16) for w in (ffn2_w_gate, ffn2_w_up, ffn2_w_down)]
    wo = w_out.astype(BF16).reshape(n_layers, N_MIXERS, G_WIDTH, d)
    for l in range(n_layers):
        x2d = _ffn_call(x2d, ffn1_norm[l], *ffn1_w, l)
        ya, pb, q, gates, kcvc, kk, vt, yd = _mixin_call(
            x2d.reshape(b, s, d), mix_norm[l], w_in[l], pool_w[l], pool_scale[l], conv_w[l],
            nsa_k_norm[l][1], nsa_k_norm[l][2])
        yb = _rwkv_call(pb, rwkv_mu[l], rwkv_w0[l], rwkv_w_up[l], rwkv_a0[l], rwkv_a_up[l],
                        rwkv_g_up[l], rwkv_k_k[l], rwkv_k_a[l], rwkv_r_k[l], rwkv_ln_w[l],
                        rwkv_ln_b[l])
        kcmp, vcmpt = _nsa_compress_call(kcvc, nsa_cmp_pos[l], nsa_cmp_k_w1[l], nsa_cmp_k_w2[l],
                                         nsa_cmp_v_w1[l], nsa_cmp_v_w2[l], nsa_k_norm[l][0])
        yc = _nsa_attn_call(q, gates, nsa_q_norm[l], kcmp, vcmpt, kk, vt)
        mix = tuple(t.reshape(n_tok, G_WIDTH) for t in (ya, yb, yc, yd))
        x2d = _ffn_call(x2d, ffn2_norm[l], *ffn2_w, l, mix=mix, wo=wo)
    return x2d.reshape(b, s, d)
```

```python
import jax
import jax.numpy as jnp
from jax import lax
from jax.experimental import pallas as pl
from jax.experimental.pallas import tpu as pltpu

F32 = jnp.float32
BF16 = jnp.bfloat16

N_MIXERS = 4
HEAD_DIM = 64
N_HEADS = 4
G_WIDTH = N_HEADS * HEAD_DIM
RMS_EPS = 1e-6
POOL_WINDOWS = (2, 4, 8, 16)
POOL_HALO = 16
CONV_HALO = 8
RWKV_W_RANK, RWKV_A_RANK, RWKV_G_RANK = 64, 32, 64
RWKV_LN_EPS = 64e-5
RWKV_CHUNK = 64
RWKV_GROUP = 8
NSA_CMP_LEN = 32
NSA_CMP_STRIDE = 16
NSA_SEL_BLOCK = 64
NSA_TOP_N = 16
NSA_WINDOW = 512
NSA_FORCE_BONUS = 1e4
NSA_NEG = -1e9
SCORE_FLOOR = 0.5 * NSA_NEG
ALIBI_SPLIT = 64
ALIBI_PARTS = 3
LOG2E = 1.4426950408889634

PB_COLS = 1024
PC_COLS = 768
PD_COLS = 768

V7X_VMEM_BYTES = 64 * 1024 * 1024
VMEM_LIMIT = V7X_VMEM_BYTES - 8 * 1024 * 1024
LANES = 128
SUBLANES = 8

FFN_ROWS = 512
FFN_COLS = 256
MIX_ROWS = 512
MIX_SLABS = 4
ATT_ROWS = 256
KEY_STEP = 256


def _params(n_axes):
    return pltpu.CompilerParams(dimension_semantics=("arbitrary",) * n_axes,
                                vmem_limit_bytes=VMEM_LIMIT)


def _const_spec(shape):
    nd = len(shape)
    return pl.BlockSpec(shape, lambda *_: (0,) * nd, pipeline_mode=pl.Buffered(1))


def _dot(a, b):
    return jnp.dot(a.astype(BF16), b.astype(BF16), preferred_element_type=F32)


def _dot_nt(a, b):
    return lax.dot_general(a.astype(BF16), b.astype(BF16), (((1,), (1,)), ((), ())),
                           preferred_element_type=F32)


def _split(x):
    hi = x.astype(BF16)
    lo = (x - hi.astype(F32)).astype(BF16)
    return hi, lo


def _dot2_exact_rhs(a, b_bf16):
    ah, al = _split(a)
    return (jnp.dot(ah, b_bf16, preferred_element_type=F32)
            + jnp.dot(al, b_bf16, preferred_element_type=F32))


def _dot2_exact_lhs(a_bf16, b):
    bh, bl = _split(b)
    return (jnp.dot(a_bf16, bh, preferred_element_type=F32)
            + jnp.dot(a_bf16, bl, preferred_element_type=F32))


def _rms_rows(x, g):
    return x * lax.rsqrt(jnp.mean(x * x, axis=-1, keepdims=True) + RMS_EPS) * g


def _sigmoid(x):
    return 1.0 / (1.0 + jnp.exp(-x))


def _head_ones():
    r = lax.broadcasted_iota(jnp.int32, (G_WIDTH, G_WIDTH), 0) // HEAD_DIM
    c = lax.broadcasted_iota(jnp.int32, (G_WIDTH, G_WIDTH), 1) // HEAD_DIM
    return r == c


def _ffn_body(x_in, g_ref, wg_ref, wu_ref, wd_ref, o_ref, acc_ref):
    h = _rms_rows(x_in, g_ref[...]).astype(BF16)
    for c in range(wg_ref.shape[1] // FFN_COLS):
        cols = slice(c * FFN_COLS, (c + 1) * FFN_COLS)
        gate = jnp.dot(h, wg_ref[:, cols], preferred_element_type=F32)
        up = jnp.dot(h, wu_ref[:, cols], preferred_element_type=F32)
        act = (gate * _sigmoid(gate) * up).astype(BF16)
        down = jnp.dot(act, wd_ref[cols, :], preferred_element_type=F32)
        if c == 0:
            acc_ref[...] = down
        else:
            acc_ref[...] += down
    o_ref[...] = x_in + 0.5 * acc_ref[...]


def _ffn_kernel(x_ref, g_ref, wg_ref, wu_ref, wd_ref, o_ref, acc_ref):
    _ffn_body(x_ref[...], g_ref, wg_ref, wu_ref, wd_ref, o_ref, acc_ref)


def _out_ffn_kernel(x_ref, ya_ref, yb_ref, yc_ref, yd_ref, wo_ref,
                    g_ref, wg_ref, wu_ref, wd_ref, o_ref, acc_ref):
    x1 = x_ref[...]
    for i, y_ref in enumerate((ya_ref, yb_ref, yc_ref, yd_ref)):
        x1 = x1 + _dot(y_ref[...], wo_ref[i])
    _ffn_body(x1, g_ref, wg_ref, wu_ref, wd_ref, o_ref, acc_ref)


def _layer_spec(stacked, layer):
    shape = stacked.shape[1:]
    nd = len(shape)
    return pl.BlockSpec((None,) + shape, lambda *_: (layer,) + (0,) * nd,
                        pipeline_mode=pl.Buffered(1))


def _ffn_call(x2d, g, wg, wu, wd, layer, mix=None, wo=None):
    n_tok, d = x2d.shape
    row_spec = pl.BlockSpec((FFN_ROWS, d), lambda i: (i, 0))
    w_specs = [_const_spec((1, d)), _layer_spec(wg, layer), _layer_spec(wu, layer),
               _layer_spec(wd, layer)]
    w_args = [g.reshape(1, d), wg, wu, wd]
    if mix is None:
        kern, in_specs, args = _ffn_kernel, [row_spec] + w_specs, [x2d] + w_args
    else:
        y_spec = pl.BlockSpec((FFN_ROWS, G_WIDTH), lambda i: (i, 0))
        kern = _out_ffn_kernel
        in_specs = [row_spec] + [y_spec] * N_MIXERS + [_layer_spec(wo, layer)] + w_specs
        args = [x2d] + list(mix) + [wo] + w_args
    return pl.pallas_call(
        kern,
        grid=(n_tok // FFN_ROWS,),
        in_specs=in_specs,
        out_specs=row_spec,
        out_shape=jax.ShapeDtypeStruct((n_tok, d), F32),
        scratch_shapes=[pltpu.VMEM((FFN_ROWS, d), F32)],
        compiler_params=_params(1),
    )(*args)


def _mixin_kernel(x_ref, g_ref, w_ref, poolw_ref, pools_ref, convw_ref, kn_ref,
                  ya_ref, pb_ref, q_ref, gate_ref, kcvc_ref, kk_ref, vt_ref, yd_ref,
                  pa_ext, z_ext, kcvc_s):
    si = pl.program_id(1)
    rows = x_ref.shape[1]

    @pl.when(si == 0)
    def _():
        pa_ext[0:POOL_HALO, :] = jnp.zeros((POOL_HALO, G_WIDTH), F32)
        z_ext[0:CONV_HALO, :] = jnp.zeros((CONV_HALO, G_WIDTH), F32)

    c0, c1, c2 = G_WIDTH, G_WIDTH + PB_COLS, G_WIDTH + PB_COLS + PC_COLS
    half = rows // MIX_SLABS
    u_parts, pd_parts, pc_parts = [], [], []
    for r0 in range(0, rows, half):
        h = _rms_rows(x_ref[0, r0:r0 + half, :], g_ref[...]).astype(BF16)
        u_parts.append(jnp.dot(h, w_ref[:, 0:c0], preferred_element_type=F32))
        pd_parts.append(jnp.dot(h, w_ref[:, c2:c2 + PD_COLS], preferred_element_type=F32))
        pb_ref[0, r0:r0 + half, :] = jnp.dot(h, w_ref[:, c0:c1], preferred_element_type=F32)
        pc_parts.append(jnp.dot(h, w_ref[:, c1:c2], preferred_element_type=F32))
    u = jnp.concatenate(u_parts, axis=0)
    pd = jnp.concatenate(pd_parts, axis=0)
    pc = jnp.concatenate(pc_parts, axis=0)

    q_ref[0] = pc[:, 0:G_WIDTH]
    gate_ref[0] = pc[:, 5 * LANES:6 * LANES]
    kcvc_s[...] = pc[:, 2 * LANES:3 * LANES]
    for t in range(NSA_CMP_STRIDE):
        kcvc_ref[0, :, t * LANES:(t + 1) * LANES] = kcvc_s[pl.ds(t, rows // NSA_CMP_STRIDE,
                                                                stride=NSA_CMP_STRIDE), :]
    sel_kv = pc[:, 3 * LANES:4 * LANES]
    win_kv = pc[:, 4 * LANES:5 * LANES]
    first = lax.broadcasted_iota(jnp.int32, sel_kv.shape, 1) < HEAD_DIM
    kx = jnp.where(first, sel_kv, pltpu.roll(win_kv, HEAD_DIM, 1))
    vx = jnp.where(first, pltpu.roll(sel_kv, HEAD_DIM, 1), win_kv)
    sq = kx * kx
    ms_first = jnp.sum(jnp.where(first, sq, 0.0), axis=-1, keepdims=True) * (1.0 / HEAD_DIM)
    ms_second = jnp.sum(jnp.where(first, 0.0, sq), axis=-1, keepdims=True) * (1.0 / HEAD_DIM)
    inv = jnp.where(first, lax.rsqrt(ms_first + RMS_EPS), lax.rsqrt(ms_second + RMS_EPS))
    kk_ref[0, :, 0:LANES] = (kx * inv * kn_ref[...]).astype(BF16)
    key_pos = si * rows + lax.broadcasted_iota(jnp.int32, (rows, 1), 0)
    kk_ref[0, :, LANES:2 * LANES] = _pos_features(key_pos, LANES)
    for j in range(vt_ref.shape[1]):
        vt_ref[0, j] = jnp.transpose(vx[j * KEY_STEP:(j + 1) * KEY_STEP, :]).astype(BF16)

    pa_ext[POOL_HALO:, :] = u
    lane_group = lax.broadcasted_iota(jnp.int32, (rows, G_WIDTH), 1) // (G_WIDTH // len(POOL_WINDOWS))
    pos = si * rows + lax.broadcasted_iota(jnp.int32, (rows, G_WIDTH), 0)
    total = u
    for k in range(1, max(POOL_WINDOWS)):
        first_group = sum(1 for w in POOL_WINDOWS if w <= k)
        shifted = pa_ext[POOL_HALO - k:POOL_HALO - k + rows, :]
        total = total + jnp.where(lane_group >= first_group, shifted, 0.0)
    window = jnp.left_shift(2, lane_group)
    cnt = jnp.minimum(pos + 1, window).astype(F32)
    pooled = total / cnt
    ya = _dot(pooled - u, poolw_ref[...]) * pools_ref[...]
    ya_ref[0] = ya
    pa_ext[0:POOL_HALO, :] = pa_ext[rows:rows + POOL_HALO, :]

    cu = pd[:, 0:G_WIDTH]
    cb = pd[:, G_WIDTH:2 * G_WIDTH]
    cc = pd[:, 2 * G_WIDTH:3 * G_WIDTH]
    z = cc * cu
    z_ext[CONV_HALO:, :] = z
    y = (convw_ref[0:1, :] * z_ext[CONV_HALO - 2:CONV_HALO - 2 + rows, :]
         + convw_ref[1:2, :] * z_ext[CONV_HALO - 1:CONV_HALO - 1 + rows, :]
         + convw_ref[2:3, :] * z)
    yd_ref[0] = cb * y
    z_ext[0:CONV_HALO, :] = z_ext[rows:rows + CONV_HALO, :]


def _pad_cols(w, width):
    return jnp.pad(w, ((0, 0), (0, width - w.shape[1])))


def _w_in_padded(w_in):
    w = w_in.astype(BF16)
    rwkv_end = G_WIDTH + 3 * G_WIDTH + RWKV_W_RANK + RWKV_A_RANK + RWKV_G_RANK
    nsa_end = rwkv_end + G_WIDTH + 6 * HEAD_DIM + 3 * N_HEADS
    zeros = lambda n: jnp.zeros((w.shape[0], n), BF16)
    return jnp.concatenate(
        [w[:, 0:rwkv_end], zeros(G_WIDTH + PB_COLS - rwkv_end),
         w[:, rwkv_end:nsa_end], zeros(PC_COLS - (nsa_end - rwkv_end)),
         w[:, nsa_end:]], axis=1)


def _mixin_call(x, g, w_in, pool_w, pool_scale, conv_w, k_norm_sel, k_norm_win):
    b, s, d = x.shape
    w = _w_in_padded(w_in)
    kn = jnp.concatenate([k_norm_sel, k_norm_win]).reshape(1, LANES)
    cmp_rows = MIX_ROWS // NSA_CMP_STRIDE
    key_steps = MIX_ROWS // KEY_STEP
    n_groups = len(POOL_WINDOWS)
    pool_ch = G_WIDTH // n_groups
    poolw = jnp.zeros((G_WIDTH, G_WIDTH), F32)
    for gi in range(n_groups):
        poolw = poolw.at[gi * pool_ch:(gi + 1) * pool_ch, gi * pool_ch:(gi + 1) * pool_ch].set(pool_w[gi])
    convw = jnp.pad(conv_w, ((0, 8 - conv_w.shape[0]), (0, 0)))

    def out_spec(c):
        return pl.BlockSpec((1, MIX_ROWS, c), lambda bi, si: (bi, si, 0))

    return pl.pallas_call(
        _mixin_kernel,
        grid=(b, s // MIX_ROWS),
        in_specs=[pl.BlockSpec((1, MIX_ROWS, d), lambda bi, si: (bi, si, 0)),
                  _const_spec((1, d)), _const_spec(w.shape),
                  _const_spec((G_WIDTH, G_WIDTH)), _const_spec((1, G_WIDTH)),
                  _const_spec((8, G_WIDTH)), _const_spec((1, LANES))],
        out_specs=[out_spec(G_WIDTH), out_spec(PB_COLS), out_spec(G_WIDTH), out_spec(LANES),
                   pl.BlockSpec((1, cmp_rows, NSA_CMP_STRIDE * LANES), lambda bi, si: (bi, si, 0)),
                   out_spec(2 * LANES),
                   pl.BlockSpec((1, key_steps, LANES, KEY_STEP), lambda bi, si: (bi, si, 0, 0)),
                   out_spec(G_WIDTH)],
        out_shape=[jax.ShapeDtypeStruct((b, s, G_WIDTH), F32),
                   jax.ShapeDtypeStruct((b, s, PB_COLS), F32),
                   jax.ShapeDtypeStruct((b, s, G_WIDTH), F32),
                   jax.ShapeDtypeStruct((b, s, LANES), F32),
                   jax.ShapeDtypeStruct((b, s // NSA_CMP_STRIDE, NSA_CMP_STRIDE * LANES), F32),
                   jax.ShapeDtypeStruct((b, s, 2 * LANES), BF16),
                   jax.ShapeDtypeStruct((b, s // KEY_STEP, LANES, KEY_STEP), BF16),
                   jax.ShapeDtypeStruct((b, s, G_WIDTH), F32)],
        scratch_shapes=[pltpu.VMEM((MIX_ROWS + POOL_HALO, G_WIDTH), F32),
                        pltpu.VMEM((MIX_ROWS + CONV_HALO, G_WIDTH), F32),
                        pltpu.VMEM((MIX_ROWS, LANES), F32)],
        compiler_params=_params(2),
    )(x, g.reshape(1, d), w, poolw.astype(BF16), pool_scale.reshape(1, G_WIDTH), convw, kn)


def _block_diag(x, mask01):
    return jnp.concatenate([x] * N_HEADS, axis=0) * mask01


def _rwkv_kernel(pb_ref, mu_ref, w0_ref, wup_ref, a0_ref, aup_ref, gup_ref, kk_ref, ka_ref,
                 rk_ref, lnw_ref, lnb_ref, o_ref,
                 ext, state, r_s, k_s, v_s, lw_s, a_s, b_s, y_s):
    si = pl.program_id(1)
    group, seq_rows = pb_ref.shape[0], pb_ref.shape[1]
    g_w = G_WIDTH
    chunk = RWKV_CHUNK

    @pl.when(si == 0)
    def _():
        ext[:, 0:8, :] = jnp.zeros((group, 8, PB_COLS), F32)
        state[...] = jnp.zeros_like(state)

    shifted = []
    for b in range(group):
        p_b = pb_ref[b]
        ext[b, 8:, :] = p_b
        prev = ext[b, 7:7 + seq_rows, :]
        shifted.append(p_b + mu_ref[...] * (prev - p_b))
        ext[b, 0:8, :] = ext[b, seq_rows:seq_rows + 8, :]
    ps = jnp.concatenate(shifted, axis=0)

    head_mask = _head_ones()
    head_ones = head_mask.astype(BF16)

    def head_sum(t):
        return _dot2_exact_rhs(t, head_ones)

    r = ps[:, 0:g_w]
    k = ps[:, g_w:2 * g_w]
    v = ps[:, 2 * g_w:3 * g_w]
    tail = ps[:, 3 * g_w:4 * g_w]
    w = w0_ref[...] + _dot(jnp.tanh(tail), wup_ref[...])
    lw = -jnp.exp(-0.5) * _sigmoid(w)
    a = _sigmoid(a0_ref[...] + _dot(tail, aup_ref[...]))
    gate = _dot(_sigmoid(tail), gup_ref[...])
    kk = k * kk_ref[...]
    kk = kk * lax.rsqrt(jnp.maximum(head_sum(kk * kk), 1e-24))
    k2 = k * (1.0 + (a - 1.0) * ka_ref[...])
    bonus = head_sum(r * k2 * rk_ref[...]) * v
    r_s[...] = r
    k_s[...] = k2
    v_s[...] = v
    lw_s[...] = lw
    a_s[...] = -kk
    b_s[...] = kk * a

    row_i = lax.broadcasted_iota(jnp.int32, (chunk, g_w), 0)
    col_j = lax.broadcasted_iota(jnp.int32, (chunk, g_w), 1) % chunk
    strict_lower = row_i > col_j
    lower = row_i >= col_j
    eye = jnp.where(row_i == col_j, 1.0, 0.0)
    tri = (lax.broadcasted_iota(jnp.int32, (chunk, chunk), 0)
           >= lax.broadcasted_iota(jnp.int32, (chunk, chunk), 1)).astype(BF16)
    n_doublings = chunk.bit_length() - 1

    head_mask16 = head_ones

    def bd(x):
        return _block_diag(x.astype(BF16), head_mask16)

    def mm(a, b16):
        return jnp.dot(a.astype(BF16), b16, preferred_element_type=F32)

    ids = range(group)

    def group_local(c):
        sls = [pl.ds(pl.multiple_of(i * seq_rows + c * chunk, chunk), chunk) for i in ids]
        r_c, k_c, v_c, lw_c, a_c, b_c = ([t[sl, :] for sl in sls]
                                         for t in (r_s, k_s, v_s, lw_s, a_s, b_s))
        cw = [_dot2_exact_lhs(tri, lw_c[i]) for i in ids]
        cw_last = [cw[i][chunk - 1:chunk, :] for i in ids]
        a_t = [a_c[i] * jnp.exp(cw[i] - lw_c[i]) for i in ids]
        r_t = [r_c[i] * jnp.exp(cw[i]) for i in ids]
        e_inv = [jnp.exp(-cw[i]) for i in ids]
        bd_v = [bd(v_c[i]) for i in ids]
        pair = [lax.dot_general(
            jnp.concatenate([a_t[i], r_t[i]], axis=0).astype(BF16),
            jnp.concatenate([bd(b_c[i] * e_inv[i]), bd(k_c[i] * e_inv[i])], axis=0),
            (((1,), (1,)), ((), ())), preferred_element_type=F32) for i in ids]
        l_ak = [jnp.where(strict_lower, pair[i][0:chunk, g_w:2 * g_w], 0.0) for i in ids]
        m_rb = [jnp.where(lower, pair[i][chunk:2 * chunk, 0:g_w], 0.0) for i in ids]
        m_rk = [jnp.where(lower, pair[i][chunk:2 * chunk, g_w:2 * g_w], 0.0) for i in ids]

        with_v = [mm(jnp.concatenate([l_ak[i], m_rk[i]], axis=0), bd_v[i]) for i in ids]
        z0 = [with_v[i][0:chunk] for i in ids]
        l_pow = [jnp.where(strict_lower, pair[i][0:chunk, 0:g_w], 0.0) for i in ids]
        t_inv = [eye + l_pow[i] for i in ids]
        l_pow = [mm(l_pow[i], bd(l_pow[i])) for i in ids]
        for it in range(1, n_doublings):
            last = it == n_doublings - 1
            res = [mm(l_pow[i], jnp.concatenate(
                [bd(t_inv[i])] + ([] if last else [bd(l_pow[i])]), axis=1)) for i in ids]
            t_inv = [t_inv[i] + res[i][:, 0:g_w] for i in ids]
            if not last:
                l_pow = [res[i][:, g_w:2 * g_w] for i in ids]
        sol = [mm(t_inv[i], jnp.concatenate([bd(a_t[i]), bd(z0[i])], axis=1)) for i in ids]
        x1 = [sol[i][:, 0:g_w] for i in ids]
        x2 = [sol[i][:, g_w:2 * g_w] for i in ids]

        q = [mm(m_rb[i], jnp.concatenate([bd(x1[i]), bd(x2[i])], axis=1)) for i in ids]
        q2 = [q[i][:, g_w:2 * g_w] + with_v[i][chunk:2 * chunk] for i in ids]
        lhs = [jnp.concatenate([x1[i], r_t[i] + q[i][:, 0:g_w]], axis=0).astype(BF16) for i in ids]
        rhs = []
        for i in ids:
            e_fut = jnp.exp(cw_last[i] - cw[i])
            rhs.append(jnp.concatenate([b_c[i] * e_fut, k_c[i] * e_fut], axis=0).astype(BF16))
        decay = [jnp.exp(cw_last[i]) for i in ids]

        s0 = [state[i] for i in ids]
        us = [lax.dot_general(lhs[i], s0[i].astype(BF16), (((1,), (1,)), ((), ())),
                              preferred_element_type=F32) for i in ids]
        uv_t = []
        for i in ids:
            y_s[sls[i], :] = us[i][chunk:2 * chunk] + q2[i]
            u = us[i][0:chunk] + x2[i]
            uv_t.append(jnp.transpose(jnp.concatenate([u, v_c[i]], axis=0)).astype(BF16))
        upd = [jnp.dot(uv_t[i], rhs[i], preferred_element_type=F32) for i in ids]
        for i in ids:
            state[i] = s0[i] * decay[i] + jnp.where(head_mask, upd[i], 0.0)

    def chunk_step(c, carry):
        group_local(c)
        return carry

    lax.fori_loop(0, seq_rows // chunk, chunk_step, 0)

    y = y_s[...]
    inv_n = 1.0 / HEAD_DIM
    mean = head_sum(y) * inv_n
    dev = y - mean
    var = head_sum(dev * dev) * inv_n
    yn = dev * lax.rsqrt(var + RWKV_LN_EPS) * lnw_ref[...] + lnb_ref[...]
    out = (yn + bonus) * gate
    for b in range(group):
        o_ref[b] = out[b * seq_rows:(b + 1) * seq_rows, :]


def _rwkv_call(pb, mu, w0, w_up, a0, a_up, g_up, k_k, k_a, r_k, ln_w, ln_b):
    b, s, _ = pb.shape
    g_w = G_WIDTH
    row = lambda t: t.reshape(1, g_w)
    o_a = RWKV_W_RANK
    o_g = o_a + RWKV_A_RANK
    wup = jnp.zeros((g_w, g_w), F32).at[0:o_a].set(w_up).astype(BF16)
    aup = jnp.zeros((g_w, g_w), F32).at[o_a:o_g].set(a_up).astype(BF16)
    gup = jnp.zeros((g_w, g_w), F32).at[o_g:o_g + RWKV_G_RANK].set(g_up).astype(BF16)
    mu_p = _pad_cols(mu.reshape(1, -1), PB_COLS)
    vec = _const_spec((1, g_w))
    mat = _const_spec((g_w, g_w))
    group = min(RWKV_GROUP, b)
    seq_rows = MIX_ROWS // group
    seq = pltpu.VMEM((MIX_ROWS, g_w), F32)
    return pl.pallas_call(
        _rwkv_kernel,
        grid=(b // group, s // seq_rows),
        in_specs=[pl.BlockSpec((group, seq_rows, PB_COLS), lambda bi, si: (bi, si, 0)),
                  _const_spec((1, PB_COLS)), vec, mat, vec, mat, mat, vec, vec, vec, vec, vec],
        out_specs=pl.BlockSpec((group, seq_rows, g_w), lambda bi, si: (bi, si, 0)),
        out_shape=jax.ShapeDtypeStruct((b, s, g_w), F32),
        scratch_shapes=[pltpu.VMEM((group, seq_rows + 8, PB_COLS), F32),
                        pltpu.VMEM((group, g_w, g_w), F32),
                        seq, seq, seq, seq, seq, seq, seq],
        compiler_params=_params(2),
    )(pb, mu_p, row(w0), wup, row(a0), aup, gup, row(k_k), row(k_a), row(r_k), row(ln_w), row(ln_b))


def _gelu_tanh(x):
    return 0.5 * x * (1.0 + jnp.tanh(0.7978845608028654 * (x + 0.044715 * x * x * x)))


def _pos_features(pos, width):
    lane = lax.broadcasted_iota(jnp.int32, (pos.shape[0], width), 1) // ALIBI_PARTS
    feat = jnp.where(lane == 0, pos // ALIBI_SPLIT,
                     jnp.where(lane == 1, pos % ALIBI_SPLIT, jnp.where(lane == 2, 1, 0)))
    return feat.astype(F32).astype(BF16)


def _bf16_parts(v):
    parts = []
    for _ in range(ALIBI_PARTS):
        head = v.astype(BF16).astype(F32)
        parts.append(head)
        v = v - head
    return parts


def _nsa_compress_kernel(x_ref, pos_ref, w_first_ref, w_second_ref, kw2_ref, vw2t_ref, kn_ref,
                         kcmp_ref, vcmpt_ref):
    n_rows = x_ref.shape[1]
    hid = kw2_ref.shape[0]
    x = x_ref[0]
    first = _dot(x + pos_ref[0:1, :], w_first_ref[...])
    second = _dot(x + pos_ref[1:2, :], w_second_ref[...])
    hidden = _gelu_tanh(first + pltpu.roll(second, n_rows - 1, 0))
    k_cmp = _dot(hidden[:, 0:hid], kw2_ref[...])
    kcmp_ref[0, :, 0:HEAD_DIM] = _rms_rows(k_cmp, kn_ref[...]).astype(BF16)
    last_token = (lax.broadcasted_iota(jnp.int32, (n_rows, 1), 0) * NSA_CMP_STRIDE
                  + (NSA_CMP_LEN - 1))
    kcmp_ref[0, :, HEAD_DIM:2 * HEAD_DIM] = _pos_features(last_token, HEAD_DIM)
    vcmpt_ref[0] = _dot_nt(vw2t_ref[...], hidden[:, hid:2 * hid]).astype(BF16)


def _interleave_kv(k_part, v_part):
    zeros = jnp.zeros_like(k_part)
    top = jnp.concatenate([k_part, zeros], axis=2)
    bottom = jnp.concatenate([zeros, v_part], axis=2)
    return jnp.concatenate([top, bottom], axis=1).reshape(NSA_CMP_STRIDE * LANES, -1)


def _nsa_compress_call(kcvc, pos, kw1, kw2, vw1, vw2, k_norm0):
    b, n_chunks, width = kcvc.shape
    hid = kw1.shape[1]
    split = lambda w1, half: w1.reshape(2, NSA_CMP_STRIDE, HEAD_DIM, hid)[half]
    w_first = _interleave_kv(split(kw1, 0), split(vw1, 0)).astype(BF16)
    w_second = _interleave_kv(split(kw1, 1), split(vw1, 1)).astype(BF16)
    pos_halves = pos.reshape(2, NSA_CMP_STRIDE, HEAD_DIM)
    pos2 = jnp.concatenate([pos_halves, pos_halves], axis=2).reshape(2, width)
    pos2 = jnp.pad(pos2, ((0, 6), (0, 0)))
    return pl.pallas_call(
        _nsa_compress_kernel,
        grid=(b,),
        in_specs=[pl.BlockSpec((1, n_chunks, width), lambda bi: (bi, 0, 0)), _const_spec((8, width)),
                  _const_spec((width, 2 * hid)), _const_spec((width, 2 * hid)),
                  _const_spec((hid, HEAD_DIM)), _const_spec((HEAD_DIM, hid)),
                  _const_spec((1, HEAD_DIM))],
        out_specs=[pl.BlockSpec((1, n_chunks, 2 * HEAD_DIM), lambda bi: (bi, 0, 0)),
                   pl.BlockSpec((1, HEAD_DIM, n_chunks), lambda bi: (bi, 0, 0))],
        out_shape=[jax.ShapeDtypeStruct((b, n_chunks, 2 * HEAD_DIM), BF16),
                   jax.ShapeDtypeStruct((b, HEAD_DIM, n_chunks), BF16)],
        compiler_params=_params(1),
    )(kcvc, pos2, w_first, w_second, kw2.astype(BF16), vw2.T.astype(BF16),
      k_norm0.reshape(1, HEAD_DIM))


def _col_reduce(x, pair_op, reduce_fn, slab=32):
    parts = [x[i:i + slab] for i in range(0, x.shape[0], slab)]
    while len(parts) > 1:
        parts = [pair_op(parts[i], parts[i + 1]) for i in range(0, len(parts), 2)]
    return reduce_fn(parts[0], axis=0, keepdims=True)


def _nsa_attn_kernel(q_ref, gate_ref, qn_ref, kc_ref, vct_ref, kk_ref, vt_ref, o_ref, s_ref):
    qi = pl.program_id(1)
    tq = ATT_ROWS
    hw = N_HEADS * tq
    q0 = qi * tq
    n_cmp_rows = kc_ref.shape[1]
    n_sel = kk_ref.shape[1] // NSA_SEL_BLOCK

    head_ones = _head_ones().astype(BF16)
    q = q_ref[0]
    ms = _dot2_exact_rhs(q * q, head_ones) * (1.0 / HEAD_DIM)
    qn = q * lax.rsqrt(ms + RMS_EPS) * qn_ref[...] * (LOG2E * HEAD_DIM ** -0.5)
    qt = jnp.transpose(qn)
    q4t = jnp.concatenate([qt[h * HEAD_DIM:(h + 1) * HEAD_DIM, :] for h in range(N_HEADS)],
                          axis=1).astype(BF16)
    lane = lax.broadcasted_iota(jnp.int32, (1, hw), 1)
    slope = jnp.exp2(-2.0 * (lane // tq + 1).astype(F32))
    t_lane = lane % tq

    coeffs = (ALIBI_SPLIT * LOG2E * slope, LOG2E * slope, -LOG2E * slope * q0.astype(F32))
    f_row = lax.broadcasted_iota(jnp.int32, (HEAD_DIM, hw), 0)
    q_feat = jnp.zeros((HEAD_DIM, hw), F32)
    for ci, coeff in enumerate(coeffs):
        for pi, part in enumerate(_bf16_parts(coeff)):
            q_feat = jnp.where(f_row == ci * ALIBI_PARTS + pi, part, q_feat)
    q_feat = q_feat.astype(BF16)
    zero_half = jnp.zeros_like(q4t)
    q_cmp = jnp.concatenate([q4t, q_feat], axis=0)
    q_sel = jnp.concatenate([q4t, zero_half, q_feat, zero_half], axis=0)
    q_win = jnp.concatenate([zero_half, q4t, q_feat, zero_half], axis=0)

    n_idx = lax.broadcasted_iota(jnp.int32, (n_cmp_rows, hw), 0)
    valid_c = (q0 + t_lane) >= (n_idx * NSA_CMP_STRIDE + (NSA_CMP_LEN - 1))
    s_c = (jnp.dot(kc_ref[0], q_cmp, preferred_element_type=F32)
           + jnp.where(valid_c, 0.0, NSA_NEG))
    m_c = jnp.maximum(jnp.max(s_c, axis=0, keepdims=True), SCORE_FLOOR)
    p_c = jnp.exp2(s_c - m_c)
    l_c = jnp.sum(p_c, axis=0, keepdims=True)
    p_c = p_c * (1.0 / jnp.maximum(l_c, 1e-30))
    o_cmp = jnp.dot(vct_ref[0], p_c.astype(BF16), preferred_element_type=F32)

    p_heads = p_c[:, 0:tq]
    for h in range(1, N_HEADS):
        p_heads = p_heads + p_c[:, h * tq:(h + 1) * tq]
    per_sel = NSA_SEL_BLOCK // NSA_CMP_STRIDE
    j_ov = lax.broadcasted_iota(jnp.int32, (n_sel, n_cmp_rows), 0)
    n_ov = lax.broadcasted_iota(jnp.int32, (n_sel, n_cmp_rows), 1)
    overlap_t = ((n_ov >= per_sel * j_ov - (NSA_CMP_LEN // NSA_CMP_STRIDE - 1))
                 & (n_ov <= per_sel * j_ov + per_sel - 1)).astype(BF16)
    p_hi = p_heads.astype(BF16)
    p_rest = p_heads - p_hi.astype(F32)
    p_mid = p_rest.astype(BF16)
    p_lo = (p_rest - p_mid.astype(F32)).astype(BF16)
    imp = (jnp.dot(overlap_t, p_hi, preferred_element_type=F32)
           + jnp.dot(overlap_t, p_mid, preferred_element_type=F32)
           + jnp.dot(overlap_t, p_lo, preferred_element_type=F32))
    j_idx = lax.broadcasted_iota(jnp.int32, (n_sel, tq), 0)
    cur = (q0 + lax.broadcasted_iota(jnp.int32, (1, tq), 1)) // NSA_SEL_BLOCK
    forced = (j_idx == 0) | (j_idx == cur) | (j_idx == cur - 1)
    imp = jnp.where(j_idx <= cur, imp + jnp.where(forced, NSA_FORCE_BONUS, 0.0), -1.0)
    imp_rows = [imp[g:g + SUBLANES, :] for g in range(0, n_sel, SUBLANES)]
    rank_rows = [jnp.zeros((SUBLANES, tq), F32) for _ in imp_rows]
    row_in_group = lax.broadcasted_iota(jnp.int32, (SUBLANES, tq), 0)
    for jp in range(n_sel):
        other = imp[jp:jp + 1, :]
        jp_group, jp_row = divmod(jp, SUBLANES)
        for g, mine in enumerate(imp_rows):
            if g > jp_group:
                ahead = other >= mine
            elif g < jp_group:
                ahead = other > mine
            else:
                ahead = (other > mine) | ((other == mine) & (row_in_group > jp_row))
            rank_rows[g] = rank_rows[g] + jnp.where(ahead, 1.0, 0.0)
    rank = jnp.concatenate(rank_rows, axis=0)
    chosen = jnp.where(rank < float(min(NSA_TOP_N, n_sel)), 0.0, NSA_NEG)

    key_i = lax.broadcasted_iota(jnp.int32, (tq, tq), 0)
    query_i = lax.broadcasted_iota(jnp.int32, (tq, tq), 1)
    causal = jnp.where(query_i >= key_i, 0.0, NSA_NEG)
    window_start = jnp.where(key_i > query_i, 0.0, NSA_NEG)
    tiles_per_step = KEY_STEP // tq
    blocks_per_tile = tq // NSA_SEL_BLOCK
    win_tiles = NSA_WINDOW // tq
    last_step = qi // tiles_per_step
    ones_rows = jnp.ones((16, KEY_STEP), BF16)

    def branch(n_steps, step_of, q_pad, v_rows, bias_fn):
        def prefetch(i, slot):
            p = step_of(i)
            k0 = pl.multiple_of(p * KEY_STEP, KEY_STEP)
            raw = jnp.dot(kk_ref[0, pl.ds(k0, KEY_STEP), :], q_pad, preferred_element_type=F32)
            s = raw + jnp.concatenate([bias_fn(p)] * N_HEADS, axis=1)
            s_ref[slot] = s
            return _col_reduce(s, jnp.maximum, jnp.max)

        def absorb(i, slot, s_max, m, l, acc):
            m_new = jnp.maximum(m, s_max)
            alpha = jnp.exp2(m - m_new)
            prob = jnp.exp2(s_ref[slot] - m_new).astype(BF16)
            v_t = jnp.concatenate([vt_ref[0, step_of(i)][v_rows[0]:v_rows[1], :], ones_rows], axis=0)
            pv = jnp.dot(v_t, prob, preferred_element_type=F32)
            l = alpha * l + pv[HEAD_DIM:HEAD_DIM + 1, :]
            return m_new, l, alpha * acc + pv[0:HEAD_DIM, :]

        def pair(i2, carry):
            max0, m, l, acc = carry
            i = 2 * i2
            max1 = prefetch(i + 1, 1)
            m, l, acc = absorb(i, 0, max0, m, l, acc)
            max0 = prefetch(i + 2, 0)
            m, l, acc = absorb(i + 1, 1, max1, m, l, acc)
            return max0, m, l, acc

        init = (prefetch(0, 0), jnp.full((1, hw), SCORE_FLOOR, F32), jnp.zeros((1, hw), F32),
                jnp.zeros((HEAD_DIM, hw), F32))
        n_pairs = (n_steps - 1) // 2
        max0, m, l, acc = lax.fori_loop(0, n_pairs, pair, init)
        i = 2 * n_pairs

        def two_left(m, l, acc):
            max1 = prefetch(i + 1, 1)
            m, l, acc = absorb(i, 0, max0, m, l, acc)
            return absorb(i + 1, 1, max1, m, l, acc)

        def one_left(m, l, acc):
            return absorb(i, 0, max0, m, l, acc)

        _, l, acc = lax.cond(n_steps - i == 2, two_left, one_left, m, l, acc)
        return acc * (1.0 / l)

    def sel_tile_bias(kt):
        rows = [jnp.max(jnp.where(j_idx == blocks_per_tile * kt + i, chosen, NSA_NEG),
                        axis=0, keepdims=True) for i in range(blocks_per_tile)]
        bias = rows[-1]
        for i in range(blocks_per_tile - 2, -1, -1):
            bias = jnp.where(key_i < (i + 1) * NSA_SEL_BLOCK, rows[i], bias)
        return bias

    def sel_step_bias(p):
        tiles = []
        for i in range(tiles_per_step):
            kt = p * tiles_per_step + i
            tiles.append(sel_tile_bias(kt) + jnp.where(kt == qi, causal, 0.0)
                         + jnp.where(kt > qi, NSA_NEG, 0.0))
        return jnp.concatenate(tiles, axis=0)

    def win_step_bias(p):
        tiles = []
        for i in range(tiles_per_step):
            behind = qi - (p * tiles_per_step + i)
            bias = jnp.where(behind == 0, causal, jnp.where(behind == win_tiles, window_start, 0.0))
            tiles.append(bias + jnp.where((behind >= 0) & (behind <= win_tiles), 0.0, NSA_NEG))
        return jnp.concatenate(tiles, axis=0)

    first_win_step = jnp.maximum(qi - win_tiles, 0) // tiles_per_step
    o_win = branch(last_step - first_win_step + 1, lambda i: first_win_step + i,
                   q_win, (HEAD_DIM, 2 * HEAD_DIM), win_step_bias)

    blocks_per_step = KEY_STEP // NSA_SEL_BLOCK
    block_id = lax.broadcasted_iota(jnp.int32, (n_sel, 1), 0)
    block_used = jnp.max(chosen, axis=1, keepdims=True) > SCORE_FLOOR
    earliest = jnp.min(jnp.where(block_used & (block_id >= blocks_per_step), block_id, n_sel),
                       axis=0, keepdims=True)[0, 0]
    first_sel_step = jnp.clip(earliest // blocks_per_step, 1, jnp.maximum(last_step, 1))
    n_sel_steps = jnp.where(last_step >= 1, last_step - first_sel_step + 2, 1)
    o_sel = branch(n_sel_steps, lambda i: jnp.where(i == 0, 0, first_sel_step + i - 1),
                   q_sel, (0, HEAD_DIM), sel_step_bias)

    g_t = jnp.transpose(_sigmoid(gate_ref[0]))

    def gate_row(c):
        return jnp.concatenate([g_t[3 * h + c:3 * h + c + 1, :] for h in range(N_HEADS)], axis=1)

    o = gate_row(0) * o_cmp + gate_row(1) * o_sel + gate_row(2) * o_win
    o_hd = jnp.concatenate([o[:, h * tq:(h + 1) * tq] for h in range(N_HEADS)], axis=0)
    o_ref[0] = jnp.transpose(o_hd)


def _nsa_attn_call(q, gates, q_norm, kcmp, vcmpt, kk, vt):
    b, s, _ = q.shape
    n_cmp_rows = kcmp.shape[1]
    qn = jnp.tile(q_norm, N_HEADS).reshape(1, G_WIDTH)
    return pl.pallas_call(
        _nsa_attn_kernel,
        grid=(b, s // ATT_ROWS),
        in_specs=[pl.BlockSpec((1, ATT_ROWS, G_WIDTH), lambda bi, qi: (bi, qi, 0)),
                  pl.BlockSpec((1, ATT_ROWS, LANES), lambda bi, qi: (bi, qi, 0)),
                  _const_spec((1, G_WIDTH)),
                  pl.BlockSpec((1, n_cmp_rows, 2 * HEAD_DIM), lambda bi, qi: (bi, 0, 0)),
                  pl.BlockSpec((1, HEAD_DIM, n_cmp_rows), lambda bi, qi: (bi, 0, 0)),
                  pl.BlockSpec((1, s, 2 * LANES), lambda bi, qi: (bi, 0, 0)),
                  pl.BlockSpec((1, s // KEY_STEP, LANES, KEY_STEP), lambda bi, qi: (bi, 0, 0, 0))],
        out_specs=pl.BlockSpec((1, ATT_ROWS, G_WIDTH), lambda bi, qi: (bi, qi, 0)),
        out_shape=jax.ShapeDtypeStruct((b, s, G_WIDTH), F32),
        scratch_shapes=[pltpu.VMEM((2, KEY_STEP, N_HEADS * ATT_ROWS), F32)],
        compiler_params=_params(2),
    )(q, gates, qn, kcmp, vcmpt, kk, vt)


def kernel(x, ffn1_norm, ffn1_w_gate, ffn1_w_up, ffn1_w_down, mix_norm, w_in, pool_w, pool_scale, rwkv_mu, rwkv_w0, rwkv_w_up, rwkv_a0, rwkv_a_up, rwkv_g_up, rwkv_k_k, rwkv_k_a, rwkv_r_k, rwkv_ln_w, rwkv_ln_b, nsa_q_norm, nsa_k_norm, nsa_cmp_pos, nsa_cmp_k_w1, nsa_cmp_k_w2, nsa_cmp_v_w1, nsa_cmp_v_w2, conv_w, w_out, ffn2_norm, ffn2_w_gate, ffn2_w_up, ffn2_w_down):
    b, s, d = x.shape
    n_tok = b * s
    assert s % MIX_ROWS == 0 and s % ATT_ROWS == 0 and s % KEY_STEP == 0 and n_tok % FFN_ROWS == 0
    assert b % min(RWKV_GROUP, b) == 0 and KEY_STEP % ATT_ROWS == 0
    x2d = x.reshape(n_tok, d)
    n_layers = ffn1_norm.shape[0]
    ffn1_w = [w.astype(BF16) for w in (ffn1_w_gate, ffn1_w_up, ffn1_w_down)]
    ffn2_w = [w.astype(BF16) for w in (ffn2_w_gate, ffn2_w_up, ffn2_w_down)]
    wo = w_out.astype(BF16).reshape(n_layers, N_MIXERS, G_WIDTH, d)
    for l in range(n_layers):
        x2d = _ffn_call(x2d, ffn1_norm[l], *ffn1_w, l)
        ya, pb, q, gates, kcvc, kk, vt, yd = _mixin_call(
            x2d.reshape(b, s, d), mix_norm[l], w_in[l], pool_w[l], pool_scale[l], conv_w[l],
            nsa_k_norm[l][1], nsa_k_norm[l][2])
        yb = _rwkv_call(pb, rwkv_mu[l], rwkv_w0[l], rwkv_w_up[l], rwkv_a0[l], rwkv_a_up[l],
                        rwkv_g_up[l], rwkv_k_k[l], rwkv_k_a[l], rwkv_r_k[l], rwkv_ln_w[l],
                        rwkv_ln_b[l])
        kcmp, vcmpt = _nsa_compress_call(kcvc, nsa_cmp_pos[l], nsa_cmp_k_w1[l], nsa_cmp_k_w2[l],
                                         nsa_cmp_v_w1[l], nsa_cmp_v_w2[l], nsa_k_norm[l][0])
        yc = _nsa_attn_call(q, gates, nsa_q_norm[l], kcmp, vcmpt, kk, vt)
        mix = tuple(t.reshape(n_tok, G_WIDTH) for t in (ya, yb, yc, yd))
        x2d = _ffn_call(x2d, ffn2_norm[l], *ffn2_w, l, mix=mix, wo=wo)
    return x2d.reshape(b, s, d)
```

```python
import jax
import jax.numpy as jnp
from jax import lax
from jax.experimental import pallas as pl
from jax.experimental.pallas import tpu as pltpu

F32 = jnp.float32
BF16 = jnp.bfloat16

N_MIXERS = 4
HEAD_DIM = 64
N_HEADS = 4
G_WIDTH = N_HEADS * HEAD_DIM
RMS_EPS = 1e-6
POOL_WINDOWS = (2, 4, 8, 16)
POOL_HALO = 16
CONV_HALO = 8
RWKV_W_RANK, RWKV_A_RANK, RWKV_G_RANK = 64, 32, 64
RWKV_LN_EPS = 64e-5
RWKV_CHUNK = 64
RWKV_GROUP = 8
NSA_CMP_LEN = 32
NSA_CMP_STRIDE = 16
NSA_SEL_BLOCK = 64
NSA_TOP_N = 16
NSA_WINDOW = 512
NSA_FORCE_BONUS = 1e4
NSA_NEG = -1e9
SCORE_FLOOR = 0.5 * NSA_NEG
ALIBI_SPLIT = 64
ALIBI_PARTS = 3
LOG2E = 1.4426950408889634

PB_COLS = 1024
PC_COLS = 768
PD_COLS = 768

V7X_VMEM_BYTES = 64 * 1024 * 1024
VMEM_LIMIT = V7X_VMEM_BYTES - 8 * 1024 * 1024
LANES = 128
SUBLANES = 8

FFN_ROWS = 512
FFN_COLS = 256
MIX_ROWS = 512
MIX_SLABS = 1
ATT_ROWS = 256
KEY_STEP = 256


def _params(n_axes):
    return pltpu.CompilerParams(dimension_semantics=("arbitrary",) * n_axes,
                                vmem_limit_bytes=VMEM_LIMIT)


def _const_spec(shape):
    nd = len(shape)
    return pl.BlockSpec(shape, lambda *_: (0,) * nd, pipeline_mode=pl.Buffered(1))


def _dot(a, b):
    return jnp.dot(a.astype(BF16), b.astype(BF16), preferred_element_type=F32)


def _dot_nt(a, b):
    return lax.dot_general(a.astype(BF16), b.astype(BF16), (((1,), (1,)), ((), ())),
                           preferred_element_type=F32)


def _split(x):
    hi = x.astype(BF16)
    lo = (x - hi.astype(F32)).astype(BF16)
    return hi, lo


def _dot2_exact_rhs(a, b_bf16):
    ah, al = _split(a)
    return (jnp.dot(ah, b_bf16, preferred_element_type=F32)
            + jnp.dot(al, b_bf16, preferred_element_type=F32))


def _dot2_exact_lhs(a_bf16, b):
    bh, bl = _split(b)
    return (jnp.dot(a_bf16, bh, preferred_element_type=F32)
            + jnp.dot(a_bf16, bl, preferred_element_type=F32))


def _rms_rows(x, g):
    return x * lax.rsqrt(jnp.mean(x * x, axis=-1, keepdims=True) + RMS_EPS) * g


def _sigmoid(x):
    return 1.0 / (1.0 + jnp.exp(-x))


def _head_ones():
    r = lax.broadcasted_iota(jnp.int32, (G_WIDTH, G_WIDTH), 0) // HEAD_DIM
    c = lax.broadcasted_iota(jnp.int32, (G_WIDTH, G_WIDTH), 1) // HEAD_DIM
    return r == c


def _ffn_body(x_in, g_ref, wg_ref, wu_ref, wd_ref, o_ref, acc_ref):
    h = _rms_rows(x_in, g_ref[...]).astype(BF16)
    for c in range(wg_ref.shape[1] // FFN_COLS):
        cols = slice(c * FFN_COLS, (c + 1) * FFN_COLS)
        gate = jnp.dot(h, wg_ref[:, cols], preferred_element_type=F32)
        up = jnp.dot(h, wu_ref[:, cols], preferred_element_type=F32)
        act = (gate * _sigmoid(gate) * up).astype(BF16)
        down = jnp.dot(act, wd_ref[cols, :], preferred_element_type=F32)
        if c == 0:
            acc_ref[...] = down
        else:
            acc_ref[...] += down
    o_ref[...] = x_in + 0.5 * acc_ref[...]


def _ffn_kernel(x_ref, g_ref, wg_ref, wu_ref, wd_ref, o_ref, acc_ref):
    _ffn_body(x_ref[...], g_ref, wg_ref, wu_ref, wd_ref, o_ref, acc_ref)


def _out_ffn_kernel(x_ref, ya_ref, yb_ref, yc_ref, yd_ref, wo_ref,
                    g_ref, wg_ref, wu_ref, wd_ref, o_ref, acc_ref):
    x1 = x_ref[...]
    for i, y_ref in enumerate((ya_ref, yb_ref, yc_ref, yd_ref)):
        x1 = x1 + _dot(y_ref[...], wo_ref[i])
    _ffn_body(x1, g_ref, wg_ref, wu_ref, wd_ref, o_ref, acc_ref)


def _layer_spec(stacked, layer):
    shape = stacked.shape[1:]
    nd = len(shape)
    return pl.BlockSpec((None,) + shape, lambda *_: (layer,) + (0,) * nd,
                        pipeline_mode=pl.Buffered(1))


def _ffn_call(x2d, g, wg, wu, wd, layer, mix=None, wo=None):
    n_tok, d = x2d.shape
    row_spec = pl.BlockSpec((FFN_ROWS, d), lambda i: (i, 0))
    w_specs = [_const_spec((1, d)), _layer_spec(wg, layer), _layer_spec(wu, layer),
               _layer_spec(wd, layer)]
    w_args = [g.reshape(1, d), wg, wu, wd]
    if mix is None:
        kern, in_specs, args = _ffn_kernel, [row_spec] + w_specs, [x2d] + w_args
    else:
        y_spec = pl.BlockSpec((FFN_ROWS, G_WIDTH), lambda i: (i, 0))
        kern = _out_ffn_kernel
        in_specs = [row_spec] + [y_spec] * N_MIXERS + [_layer_spec(wo, layer)] + w_specs
        args = [x2d] + list(mix) + [wo] + w_args
    return pl.pallas_call(
        kern,
        grid=(n_tok // FFN_ROWS,),
        in_specs=in_specs,
        out_specs=row_spec,
        out_shape=jax.ShapeDtypeStruct((n_tok, d), F32),
        scratch_shapes=[pltpu.VMEM((FFN_ROWS, d), F32)],
        compiler_params=_params(1),
    )(*args)


def _mixin_kernel(x_ref, g_ref, w_ref, poolw_ref, pools_ref, convw_ref, kn_ref,
                  ya_ref, pb_ref, q_ref, gate_ref, kcvc_ref, kk_ref, vt_ref, yd_ref,
                  pa_ext, z_ext, kcvc_s):
    si = pl.program_id(1)
    rows = x_ref.shape[1]

    @pl.when(si == 0)
    def _():
        pa_ext[0:POOL_HALO, :] = jnp.zeros((POOL_HALO, G_WIDTH), F32)
        z_ext[0:CONV_HALO, :] = jnp.zeros((CONV_HALO, G_WIDTH), F32)

    c0, c1, c2 = G_WIDTH, G_WIDTH + PB_COLS, G_WIDTH + PB_COLS + PC_COLS
    half = rows // MIX_SLABS
    u_parts, pd_parts, pc_parts = [], [], []
    for r0 in range(0, rows, half):
        h = _rms_rows(x_ref[0, r0:r0 + half, :], g_ref[...]).astype(BF16)
        u_parts.append(jnp.dot(h, w_ref[:, 0:c0], preferred_element_type=F32))
        pd_parts.append(jnp.dot(h, w_ref[:, c2:c2 + PD_COLS], preferred_element_type=F32))
        pb_ref[0, r0:r0 + half, :] = jnp.dot(h, w_ref[:, c0:c1], preferred_element_type=F32)
        pc_parts.append(jnp.dot(h, w_ref[:, c1:c2], preferred_element_type=F32))
    u = jnp.concatenate(u_parts, axis=0)
    pd = jnp.concatenate(pd_parts, axis=0)
    pc = jnp.concatenate(pc_parts, axis=0)

    q_ref[0] = pc[:, 0:G_WIDTH]
    gate_ref[0] = pc[:, 5 * LANES:6 * LANES]
    kcvc_s[...] = pc[:, 2 * LANES:3 * LANES]
    for t in range(NSA_CMP_STRIDE):
        kcvc_ref[0, :, t * LANES:(t + 1) * LANES] = kcvc_s[pl.ds(t, rows // NSA_CMP_STRIDE,
                                                                stride=NSA_CMP_STRIDE), :]
    sel_kv = pc[:, 3 * LANES:4 * LANES]
    win_kv = pc[:, 4 * LANES:5 * LANES]
    first = lax.broadcasted_iota(jnp.int32, sel_kv.shape, 1) < HEAD_DIM
    kx = jnp.where(first, sel_kv, pltpu.roll(win_kv, HEAD_DIM, 1))
    vx = jnp.where(first, pltpu.roll(sel_kv, HEAD_DIM, 1), win_kv)
    sq = kx * kx
    ms_first = jnp.sum(jnp.where(first, sq, 0.0), axis=-1, keepdims=True) * (1.0 / HEAD_DIM)
    ms_second = jnp.sum(jnp.where(first, 0.0, sq), axis=-1, keepdims=True) * (1.0 / HEAD_DIM)
    inv = jnp.where(first, lax.rsqrt(ms_first + RMS_EPS), lax.rsqrt(ms_second + RMS_EPS))
    kk_ref[0, :, 0:LANES] = (kx * inv * kn_ref[...]).astype(BF16)
    key_pos = si * rows + lax.broadcasted_iota(jnp.int32, (rows, 1), 0)
    kk_ref[0, :, LANES:2 * LANES] = _pos_features(key_pos, LANES)
    for j in range(vt_ref.shape[1]):
        vt_ref[0, j] = jnp.transpose(vx[j * KEY_STEP:(j + 1) * KEY_STEP, :]).astype(BF16)

    pa_ext[POOL_HALO:, :] = u
    lane_group = lax.broadcasted_iota(jnp.int32, (rows, G_WIDTH), 1) // (G_WIDTH // len(POOL_WINDOWS))
    pos = si * rows + lax.broadcasted_iota(jnp.int32, (rows, G_WIDTH), 0)
    total = u
    for k in range(1, max(POOL_WINDOWS)):
        first_group = sum(1 for w in POOL_WINDOWS if w <= k)
        shifted = pa_ext[POOL_HALO - k:POOL_HALO - k + rows, :]
        total = total + jnp.where(lane_group >= first_group, shifted, 0.0)
    window = jnp.left_shift(2, lane_group)
    cnt = jnp.minimum(pos + 1, window).astype(F32)
    pooled = total / cnt
    ya = _dot(pooled - u, poolw_ref[...]) * pools_ref[...]
    ya_ref[0] = ya
    pa_ext[0:POOL_HALO, :] = pa_ext[rows:rows + POOL_HALO, :]

    cu = pd[:, 0:G_WIDTH]
    cb = pd[:, G_WIDTH:2 * G_WIDTH]
    cc = pd[:, 2 * G_WIDTH:3 * G_WIDTH]
    z = cc * cu
    z_ext[CONV_HALO:, :] = z
    y = (convw_ref[0:1, :] * z_ext[CONV_HALO - 2:CONV_HALO - 2 + rows, :]
         + convw_ref[1:2, :] * z_ext[CONV_HALO - 1:CONV_HALO - 1 + rows, :]
         + convw_ref[2:3, :] * z)
    yd_ref[0] = cb * y
    z_ext[0:CONV_HALO, :] = z_ext[rows:rows + CONV_HALO, :]


def _pad_cols(w, width):
    return jnp.pad(w, ((0, 0), (0, width - w.shape[1])))


def _w_in_padded(w_in):
    w = w_in.astype(BF16)
    rwkv_end = G_WIDTH + 3 * G_WIDTH + RWKV_W_RANK + RWKV_A_RANK + RWKV_G_RANK
    nsa_end = rwkv_end + G_WIDTH + 6 * HEAD_DIM + 3 * N_HEADS
    zeros = lambda n: jnp.zeros((w.shape[0], n), BF16)
    return jnp.concatenate(
        [w[:, 0:rwkv_end], zeros(G_WIDTH + PB_COLS - rwkv_end),
         w[:, rwkv_end:nsa_end], zeros(PC_COLS - (nsa_end - rwkv_end)),
         w[:, nsa_end:]], axis=1)


def _mixin_call(x, g, w_in, pool_w, pool_scale, conv_w, k_norm_sel, k_norm_win):
    b, s, d = x.shape
    w = _w_in_padded(w_in)
    kn = jnp.concatenate([k_norm_sel, k_norm_win]).reshape(1, LANES)
    cmp_rows = MIX_ROWS // NSA_CMP_STRIDE
    key_steps = MIX_ROWS // KEY_STEP
    n_groups = len(POOL_WINDOWS)
    pool_ch = G_WIDTH // n_groups
    poolw = jnp.zeros((G_WIDTH, G_WIDTH), F32)
    for gi in range(n_groups):
        poolw = poolw.at[gi * pool_ch:(gi + 1) * pool_ch, gi * pool_ch:(gi + 1) * pool_ch].set(pool_w[gi])
    convw = jnp.pad(conv_w, ((0, 8 - conv_w.shape[0]), (0, 0)))

    def out_spec(c):
        return pl.BlockSpec((1, MIX_ROWS, c), lambda bi, si: (bi, si, 0))

    return pl.pallas_call(
        _mixin_kernel,
        grid=(b, s // MIX_ROWS),
        in_specs=[pl.BlockSpec((1, MIX_ROWS, d), lambda bi, si: (bi, si, 0)),
                  _const_spec((1, d)), _const_spec(w.shape),
                  _const_spec((G_WIDTH, G_WIDTH)), _const_spec((1, G_WIDTH)),
                  _const_spec((8, G_WIDTH)), _const_spec((1, LANES))],
        out_specs=[out_spec(G_WIDTH), out_spec(PB_COLS), out_spec(G_WIDTH), out_spec(LANES),
                   pl.BlockSpec((1, cmp_rows, NSA_CMP_STRIDE * LANES), lambda bi, si: (bi, si, 0)),
                   out_spec(2 * LANES),
                   pl.BlockSpec((1, key_steps, LANES, KEY_STEP), lambda bi, si: (bi, si, 0, 0)),
                   out_spec(G_WIDTH)],
        out_shape=[jax.ShapeDtypeStruct((b, s, G_WIDTH), F32),
                   jax.ShapeDtypeStruct((b, s, PB_COLS), F32),
                   jax.ShapeDtypeStruct((b, s, G_WIDTH), F32),
                   jax.ShapeDtypeStruct((b, s, LANES), F32),
                   jax.ShapeDtypeStruct((b, s // NSA_CMP_STRIDE, NSA_CMP_STRIDE * LANES), F32),
                   jax.ShapeDtypeStruct((b, s, 2 * LANES), BF16),
                   jax.ShapeDtypeStruct((b, s // KEY_STEP, LANES, KEY_STEP), BF16),
                   jax.ShapeDtypeStruct((b, s, G_WIDTH), F32)],
        scratch_shapes=[pltpu.VMEM((MIX_ROWS + POOL_HALO, G_WIDTH), F32),
                        pltpu.VMEM((MIX_ROWS + CONV_HALO, G_WIDTH), F32),
                        pltpu.VMEM((MIX_ROWS, LANES), F32)],
        compiler_params=_params(2),
    )(x, g.reshape(1, d), w, poolw.astype(BF16), pool_scale.reshape(1, G_WIDTH), convw, kn)


def _block_diag(x, mask01):
    return jnp.concatenate([x] * N_HEADS, axis=0) * mask01


def _rwkv_kernel(pb_ref, mu_ref, w0_ref, wup_ref, a0_ref, aup_ref, gup_ref, kk_ref, ka_ref,
                 rk_ref, lnw_ref, lnb_ref, o_ref,
                 ext, state, r_s, k_s, v_s, lw_s, a_s, b_s, y_s):
    si = pl.program_id(1)
    group, seq_rows = pb_ref.shape[0], pb_ref.shape[1]
    g_w = G_WIDTH
    chunk = RWKV_CHUNK

    @pl.when(si == 0)
    def _():
        ext[:, 0:8, :] = jnp.zeros((group, 8, PB_COLS), F32)
        state[...] = jnp.zeros_like(state)

    shifted = []
    for b in range(group):
        p_b = pb_ref[b]
        ext[b, 8:, :] = p_b
        prev = ext[b, 7:7 + seq_rows, :]
        shifted.append(p_b + mu_ref[...] * (prev - p_b))
        ext[b, 0:8, :] = ext[b, seq_rows:seq_rows + 8, :]
    ps = jnp.concatenate(shifted, axis=0)

    head_mask = _head_ones()
    head_ones = head_mask.astype(BF16)

    def head_sum(t):
        return _dot2_exact_rhs(t, head_ones)

    r = ps[:, 0:g_w]
    k = ps[:, g_w:2 * g_w]
    v = ps[:, 2 * g_w:3 * g_w]
    tail = ps[:, 3 * g_w:4 * g_w]
    w = w0_ref[...] + _dot(jnp.tanh(tail), wup_ref[...])
    lw = -jnp.exp(-0.5) * _sigmoid(w)
    a = _sigmoid(a0_ref[...] + _dot(tail, aup_ref[...]))
    gate = _dot(_sigmoid(tail), gup_ref[...])
    kk = k * kk_ref[...]
    kk = kk * lax.rsqrt(jnp.maximum(head_sum(kk * kk), 1e-24))
    k2 = k * (1.0 + (a - 1.0) * ka_ref[...])
    bonus = head_sum(r * k2 * rk_ref[...]) * v
    r_s[...] = r
    k_s[...] = k2
    v_s[...] = v
    lw_s[...] = lw
    a_s[...] = -kk
    b_s[...] = kk * a

    row_i = lax.broadcasted_iota(jnp.int32, (chunk, g_w), 0)
    col_j = lax.broadcasted_iota(jnp.int32, (chunk, g_w), 1) % chunk
    strict_lower = row_i > col_j
    lower = row_i >= col_j
    eye = jnp.where(row_i == col_j, 1.0, 0.0)
    tri = (lax.broadcasted_iota(jnp.int32, (chunk, chunk), 0)
           >= lax.broadcasted_iota(jnp.int32, (chunk, chunk), 1)).astype(BF16)
    n_doublings = chunk.bit_length() - 1

    head_mask16 = head_ones

    def bd(x):
        return _block_diag(x.astype(BF16), head_mask16)

    def mm(a, b16):
        return jnp.dot(a.astype(BF16), b16, preferred_element_type=F32)

    ids = range(group)

    def group_local(c):
        sls = [pl.ds(pl.multiple_of(i * seq_rows + c * chunk, chunk), chunk) for i in ids]
        r_c, k_c, v_c, lw_c, a_c, b_c = ([t[sl, :] for sl in sls]
                                         for t in (r_s, k_s, v_s, lw_s, a_s, b_s))
        cw = [_dot2_exact_lhs(tri, lw_c[i]) for i in ids]
        cw_last = [cw[i][chunk - 1:chunk, :] for i in ids]
        a_t = [a_c[i] * jnp.exp(cw[i] - lw_c[i]) for i in ids]
        r_t = [r_c[i] * jnp.exp(cw[i]) for i in ids]
        e_inv = [jnp.exp(-cw[i]) for i in ids]
        bd_v = [bd(v_c[i]) for i in ids]
        pair = [lax.dot_general(
            jnp.concatenate([a_t[i], r_t[i]], axis=0).astype(BF16),
            jnp.concatenate([bd(b_c[i] * e_inv[i]), bd(k_c[i] * e_inv[i])], axis=0),
            (((1,), (1,)), ((), ())), preferred_element_type=F32) for i in ids]
        l_ak = [jnp.where(strict_lower, pair[i][0:chunk, g_w:2 * g_w], 0.0) for i in ids]
        m_rb = [jnp.where(lower, pair[i][chunk:2 * chunk, 0:g_w], 0.0) for i in ids]
        m_rk = [jnp.where(lower, pair[i][chunk:2 * chunk, g_w:2 * g_w], 0.0) for i in ids]

        with_v = [mm(jnp.concatenate([l_ak[i], m_rk[i]], axis=0), bd_v[i]) for i in ids]
        z0 = [with_v[i][0:chunk] for i in ids]
        l_pow = [jnp.where(strict_lower, pair[i][0:chunk, 0:g_w], 0.0) for i in ids]
        t_inv = [eye + l_pow[i] for i in ids]
        l_pow = [mm(l_pow[i], bd(l_pow[i])) for i in ids]
        for it in range(1, n_doublings):
            last = it == n_doublings - 1
            res = [mm(l_pow[i], jnp.concatenate(
                [bd(t_inv[i])] + ([] if last else [bd(l_pow[i])]), axis=1)) for i in ids]
            t_inv = [t_inv[i] + res[i][:, 0:g_w] for i in ids]
            if not last:
                l_pow = [res[i][:, g_w:2 * g_w] for i in ids]
        sol = [mm(t_inv[i], jnp.concatenate([bd(a_t[i]), bd(z0[i])], axis=1)) for i in ids]
        x1 = [sol[i][:, 0:g_w] for i in ids]
        x2 = [sol[i][:, g_w:2 * g_w] for i in ids]

        q = [mm(m_rb[i], jnp.concatenate([bd(x1[i]), bd(x2[i])], axis=1)) for i in ids]
        q2 = [q[i][:, g_w:2 * g_w] + with_v[i][chunk:2 * chunk] for i in ids]
        lhs = [jnp.concatenate([x1[i], r_t[i] + q[i][:, 0:g_w]], axis=0).astype(BF16) for i in ids]
        rhs = []
        for i in ids:
            e_fut = jnp.exp(cw_last[i] - cw[i])
            rhs.append(jnp.concatenate([b_c[i] * e_fut, k_c[i] * e_fut], axis=0).astype(BF16))
        decay = [jnp.exp(cw_last[i]) for i in ids]

        s0 = [state[i] for i in ids]
        us = [lax.dot_general(lhs[i], s0[i].astype(BF16), (((1,), (1,)), ((), ())),
                              preferred_element_type=F32) for i in ids]
        uv_t = []
        for i in ids:
            y_s[sls[i], :] = us[i][chunk:2 * chunk] + q2[i]
            u = us[i][0:chunk] + x2[i]
            uv_t.append(jnp.transpose(jnp.concatenate([u, v_c[i]], axis=0)).astype(BF16))
        upd = [jnp.dot(uv_t[i], rhs[i], preferred_element_type=F32) for i in ids]
        for i in ids:
            state[i] = s0[i] * decay[i] + jnp.where(head_mask, upd[i], 0.0)

    def chunk_step(c, carry):
        group_local(c)
        return carry

    lax.fori_loop(0, seq_rows // chunk, chunk_step, 0)

    y = y_s[...]
    inv_n = 1.0 / HEAD_DIM
    mean = head_sum(y) * inv_n
    dev = y - mean
    var = head_sum(dev * dev) * inv_n
    yn = dev * lax.rsqrt(var + RWKV_LN_EPS) * lnw_ref[...] + lnb_ref[...]
    out = (yn + bonus) * gate
    for b in range(group):
        o_ref[b] = out[b * seq_rows:(b + 1) * seq_rows, :]


def _rwkv_call(pb, mu, w0, w_up, a0, a_up, g_up, k_k, k_a, r_k, ln_w, ln_b):
    b, s, _ = pb.shape
    g_w = G_WIDTH
    row = lambda t: t.reshape(1, g_w)
    o_a = RWKV_W_RANK
    o_g = o_a + RWKV_A_RANK
    wup = jnp.zeros((g_w, g_w), F32).at[0:o_a].set(w_up).astype(BF16)
    aup = jnp.zeros((g_w, g_w), F32).at[o_a:o_g].set(a_up).astype(BF16)
    gup = jnp.zeros((g_w, g_w), F32).at[o_g:o_g + RWKV_G_RANK].set(g_up).astype(BF16)
    mu_p = _pad_cols(mu.reshape(1, -1), PB_COLS)
    vec = _const_spec((1, g_w))
    mat = _const_spec((g_w, g_w))
    group = min(RWKV_GROUP, b)
    seq_rows = MIX_ROWS // group
    seq = pltpu.VMEM((MIX_ROWS, g_w), F32)
    return pl.pallas_call(
        _rwkv_kernel,
        grid=(b // group, s // seq_rows),
        in_specs=[pl.BlockSpec((group, seq_rows, PB_COLS), lambda bi, si: (bi, si, 0)),
                  _const_spec((1, PB_COLS)), vec, mat, vec, mat, mat, vec, vec, vec, vec, vec],
        out_specs=pl.BlockSpec((group, seq_rows, g_w), lambda bi, si: (bi, si, 0)),
        out_shape=jax.ShapeDtypeStruct((b, s, g_w), F32),
        scratch_shapes=[pltpu.VMEM((group, seq_rows + 8, PB_COLS), F32),
                        pltpu.VMEM((group, g_w, g_w), F32),
                        seq, seq, seq, seq, seq, seq, seq],
        compiler_params=_params(2),
    )(pb, mu_p, row(w0), wup, row(a0), aup, gup, row(k_k), row(k_a), row(r_k), row(ln_w), row(ln_b))


def _gelu_tanh(x):
    return 0.5 * x * (1.0 + jnp.tanh(0.7978845608028654 * (x + 0.044715 * x * x * x)))


def _pos_features(pos, width):
    lane = lax.broadcasted_iota(jnp.int32, (pos.shape[0], width), 1) // ALIBI_PARTS
    feat = jnp.where(lane == 0, pos // ALIBI_SPLIT,
                     jnp.where(lane == 1, pos % ALIBI_SPLIT, jnp.where(lane == 2, 1, 0)))
    return feat.astype(F32).astype(BF16)


def _bf16_parts(v):
    parts = []
    for _ in range(ALIBI_PARTS):
        head = v.astype(BF16).astype(F32)
        parts.append(head)
        v = v - head
    return parts


def _nsa_compress_kernel(x_ref, pos_ref, w_first_ref, w_second_ref, kw2_ref, vw2t_ref, kn_ref,
                         kcmp_ref, vcmpt_ref):
    n_rows = x_ref.shape[1]
    hid = kw2_ref.shape[0]
    x = x_ref[0]
    first = _dot(x + pos_ref[0:1, :], w_first_ref[...])
    second = _dot(x + pos_ref[1:2, :], w_second_ref[...])
    hidden = _gelu_tanh(first + pltpu.roll(second, n_rows - 1, 0))
    k_cmp = _dot(hidden[:, 0:hid], kw2_ref[...])
    kcmp_ref[0, :, 0:HEAD_DIM] = _rms_rows(k_cmp, kn_ref[...]).astype(BF16)
    last_token = (lax.broadcasted_iota(jnp.int32, (n_rows, 1), 0) * NSA_CMP_STRIDE
                  + (NSA_CMP_LEN - 1))
    kcmp_ref[0, :, HEAD_DIM:2 * HEAD_DIM] = _pos_features(last_token, HEAD_DIM)
    vcmpt_ref[0] = _dot_nt(vw2t_ref[...], hidden[:, hid:2 * hid]).astype(BF16)


def _interleave_kv(k_part, v_part):
    zeros = jnp.zeros_like(k_part)
    top = jnp.concatenate([k_part, zeros], axis=2)
    bottom = jnp.concatenate([zeros, v_part], axis=2)
    return jnp.concatenate([top, bottom], axis=1).reshape(NSA_CMP_STRIDE * LANES, -1)


def _nsa_compress_call(kcvc, pos, kw1, kw2, vw1, vw2, k_norm0):
    b, n_chunks, width = kcvc.shape
    hid = kw1.shape[1]
    split = lambda w1, half: w1.reshape(2, NSA_CMP_STRIDE, HEAD_DIM, hid)[half]
    w_first = _interleave_kv(split(kw1, 0), split(vw1, 0)).astype(BF16)
    w_second = _interleave_kv(split(kw1, 1), split(vw1, 1)).astype(BF16)
    pos_halves = pos.reshape(2, NSA_CMP_STRIDE, HEAD_DIM)
    pos2 = jnp.concatenate([pos_halves, pos_halves], axis=2).reshape(2, width)
    pos2 = jnp.pad(pos2, ((0, 6), (0, 0)))
    return pl.pallas_call(
        _nsa_compress_kernel,
        grid=(b,),
        in_specs=[pl.BlockSpec((1, n_chunks, width), lambda bi: (bi, 0, 0)), _const_spec((8, width)),
                  _const_spec((width, 2 * hid)), _const_spec((width, 2 * hid)),
                  _const_spec((hid, HEAD_DIM)), _const_spec((HEAD_DIM, hid)),
                  _const_spec((1, HEAD_DIM))],
        out_specs=[pl.BlockSpec((1, n_chunks, 2 * HEAD_DIM), lambda bi: (bi, 0, 0)),
                   pl.BlockSpec((1, HEAD_DIM, n_chunks), lambda bi: (bi, 0, 0))],
        out_shape=[jax.ShapeDtypeStruct((b, n_chunks, 2 * HEAD_DIM), BF16),
                   jax.ShapeDtypeStruct((b, HEAD_DIM, n_chunks), BF16)],
        compiler_params=_params(1),
    )(kcvc, pos2, w_first, w_second, kw2.astype(BF16), vw2.T.astype(BF16),
      k_norm0.reshape(1, HEAD_DIM))


def _col_reduce(x, pair_op, reduce_fn, slab=32):
    parts = [x[i:i + slab] for i in range(0, x.shape[0], slab)]
    while len(parts) > 1:
        parts = [pair_op(parts[i], parts[i + 1]) for i in range(0, len(parts), 2)]
    return reduce_fn(parts[0], axis=0, keepdims=True)


def _nsa_attn_kernel(q_ref, gate_ref, qn_ref, kc_ref, vct_ref, kk_ref, vt_ref, o_ref, s_ref):
    qi = pl.program_id(1)
    tq = ATT_ROWS
    hw = N_HEADS * tq
    q0 = qi * tq
    n_cmp_rows = kc_ref.shape[1]
    n_sel = kk_ref.shape[1] // NSA_SEL_BLOCK

    head_ones = _head_ones().astype(BF16)
    q = q_ref[0]
    ms = _dot2_exact_rhs(q * q, head_ones) * (1.0 / HEAD_DIM)
    qn = q * lax.rsqrt(ms + RMS_EPS) * qn_ref[...] * (LOG2E * HEAD_DIM ** -0.5)
    qt = jnp.transpose(qn)
    q4t = jnp.concatenate([qt[h * HEAD_DIM:(h + 1) * HEAD_DIM, :] for h in range(N_HEADS)],
                          axis=1).astype(BF16)
    lane = lax.broadcasted_iota(jnp.int32, (1, hw), 1)
    slope = jnp.exp2(-2.0 * (lane // tq + 1).astype(F32))
    t_lane = lane % tq

    coeffs = (ALIBI_SPLIT * LOG2E * slope, LOG2E * slope, -LOG2E * slope * q0.astype(F32))
    f_row = lax.broadcasted_iota(jnp.int32, (HEAD_DIM, hw), 0)
    q_feat = jnp.zeros((HEAD_DIM, hw), F32)
    for ci, coeff in enumerate(coeffs):
        for pi, part in enumerate(_bf16_parts(coeff)):
            q_feat = jnp.where(f_row == ci * ALIBI_PARTS + pi, part, q_feat)
    q_feat = q_feat.astype(BF16)
    zero_half = jnp.zeros_like(q4t)
    q_cmp = jnp.concatenate([q4t, q_feat], axis=0)
    q_sel = jnp.concatenate([q4t, zero_half, q_feat, zero_half], axis=0)
    q_win = jnp.concatenate([zero_half, q4t, q_feat, zero_half], axis=0)

    n_idx = lax.broadcasted_iota(jnp.int32, (n_cmp_rows, hw), 0)
    valid_c = (q0 + t_lane) >= (n_idx * NSA_CMP_STRIDE + (NSA_CMP_LEN - 1))
    s_c = (jnp.dot(kc_ref[0], q_cmp, preferred_element_type=F32)
           + jnp.where(valid_c, 0.0, NSA_NEG))
    m_c = jnp.maximum(jnp.max(s_c, axis=0, keepdims=True), SCORE_FLOOR)
    p_c = jnp.exp2(s_c - m_c)
    l_c = jnp.sum(p_c, axis=0, keepdims=True)
    p_c = p_c * (1.0 / jnp.maximum(l_c, 1e-30))
    o_cmp = jnp.dot(vct_ref[0], p_c.astype(BF16), preferred_element_type=F32)

    p_heads = p_c[:, 0:tq]
    for h in range(1, N_HEADS):
        p_heads = p_heads + p_c[:, h * tq:(h + 1) * tq]
    per_sel = NSA_SEL_BLOCK // NSA_CMP_STRIDE
    j_ov = lax.broadcasted_iota(jnp.int32, (n_sel, n_cmp_rows), 0)
    n_ov = lax.broadcasted_iota(jnp.int32, (n_sel, n_cmp_rows), 1)
    overlap_t = ((n_ov >= per_sel * j_ov - (NSA_CMP_LEN // NSA_CMP_STRIDE - 1))
                 & (n_ov <= per_sel * j_ov + per_sel - 1)).astype(BF16)
    p_hi = p_heads.astype(BF16)
    p_rest = p_heads - p_hi.astype(F32)
    p_mid = p_rest.astype(BF16)
    p_lo = (p_rest - p_mid.astype(F32)).astype(BF16)
    imp = (jnp.dot(overlap_t, p_hi, preferred_element_type=F32)
           + jnp.dot(overlap_t, p_mid, preferred_element_type=F32)
           + jnp.dot(overlap_t, p_lo, preferred_element_type=F32))
    j_idx = lax.broadcasted_iota(jnp.int32, (n_sel, tq), 0)
    cur = (q0 + lax.broadcasted_iota(jnp.int32, (1, tq), 1)) // NSA_SEL_BLOCK
    forced = (j_idx == 0) | (j_idx == cur) | (j_idx == cur - 1)
    imp = jnp.where(j_idx <= cur, imp + jnp.where(forced, NSA_FORCE_BONUS, 0.0), -1.0)
    imp_rows = [imp[g:g + SUBLANES, :] for g in range(0, n_sel, SUBLANES)]
    rank_rows = [jnp.zeros((SUBLANES, tq), F32) for _ in imp_rows]
    row_in_group = lax.broadcasted_iota(jnp.int32, (SUBLANES, tq), 0)
    for jp in range(n_sel):
        other = imp[jp:jp + 1, :]
        jp_group, jp_row = divmod(jp, SUBLANES)
        for g, mine in enumerate(imp_rows):
            if g > jp_group:
                ahead = other >= mine
            elif g < jp_group:
                ahead = other > mine
            else:
                ahead = (other > mine) | ((other == mine) & (row_in_group > jp_row))
            rank_rows[g] = rank_rows[g] + jnp.where(ahead, 1.0, 0.0)
    rank = jnp.concatenate(rank_rows, axis=0)
    chosen = jnp.where(rank < float(min(NSA_TOP_N, n_sel)), 0.0, NSA_NEG)

    key_i = lax.broadcasted_iota(jnp.int32, (tq, tq), 0)
    query_i = lax.broadcasted_iota(jnp.int32, (tq, tq), 1)
    causal = jnp.where(query_i >= key_i, 0.0, NSA_NEG)
    window_start = jnp.where(key_i > query_i, 0.0, NSA_NEG)
    tiles_per_step = KEY_STEP // tq
    blocks_per_tile = tq // NSA_SEL_BLOCK
    win_tiles = NSA_WINDOW // tq
    last_step = qi // tiles_per_step
    ones_rows = jnp.ones((16, KEY_STEP), BF16)

    def branch(n_steps, step_of, q_pad, v_rows, bias_fn):
        def prefetch(i, slot):
            p = step_of(i)
            k0 = pl.multiple_of(p * KEY_STEP, KEY_STEP)
            raw = jnp.dot(kk_ref[0, pl.ds(k0, KEY_STEP), :], q_pad, preferred_element_type=F32)
            s = raw + jnp.concatenate([bias_fn(p)] * N_HEADS, axis=1)
            s_ref[slot] = s
            return _col_reduce(s, jnp.maximum, jnp.max)

        def absorb(i, slot, s_max, m, l, acc):
            m_new = jnp.maximum(m, s_max)
            alpha = jnp.exp2(m - m_new)
            prob = jnp.exp2(s_ref[slot] - m_new).astype(BF16)
            v_t = jnp.concatenate([vt_ref[0, step_of(i)][v_rows[0]:v_rows[1], :], ones_rows], axis=0)
            pv = jnp.dot(v_t, prob, preferred_element_type=F32)
            l = alpha * l + pv[HEAD_DIM:HEAD_DIM + 1, :]
            return m_new, l, alpha * acc + pv[0:HEAD_DIM, :]

        def pair(i2, carry):
            max0, m, l, acc = carry
            i = 2 * i2
            max1 = prefetch(i + 1, 1)
            m, l, acc = absorb(i, 0, max0, m, l, acc)
            max0 = prefetch(i + 2, 0)
            m, l, acc = absorb(i + 1, 1, max1, m, l, acc)
            return max0, m, l, acc

        init = (prefetch(0, 0), jnp.full((1, hw), SCORE_FLOOR, F32), jnp.zeros((1, hw), F32),
                jnp.zeros((HEAD_DIM, hw), F32))
        n_pairs = (n_steps - 1) // 2
        max0, m, l, acc = lax.fori_loop(0, n_pairs, pair, init)
        i = 2 * n_pairs

        def two_left(m, l, acc):
            max1 = prefetch(i + 1, 1)
            m, l, acc = absorb(i, 0, max0, m, l, acc)
            return absorb(i + 1, 1, max1, m, l, acc)

        def one_left(m, l, acc):
            return absorb(i, 0, max0, m, l, acc)

        _, l, acc = lax.cond(n_steps - i == 2, two_left, one_left, m, l, acc)
        return acc * (1.0 / l)

    def sel_tile_bias(kt):
        rows = [jnp.max(jnp.where(j_idx == blocks_per_tile * kt + i, chosen, NSA_NEG),
                        axis=0, keepdims=True) for i in range(blocks_per_tile)]
        bias = rows[-1]
        for i in range(blocks_per_tile - 2, -1, -1):
            bias = jnp.where(key_i < (i + 1) * NSA_SEL_BLOCK, rows[i], bias)
        return bias

    def sel_step_bias(p):
        tiles = []
        for i in range(tiles_per_step):
            kt = p * tiles_per_step + i
            tiles.append(sel_tile_bias(kt) + jnp.where(kt == qi, causal, 0.0)
                         + jnp.where(kt > qi, NSA_NEG, 0.0))
        return jnp.concatenate(tiles, axis=0)

    def win_step_bias(p):
        tiles = []
        for i in range(tiles_per_step):
            behind = qi - (p * tiles_per_step + i)
            bias = jnp.where(behind == 0, causal, jnp.where(behind == win_tiles, window_start, 0.0))
            tiles.append(bias + jnp.where((behind >= 0) & (behind <= win_tiles), 0.0, NSA_NEG))
        return jnp.concatenate(tiles, axis=0)

    first_win_step = jnp.maximum(qi - win_tiles, 0) // tiles_per_step
    o_win = branch(last_step - first_win_step + 1, lambda i: first_win_step + i,
                   q_win, (HEAD_DIM, 2 * HEAD_DIM), win_step_bias)

    blocks_per_step = KEY_STEP // NSA_SEL_BLOCK
    block_id = lax.broadcasted_iota(jnp.int32, (n_sel, 1), 0)
    block_used = jnp.max(chosen, axis=1, keepdims=True) > SCORE_FLOOR
    earliest = jnp.min(jnp.where(block_used & (block_id >= blocks_per_step), block_id, n_sel),
                       axis=0, keepdims=True)[0, 0]
    first_sel_step = jnp.clip(earliest // blocks_per_step, 1, jnp.maximum(last_step, 1))
    n_sel_steps = jnp.where(last_step >= 1, last_step - first_sel_step + 2, 1)
    o_sel = branch(n_sel_steps, lambda i: jnp.where(i == 0, 0, first_sel_step + i - 1),
                   q_sel, (0, HEAD_DIM), sel_step_bias)

    g_t = jnp.transpose(_sigmoid(gate_ref[0]))

    def gate_row(c):
        return jnp.concatenate([g_t[3 * h + c:3 * h + c + 1, :] for h in range(N_HEADS)], axis=1)

    o = gate_row(0) * o_cmp + gate_row(1) * o_sel + gate_row(2) * o_win
    o_hd = jnp.concatenate([o[:, h * tq:(h + 1) * tq] for h in range(N_HEADS)], axis=0)
    o_ref[0] = jnp.transpose(o_hd)


def _nsa_attn_call(q, gates, q_norm, kcmp, vcmpt, kk, vt):
    b, s, _ = q.shape
    n_cmp_rows = kcmp.shape[1]
    qn = jnp.tile(q_norm, N_HEADS).reshape(1, G_WIDTH)
    return pl.pallas_call(
        _nsa_attn_kernel,
        grid=(b, s // ATT_ROWS),
        in_specs=[pl.BlockSpec((1, ATT_ROWS, G_WIDTH), lambda bi, qi: (bi, qi, 0)),
                  pl.BlockSpec((1, ATT_ROWS, LANES), lambda bi, qi: (bi, qi, 0)),
                  _const_spec((1, G_WIDTH)),
                  pl.BlockSpec((1, n_cmp_rows, 2 * HEAD_DIM), lambda bi, qi: (bi, 0, 0)),
                  pl.BlockSpec((1, HEAD_DIM, n_cmp_rows), lambda bi, qi: (bi, 0, 0)),
                  pl.BlockSpec((1, s, 2 * LANES), lambda bi, qi: (bi, 0, 0)),
                  pl.BlockSpec((1, s // KEY_STEP, LANES, KEY_STEP), lambda bi, qi: (bi, 0, 0, 0))],
        out_specs=pl.BlockSpec((1, ATT_ROWS, G_WIDTH), lambda bi, qi: (bi, qi, 0)),
        out_shape=jax.ShapeDtypeStruct((b, s, G_WIDTH), F32),
        scratch_shapes=[pltpu.VMEM((2, KEY_STEP, N_HEADS * ATT_ROWS), F32)],
        compiler_params=_params(2),
    )(q, gates, qn, kcmp, vcmpt, kk, vt)


def kernel(x, ffn1_norm, ffn1_w_gate, ffn1_w_up, ffn1_w_down, mix_norm, w_in, pool_w, pool_scale, rwkv_mu, rwkv_w0, rwkv_w_up, rwkv_a0, rwkv_a_up, rwkv_g_up, rwkv_k_k, rwkv_k_a, rwkv_r_k, rwkv_ln_w, rwkv_ln_b, nsa_q_norm, nsa_k_norm, nsa_cmp_pos, nsa_cmp_k_w1, nsa_cmp_k_w2, nsa_cmp_v_w1, nsa_cmp_v_w2, conv_w, w_out, ffn2_norm, ffn2_w_gate, ffn2_w_up, ffn2_w_down):
    b, s, d = x.shape
    n_tok = b * s
    assert s % MIX_ROWS == 0 and s % ATT_ROWS == 0 and s % KEY_STEP == 0 and n_tok % FFN_ROWS == 0
    assert b % min(RWKV_GROUP, b) == 0 and KEY_STEP % ATT_ROWS == 0
    x2d = x.reshape(n_tok, d)
    n_layers = ffn1_norm.shape[0]
    ffn1_w = [w.astype(BF16) for w in (ffn1_w_gate, ffn1_w_up, ffn1_w_down)]
    ffn2_w = [w.astype(BF16) for w in (ffn2_w_gate, ffn2_w_up, ffn2_w_down)]
    wo = w_out.astype(BF16).reshape(n_layers, N_MIXERS, G_WIDTH, d)
    for l in range(n_layers):
        x2d = _ffn_call(x2d, ffn1_norm[l], *ffn1_w, l)
        ya, pb, q, gates, kcvc, kk, vt, yd = _mixin_call(
            x2d.reshape(b, s, d), mix_norm[l], w_in[l], pool_w[l], pool_scale[l], conv_w[l],
            nsa_k_norm[l][1], nsa_k_norm[l][2])
        yb = _rwkv_call(pb, rwkv_mu[l], rwkv_w0[l], rwkv_w_up[l], rwkv_a0[l], rwkv_a_up[l],
                        rwkv_g_up[l], rwkv_k_k[l], rwkv_k_a[l], rwkv_r_k[l], rwkv_ln_w[l],
                        rwkv_ln_b[l])
        kcmp, vcmpt = _nsa_compress_call(kcvc, nsa_cmp_pos[l], nsa_cmp_k_w1[l], nsa_cmp_k_w2[l],
                                         nsa_cmp_v_w1[l], nsa_cmp_v_w2[l], nsa_k_norm[l][0])
        yc = _nsa_attn_call(q, gates, nsa_q_norm[l], kcmp, vcmpt, kk, vt)
        mix = tuple(t.reshape(n_tok, G_WIDTH) for t in (ya, yb, yc, yd))
        x2d = _ffn_call(x2d, ffn2_norm[l], *ffn2_w, l, mix=mix, wo=wo)
    return x2d.reshape(b, s, d)
```
